```python
import math
import jax
import jax.numpy as jnp
from jax import lax
import numpy as np

D_MODEL = 2048
BATCH = 16
SEQ = 256
DEPTH = 4
DEC_BATCH = 8
DEC_SEQ = 2048
PAST_LEN = 256

GRID_W = 64
N_EVEN = (DEPTH + 1) // 2
N_ODD = DEPTH // 2
EPS = 1e-6
F32 = jnp.float32

RET_HEADS = 4
RET_DK = D_MODEL // 8
RET_DV = D_MODEL // 8
RET_QK = RET_HEADS * RET_DK
RET_WIDTH = RET_HEADS * RET_DV
RET_CHUNK = 128

S5_WIDTH = D_MODEL - RET_WIDTH
S5_GROUP = 16
S5_GROUPS = S5_WIDTH // S5_GROUP
S5_STATE = 64
EVEN_IN = 2 * RET_QK + 2 * RET_WIDTH + S5_WIDTH

MLA_HEADS = 16
Q_LORA = 512
KV_LORA = 512
QK_NOPE = 128
QK_ROPE = 64
V_HEAD = 128
MLA_IN = Q_LORA + KV_LORA + QK_ROPE
ROPE_BASE = 10000.0
ATTN_BLOCK = 128

N_EXPERTS = 16
N_EXPERT_GROUPS = 4
EXPERTS_PER_GROUP = N_EXPERTS // N_EXPERT_GROUPS
TOP_K = 2
D_EXPERT = 512

kernel_name = 'hybrid_diffusion_prefix_step'


def rmsnorm(x, g):
    xf = x.astype(F32)
    y = xf * lax.rsqrt(jnp.mean(xf * xf, axis=-1, keepdims=True) + EPS)
    return (y * g.astype(F32)).astype(x.dtype)


def adaln(cond, w, b):
    m = jnp.dot(jax.nn.silu(cond), w) + b
    return jnp.split(m[:, None, :], 6, axis=-1)


def modulate(h, shift, scale):
    return h * (1 + scale) + shift


def grid_angles(n_tokens):
    rows = n_tokens // GRID_W
    row = jnp.repeat(jnp.arange(rows, dtype=F32), GRID_W)
    col = jnp.tile(jnp.arange(GRID_W, dtype=F32), rows)
    half = QK_ROPE // 2
    freqs = jnp.power(ROPE_BASE, -jnp.arange(0, half, 2, dtype=F32) / half)
    return row[:, None] * freqs, col[:, None] * freqs


def rope_1d(x, ang):
    x1, x2 = jnp.split(x.astype(F32), 2, axis=-1)
    cs, sn = jnp.cos(ang), jnp.sin(ang)
    return jnp.concatenate([x1 * cs - x2 * sn, x1 * sn + x2 * cs], axis=-1)


def axial_rope(x, ang_row, ang_col):
    xr, xc = jnp.split(x, 2, axis=-1)
    return jnp.concatenate([rope_1d(xr, ang_row), rope_1d(xc, ang_col)], axis=-1).astype(x.dtype)


def retention_scan(q, k, v, log_gamma, s0):
    B, L, H, DK = q.shape
    C = RET_CHUNK
    n = L // C
    qc = q.reshape(B, n, C, H, DK)
    kc = k.reshape(B, n, C, H, DK)
    vc = v.reshape(B, n, C, H, -1)
    pos = jnp.arange(C, dtype=F32)
    diff = pos[:, None] - pos[None, :]
    decay = jnp.where(diff >= 0, jnp.exp(log_gamma[:, None, None] * jnp.maximum(diff, 0.0)), 0.0)
    scores = jnp.einsum('bnihd,bnjhd->bnhij', qc, kc) * decay
    o_intra = jnp.einsum('bnhij,bnjhe->bnihe', scores, vc)
    q_dec = jnp.exp(log_gamma[None, :] * (pos[:, None] + 1.0))
    k_dec = jnp.exp(log_gamma[None, :] * (C - 1.0 - pos[:, None]))
    kv_chunk = jnp.einsum('bnjhd,jh,bnjhe->bnhde', kc, k_dec, vc)
    chunk_dec = jnp.exp(log_gamma * C)[None, :, None, None]

    def step(s, inp):
        q_n, kv_n = inp
        o = jnp.einsum('bihd,bhde->bihe', q_n, s)
        return s * chunk_dec + kv_n, o

    s_fin, o_cross = lax.scan(step, s0, (jnp.moveaxis(qc * q_dec[:, :, None], 1, 0), jnp.moveaxis(kv_chunk, 1, 0)))
    o = o_intra + jnp.moveaxis(o_cross, 0, 1)
    return o.reshape(B, L, H, -1), s_fin


def _linear_combine(e1, e2):
    a1, b1 = e1
    a2, b2 = e2
    return a1 * a2, a2 * b1 + b2


def s5_bidir(u, a_re, a_im, log_dt, b_re, b_im, c_re, c_im, d, w_glu, x0_re, x0_im):
    B, L, _ = u.shape
    uf = u.astype(F32)
    ug = uf.reshape(B, L, S5_GROUPS, S5_GROUP)
    y = d.astype(F32) * uf
    fin_re, fin_im = [], []
    for dr in range(2):
        lam = lax.complex(a_re[dr].astype(F32), a_im[dr].astype(F32))
        dt = jnp.exp(log_dt[dr].astype(F32))[:, None]
        lam_bar = jnp.exp(lam * dt)
        b_bar = ((lam_bar - 1.0) / lam)[..., None] * lax.complex(b_re[dr].astype(F32), b_im[dr].astype(F32))
        c_mat = lax.complex(c_re[dr].astype(F32), c_im[dr].astype(F32))
        useq = ug if dr == 0 else ug[:, ::-1]
        bu = jnp.einsum('blgc,gpc->blgp', useq.astype(jnp.complex64), b_bar)
        x0 = lax.complex(x0_re[:, dr].astype(F32), x0_im[:, dr].astype(F32))
        bu = bu.at[:, 0].add(lam_bar * x0)
        a = jnp.broadcast_to(lam_bar, (1, L) + lam_bar.shape)
        _, states = lax.associative_scan(_linear_combine, (a, bu), axis=1)
        yd = jnp.einsum('blgp,gcp->blgc', states, c_mat).real
        if dr == 1:
            yd = yd[:, ::-1]
        y = y + yd.reshape(B, L, S5_WIDTH)
        fin_re.append(states[:, -1].real)
        fin_im.append(states[:, -1].imag)
    z = jax.nn.gelu(y)
    out = z * jax.nn.sigmoid(jnp.dot(z, w_glu.astype(F32)))
    return out.astype(u.dtype), jnp.stack(fin_re, axis=1), jnp.stack(fin_im, axis=1)


def even_mixer(h, w_in, w_out, ret_decay, s5_params, ret_s0, s5_x0_re, s5_x0_im):
    B, L, _ = h.shape
    z = jnp.dot(h, w_in)
    q, k, v, g, u = jnp.split(z, [RET_QK, 2 * RET_QK, 2 * RET_QK + RET_WIDTH, 2 * RET_QK + 2 * RET_WIDTH], axis=-1)
    q = q.astype(F32).reshape(B, L, RET_HEADS, RET_DK) * (RET_DK ** -0.5)
    k = k.astype(F32).reshape(B, L, RET_HEADS, RET_DK)
    v = v.astype(F32).reshape(B, L, RET_HEADS, RET_DV)
    log_gamma = -jnp.exp(ret_decay.astype(F32))
    s0 = ret_s0.astype(F32)
    o_f, s_f = retention_scan(q, k, v, log_gamma[0], s0[:, 0])
    o_b, s_b = retention_scan(q[:, ::-1], k[:, ::-1], v[:, ::-1], log_gamma[1], s0[:, 1])
    o = o_f + o_b[:, ::-1]
    o = o * lax.rsqrt(jnp.mean(o * o, axis=-1, keepdims=True) + EPS)
    ret_out = jax.nn.silu(g) * o.reshape(B, L, RET_WIDTH).astype(h.dtype)
    s5_out, fin_re, fin_im = s5_bidir(u, *s5_params, s5_x0_re, s5_x0_im)
    out = jnp.dot(jnp.concatenate([ret_out, s5_out], axis=-1), w_out)
    return out, jnp.stack([s_f, s_b], axis=1), fin_re, fin_im


def mla_project(h, w_in, q_norm, w_uq, kv_norm):
    B, L, _ = h.shape
    a = jnp.dot(h, w_in)
    cq, ckv, k_rope = jnp.split(a, [Q_LORA, Q_LORA + KV_LORA], axis=-1)
    q = jnp.dot(rmsnorm(cq, q_norm), w_uq).reshape(B, L, MLA_HEADS, QK_NOPE + QK_ROPE)
    q_nope, q_rope = jnp.split(q, [QK_NOPE], axis=-1)
    return q_nope, q_rope, rmsnorm(ckv, kv_norm), k_rope


def mla_expand(ckv, w_ukv):
    B, L, _ = ckv.shape
    kv = jnp.dot(ckv, w_ukv).reshape(B, L, MLA_HEADS, QK_NOPE + V_HEAD)
    k_nope, v = jnp.split(kv, [QK_NOPE], axis=-1)
    return k_nope, v


def mla_attention(q_nope, q_rope, k_nope, k_rope, v):
    B, Lq, H, _ = q_nope.shape
    nb = Lq // ATTN_BLOCK
    scale = (QK_NOPE + QK_ROPE) ** -0.5

    def block(qs):
        qn, qr = qs
        s = jnp.einsum('bqhd,bkhd->bhqk', qn, k_nope) + jnp.einsum('bqhr,bkr->bhqk', qr, k_rope)
        p = jax.nn.softmax(s.astype(F32) * scale, axis=-1)
        return jnp.einsum('bhqk,bkhd->bqhd', p.astype(v.dtype), v)

    qn_b = q_nope.reshape(B, nb, ATTN_BLOCK, H, QK_NOPE).swapaxes(0, 1)
    qr_b = q_rope.reshape(B, nb, ATTN_BLOCK, H, QK_ROPE).swapaxes(0, 1)
    o = lax.map(block, (qn_b, qr_b))
    return o.swapaxes(0, 1).reshape(B, Lq, H * V_HEAD)


def mla_context(h, w_in, q_norm, w_uq, kv_norm, w_ukv, w_out):
    q_nope, q_rope, ckv, k_rope = mla_project(h, w_in, q_norm, w_uq, kv_norm)
    k_nope, v = mla_expand(ckv, w_ukv)
    o = mla_attention(q_nope, q_rope, k_nope, k_rope, v)
    return jnp.dot(o, w_out), ckv, k_rope


def mla_latent(h, ckv_ctx, krope_ctx, ang_row, ang_col, w_in, q_norm, w_uq, kv_norm, w_ukv, w_out):
    q_nope, q_rope, ckv, k_rope = mla_project(h, w_in, q_norm, w_uq, kv_norm)
    q_rope = axial_rope(q_rope, ang_row[:, None, :], ang_col[:, None, :])
    k_rope = axial_rope(k_rope, ang_row, ang_col)
    ckv_all = jnp.concatenate([ckv_ctx.astype(ckv.dtype), ckv], axis=1)
    kr_all = jnp.concatenate([krope_ctx.astype(k_rope.dtype), k_rope], axis=1)
    k_nope, v = mla_expand(ckv_all, w_ukv)
    o = mla_attention(q_nope, q_rope, k_nope, kr_all, v)
    return jnp.dot(o, w_out)


def moe(h, router_w, router_bias, w_gate, w_up, w_down):
    B, L, D = h.shape
    t = h.reshape(B * L, D)
    scores = jax.nn.sigmoid(jnp.dot(t, router_w).astype(F32))
    biased = scores + router_bias.astype(F32)
    grp_top = lax.top_k(biased.reshape(-1, N_EXPERT_GROUPS, EXPERTS_PER_GROUP), 2)[0]
    best_group = jnp.argmax(jnp.sum(grp_top, axis=-1), axis=-1)
    in_group = (jnp.arange(N_EXPERTS) // EXPERTS_PER_GROUP)[None, :] == best_group[:, None]
    _, idx = lax.top_k(jnp.where(in_group, biased, -jnp.inf), TOP_K)
    w = jnp.take_along_axis(scores, idx, axis=-1)
    w = w / jnp.sum(w, axis=-1, keepdims=True)
    gates = jnp.sum(jax.nn.one_hot(idx, N_EXPERTS, dtype=F32) * w[..., None], axis=1).astype(h.dtype)
    act = jax.nn.silu(jnp.einsum('td,edf->tef', t, w_gate)) * jnp.einsum('td,edf->tef', t, w_up)
    out = jnp.einsum('tef,efd->td', act * gates[:, :, None], w_down)
    return out.reshape(B, L, D)


def setup_inputs(seed: int = 0) -> dict:
    key = jax.random.key(seed)
    ks = jax.random.split(key, 40)

    def nrm(i, shape, scale):
        return jax.random.normal(ks[i], shape, F32) * scale

    ret_base = jnp.asarray(np.log(-np.log(1.0 - 2.0 ** (-5.0 - np.arange(RET_HEADS)))), F32)
    a_im_base = jnp.pi * jnp.arange(S5_STATE, dtype=F32)
    return {
        'x_prompt': nrm(0, (BATCH, SEQ, D_MODEL), 1.0),
        'x_sample': nrm(1, (DEC_BATCH, DEC_SEQ, D_MODEL), 1.0),
        'c': nrm(2, (DEC_BATCH, D_MODEL), 1.0),
        'state_ret': nrm(3, (DEC_BATCH, N_EVEN, 2, RET_HEADS, RET_DK, RET_DV), 1.0),
        'state_s5_re': nrm(4, (DEC_BATCH, N_EVEN, 2, S5_GROUPS, S5_STATE), 0.1),
        'state_s5_im': nrm(5, (DEC_BATCH, N_EVEN, 2, S5_GROUPS, S5_STATE), 0.1),
        'cache_ckv': nrm(6, (DEC_BATCH, N_ODD, PAST_LEN, KV_LORA), 1.0),
        'cache_krope': nrm(7, (DEC_BATCH, N_ODD, PAST_LEN, QK_ROPE), 1.0),
        'c_ctx': nrm(8, (D_MODEL,), 1.0),
        'ada_w': nrm(9, (DEPTH, D_MODEL, 6 * D_MODEL), 0.5 * D_MODEL ** -0.5),
        'ada_b': nrm(10, (DEPTH, 6 * D_MODEL), 0.02),
        'norm_mix': 1.0 + nrm(11, (DEPTH, D_MODEL), 0.02),
        'norm_ffn': 1.0 + nrm(12, (DEPTH, D_MODEL), 0.02),
        'norm_final': 1.0 + nrm(13, (D_MODEL,), 0.02),
        'even_w_in': nrm(14, (N_EVEN, D_MODEL, EVEN_IN), D_MODEL ** -0.5),
        'even_w_out': nrm(15, (N_EVEN, RET_WIDTH + S5_WIDTH, D_MODEL), (RET_WIDTH + S5_WIDTH) ** -0.5),
        'ret_decay': ret_base + nrm(16, (N_EVEN, 2, RET_HEADS), 0.1),
        's5_a_re': -0.5 + nrm(17, (N_EVEN, 2, S5_GROUPS, S5_STATE), 0.01),
        's5_a_im': a_im_base + nrm(18, (N_EVEN, 2, S5_GROUPS, S5_STATE), 0.01),
        's5_log_dt': jax.random.uniform(ks[19], (N_EVEN, 2, S5_GROUPS), F32, math.log(1e-3), math.log(1e-1)),
        's5_b_re': nrm(20, (N_EVEN, 2, S5_GROUPS, S5_STATE, S5_GROUP), (2 * S5_GROUP) ** -0.5),
        's5_b_im': nrm(21, (N_EVEN, 2, S5_GROUPS, S5_STATE, S5_GROUP), (2 * S5_GROUP) ** -0.5),
        's5_c_re': nrm(22, (N_EVEN, 2, S5_GROUPS, S5_GROUP, S5_STATE), (2 * S5_STATE) ** -0.5),
        's5_c_im': nrm(23, (N_EVEN, 2, S5_GROUPS, S5_GROUP, S5_STATE), (2 * S5_STATE) ** -0.5),
        's5_d': nrm(24, (N_EVEN, S5_WIDTH), 1.0),
        's5_w_glu': nrm(25, (N_EVEN, S5_WIDTH, S5_WIDTH), S5_WIDTH ** -0.5),
        'mla_w_in': nrm(26, (N_ODD, D_MODEL, MLA_IN), D_MODEL ** -0.5),
        'mla_q_norm': 1.0 + nrm(27, (N_ODD, Q_LORA), 0.02),
        'mla_w_uq': nrm(28, (N_ODD, Q_LORA, MLA_HEADS * (QK_NOPE + QK_ROPE)), Q_LORA ** -0.5),
        'mla_kv_norm': 1.0 + nrm(29, (N_ODD, KV_LORA), 0.02),
        'mla_w_ukv': nrm(30, (N_ODD, KV_LORA, MLA_HEADS * (QK_NOPE + V_HEAD)), KV_LORA ** -0.5),
        'mla_w_out': nrm(31, (N_ODD, MLA_HEADS * V_HEAD, D_MODEL), (MLA_HEADS * V_HEAD) ** -0.5),
        'router_w': nrm(32, (D_MODEL, N_EXPERTS), D_MODEL ** -0.5),
        'router_bias': nrm(33, (N_EXPERTS,), 0.01),
        'moe_w_gate': nrm(34, (DEPTH, N_EXPERTS, D_MODEL, D_EXPERT), D_MODEL ** -0.5),
        'moe_w_up': nrm(35, (DEPTH, N_EXPERTS, D_MODEL, D_EXPERT), D_MODEL ** -0.5),
        'moe_w_down': nrm(36, (DEPTH, N_EXPERTS, D_EXPERT, D_MODEL), D_EXPERT ** -0.5),
    }


def reference(x_prompt, x_sample, c, state_ret, state_s5_re, state_s5_im, cache_ckv, cache_krope,
              c_ctx, ada_w, ada_b, norm_mix, norm_ffn, norm_final,
              even_w_in, even_w_out, ret_decay, s5_a_re, s5_a_im, s5_log_dt, s5_b_re, s5_b_im,
              s5_c_re, s5_c_im, s5_d, s5_w_glu,
              mla_w_in, mla_q_norm, mla_w_uq, mla_kv_norm, mla_w_ukv, mla_w_out,
              router_w, router_bias, moe_w_gate, moe_w_up, moe_w_down):
    xc = x_prompt
    xs = x_sample
    bp = x_prompt.shape[0]
    ang_row, ang_col = grid_angles(x_sample.shape[1])
    zero_ret = jnp.zeros((bp, 2, RET_HEADS, RET_DK, RET_DV), F32)
    zero_s5 = jnp.zeros((bp, 2, S5_GROUPS, S5_STATE), F32)
    rets, s5r, s5i, ckvs, krs = [], [], [], [], []
    for layer in range(DEPTH):
        mc = adaln(c_ctx[None, :], ada_w[layer], ada_b[layer])
        ms = adaln(c, ada_w[layer], ada_b[layer])
        hc = modulate(rmsnorm(xc, norm_mix[layer]), mc[0], mc[1])
        hs = modulate(rmsnorm(xs, norm_mix[layer]), ms[0], ms[1])
        if layer % 2 == 0:
            i = layer // 2
            s5p = (s5_a_re[i], s5_a_im[i], s5_log_dt[i], s5_b_re[i], s5_b_im[i],
                   s5_c_re[i], s5_c_im[i], s5_d[i], s5_w_glu[i])
            oc, r_fin, f_re, f_im = even_mixer(hc, even_w_in[i], even_w_out[i], ret_decay[i], s5p,
                                               zero_ret, zero_s5, zero_s5)
            os_, _, _, _ = even_mixer(hs, even_w_in[i], even_w_out[i], ret_decay[i], s5p,
                                      state_ret[:, i], state_s5_re[:, i], state_s5_im[:, i])
            rets.append(r_fin)
            s5r.append(f_re)
            s5i.append(f_im)
        else:
            j = layer // 2
            oc, ckv_c, kr_c = mla_context(hc, mla_w_in[j], mla_q_norm[j], mla_w_uq[j], mla_kv_norm[j],
                                          mla_w_ukv[j], mla_w_out[j])
            os_ = mla_latent(hs, cache_ckv[:, j], cache_krope[:, j], ang_row, ang_col, mla_w_in[j],
                             mla_q_norm[j], mla_w_uq[j], mla_kv_norm[j], mla_w_ukv[j], mla_w_out[j])
            ckvs.append(ckv_c)
            krs.append(kr_c)
        xc = xc + mc[2] * oc
        xs = xs + ms[2] * os_
        xc = xc + mc[5] * moe(modulate(rmsnorm(xc, norm_ffn[layer]), mc[3], mc[4]), router_w, router_bias,
                              moe_w_gate[layer], moe_w_up[layer], moe_w_down[layer])
        xs = xs + ms[5] * moe(modulate(rmsnorm(xs, norm_ffn[layer]), ms[3], ms[4]), router_w, router_bias,
                              moe_w_gate[layer], moe_w_up[layer], moe_w_down[layer])
    y_prompt = rmsnorm(xc, norm_final)
    y_sample = rmsnorm(xs, norm_final)
    new_state_ret = jnp.stack(rets, axis=1)
    new_state_s5_re = jnp.stack(s5r, axis=1)
    new_state_s5_im = jnp.stack(s5i, axis=1)
    new_cache_ckv = jnp.stack(ckvs, axis=1)
    new_cache_krope = jnp.stack(krs, axis=1)
    return (y_prompt, y_sample, new_state_ret, new_state_s5_re, new_state_s5_im, new_cache_ckv, new_cache_krope)
```

```python
import functools
import math

import jax
import jax.numpy as jnp
from jax import lax
from jax.experimental import pallas as pl
from jax.experimental.pallas import tpu as pltpu

F32 = jnp.float32
BF16 = jnp.bfloat16
EPS = 1e-6

RET_HEADS = 4
RET_CHUNK = 128
S5_GROUP = 16
S5_Q = 16
MLA_HEADS = 16
QK_NOPE = 128
QK_ROPE = 64
V_HEAD = 128
QK_PAD = 256
GRID_W = 64
ROPE_BASE = 10000.0
N_EXPERT_GROUPS = 4
N_MOD = 6

VMEM_LIMIT = 56 * 1024 * 1024
TM = 512
TM_MOE = 256


def _cp(sem, vmem=VMEM_LIMIT):
    return pltpu.CompilerParams(dimension_semantics=sem, vmem_limit_bytes=vmem)


def _sigmoid(x):
    return 1.0 / (1.0 + jnp.exp(-x))


def _silu(x):
    return x * _sigmoid(x)


def _gelu_tanh(x):
    return 0.5 * x * (1.0 + jnp.tanh(math.sqrt(2.0 / math.pi) * (x + 0.044715 * (x * x * x))))


def _dot(a, b):
    return jnp.dot(a, b, preferred_element_type=F32)


def _dot_nt(a, b):
    return lax.dot_general(a, b, (((1,), (1,)), ((), ())), preferred_element_type=F32)


def _dot_tn(a, b):
    return lax.dot_general(a, b, (((0,), (0,)), ((), ())), preferred_element_type=F32)


class _Dims:
    def __init__(self, x_prompt, x_sample):
        self.bc, self.lc, self.d = x_prompt.shape
        self.bs, self.ls, _ = x_sample.shape
        self.tc = self.bc * self.lc
        self.ts = self.bs * self.ls
        self.t = self.tc + self.ts
        self.rp = -(-(1 + self.bs) // 8) * 8

    def mod_row(self, i, tm):
        nct = self.tc // tm
        return jnp.where(i < nct, 0, 1 + (i - nct) // (self.ls // tm))


def _mod_idx(dm, layer, k, tm):
    def idx(i, *_):
        return ((layer * dm.rp + dm.mod_row(i, tm)) * N_MOD + k, 0, 0)
    return idx


def _adaln_kernel(c_ref, w_ref, b_ref, o_ref):
    cs = _silu(c_ref[...]).astype(BF16)
    o_ref[0] = _dot(cs, w_ref[0].astype(BF16)) + b_ref[0]


def _adaln(cond, ada_w, ada_b, tn):
    depth, d, n = ada_w.shape
    rp = cond.shape[0]
    return pl.pallas_call(
        _adaln_kernel,
        grid=(depth, n // tn),
        in_specs=[pl.BlockSpec((rp, d), lambda l, j: (0, 0)),
                  pl.BlockSpec((1, d, tn), lambda l, j: (l, 0, j)),
                  pl.BlockSpec((1, 1, tn), lambda l, j: (l, 0, j))],
        out_specs=pl.BlockSpec((1, rp, tn), lambda l, j: (l, 0, j)),
        out_shape=jax.ShapeDtypeStruct((depth, rp, n), F32),
        compiler_params=_cp(("parallel", "parallel")),
        name="adaln",
    )(cond, ada_w, ada_b.reshape(depth, 1, n))


def _nmm_kernel(x_ref, g_ref, sh_ref, sc_ref, w_ref, o_ref, hn_ref):
    @pl.when(pl.program_id(1) == 0)
    def _():
        x = x_ref[...]
        y = x * lax.rsqrt(jnp.mean(x * x, axis=-1, keepdims=True) + EPS) * g_ref[0]
        hn_ref[...] = (y * (1.0 + sc_ref[0]) + sh_ref[0]).astype(BF16)

    o_ref[...] = _dot(hn_ref[...], w_ref[...]).astype(o_ref.dtype)


def _norm_mod_matmul(dm, x, gain, mods, layer, k_shift, w, out_dtype, tn, name):
    t, d = x.shape
    n = w.shape[1]
    return pl.pallas_call(
        _nmm_kernel,
        grid=(t // TM, n // tn),
        in_specs=[pl.BlockSpec((TM, d), lambda i, j: (i, 0)),
                  pl.BlockSpec((1, 1, d), lambda i, j: (layer, 0, 0)),
                  pl.BlockSpec((1, 1, d), _mod_idx(dm, layer, k_shift, TM)),
                  pl.BlockSpec((1, 1, d), _mod_idx(dm, layer, k_shift + 1, TM)),
                  pl.BlockSpec((d, tn), lambda i, j: (0, j))],
        out_specs=pl.BlockSpec((TM, tn), lambda i, j: (i, j)),
        out_shape=jax.ShapeDtypeStruct((t, n), out_dtype),
        scratch_shapes=[pltpu.VMEM((TM, d), BF16)],
        compiler_params=_cp(("parallel", "arbitrary")),
        name=name,
    )(x, gain, mods, mods, w)


def _mmres_kernel(n_in, x_ref, gate_ref, *refs):
    a_refs, w_refs, o_ref = refs[:n_in], refs[n_in:2 * n_in], refs[2 * n_in]
    acc = _dot(a_refs[0][...], w_refs[0][...])
    for a_ref, w_ref in zip(a_refs[1:], w_refs[1:]):
        acc += _dot(a_ref[...], w_ref[...])
    o_ref[...] = x_ref[...] + gate_ref[0] * acc


def _matmul_residual(dm, x, mods, layer, k_gate, acts, ws, tn, name):
    t, d = x.shape
    n_in = len(acts)

    def gate_idx(i, j):
        return ((layer * dm.rp + dm.mod_row(i, TM)) * N_MOD + k_gate, 0, j)

    in_specs = [pl.BlockSpec((TM, tn), lambda i, j: (i, j)),
                pl.BlockSpec((1, 1, tn), gate_idx)]
    in_specs += [pl.BlockSpec((TM, a.shape[1]), lambda i, j: (i, 0)) for a in acts]
    in_specs += [pl.BlockSpec((w.shape[0], tn), lambda i, j: (0, j)) for w in ws]
    return pl.pallas_call(
        functools.partial(_mmres_kernel, n_in),
        grid=(t // TM, d // tn),
        in_specs=in_specs,
        out_specs=pl.BlockSpec((TM, tn), lambda i, j: (i, j)),
        out_shape=jax.ShapeDtypeStruct((t, d), F32),
        compiler_params=_cp(("parallel", "parallel")),
        name=name,
    )(x, mods, *acts, *ws)


def _ret_kernel(has_s0, nch, scale, lg_ref, q_ref, k_ref, v_ref, g_ref, *refs):
    if has_s0:
        s0_ref, o_ref, sfin_ref, oacc_ref, st_ref = refs
    else:
        o_ref, sfin_ref, oacc_ref, st_ref = refs
    c = RET_CHUNK
    h = pl.program_id(1)
    lgf = lg_ref[0, h]
    lgb = lg_ref[1, h]
    dv = v_ref.shape[1]

    ii = lax.broadcasted_iota(jnp.int32, (c, c), 0).astype(F32)
    jj = lax.broadcasted_iota(jnp.int32, (c, c), 1).astype(F32)
    diff = ii - jj
    dtot = (jnp.where(diff >= 0, jnp.exp(lgf * jnp.maximum(diff, 0.0)), 0.0)
            + jnp.where(diff <= 0, jnp.exp(lgb * jnp.maximum(-diff, 0.0)), 0.0)) * scale
    pos = lax.broadcasted_iota(jnp.int32, (c, 1), 0).astype(F32)
    qdec_f = jnp.exp(lgf * (pos + 1.0)) * scale
    kdec_f = jnp.exp(lgf * (c - 1.0 - pos))
    qdec_b = jnp.exp(lgb * (c - pos)) * scale
    kdec_b = jnp.exp(lgb * pos)
    cdec_f = jnp.exp(jnp.full((1, dv), lgf * c, F32))
    cdec_b = jnp.exp(jnp.full((1, dv), lgb * c, F32))

    if has_s0:
        st_ref[...] = s0_ref[0, 0, 0]
    else:
        st_ref[...] = jnp.zeros_like(st_ref)

    def fwd(n, carry):
        r = pl.multiple_of(n * c, c)
        qn = q_ref[pl.ds(r, c), :]
        kn = k_ref[pl.ds(r, c), :]
        vn = v_ref[pl.ds(r, c), :]
        p = (_dot_nt(qn, kn) * dtot).astype(BF16)
        o = _dot(p, vn)
        o += _dot((qn.astype(F32) * qdec_f).astype(BF16), st_ref[...].astype(BF16))
        oacc_ref[pl.ds(r, c), :] = o
        kd = (kn.astype(F32) * kdec_f).astype(BF16)
        st_ref[...] = st_ref[...] * cdec_f + _dot_tn(kd, vn)
        return carry

    lax.fori_loop(0, nch, fwd, 0)
    sfin_ref[0, 0, 0] = st_ref[...]

    if has_s0:
        st_ref[...] = s0_ref[0, 1, 0]
    else:
        st_ref[...] = jnp.zeros_like(st_ref)

    def bwd(m, carry):
        r = pl.multiple_of((nch - 1 - m) * c, c)
        qn = q_ref[pl.ds(r, c), :]
        kn = k_ref[pl.ds(r, c), :]
        vn = v_ref[pl.ds(r, c), :]
        oacc_ref[pl.ds(r, c), :] += _dot((qn.astype(F32) * qdec_b).astype(BF16), st_ref[...].astype(BF16))
        kd = (kn.astype(F32) * kdec_b).astype(BF16)
        st_ref[...] = st_ref[...] * cdec_b + _dot_tn(kd, vn)
        return carry

    lax.fori_loop(0, nch, bwd, 0)
    sfin_ref[0, 1, 0] = st_ref[...]

    o = oacc_ref[...]
    o = o * lax.rsqrt(jnp.mean(o * o, axis=-1, keepdims=True) + EPS)
    o_ref[...] = (_silu(g_ref[...].astype(F32)) * o).astype(o_ref.dtype)


def _retention(z, dk, log_gamma, s0, nb, seq, row_blk0, name):
    hh = RET_HEADS
    dv = dk
    nch = seq // RET_CHUNK
    has_s0 = s0 is not None
    scale = float(dk) ** -0.5

    def col(off):
        return lambda b, h: (row_blk0 + b, off + h)

    in_specs = [pl.BlockSpec(memory_space=pltpu.SMEM),
                pl.BlockSpec((seq, dk), col(0)),
                pl.BlockSpec((seq, dk), col(hh)),
                pl.BlockSpec((seq, dv), col(2 * hh)),
                pl.BlockSpec((seq, dv), col(3 * hh))]
    args = [log_gamma, z, z, z, z]
    if has_s0:
        in_specs.append(pl.BlockSpec((1, 2, 1, dk, dv), lambda b, h: (b, 0, h, 0, 0)))
        args.append(s0)
    return pl.pallas_call(
        functools.partial(_ret_kernel, has_s0, nch, scale),
        grid=(nb, hh),
        in_specs=in_specs,
        out_specs=[pl.BlockSpec((seq, dv), lambda b, h: (b, h)),
                   pl.BlockSpec((1, 2, 1, dk, dv), lambda b, h: (b, 0, h, 0, 0))],
        out_shape=[jax.ShapeDtypeStruct((nb * seq, hh * dv), BF16),
                   jax.ShapeDtypeStruct((nb, 2, hh, dk, dv), F32)],
        scratch_shapes=[pltpu.VMEM((seq, dv), F32), pltpu.VMEM((dk, dv), F32)],
        compiler_params=_cp(("parallel", "parallel")),
        name=name,
    )(*args)


def _s5_kernel(rc, ncc, bc, ncs, bs, u_ref, w1_ref, v_ref, lam_ref, d_ref, x0_ref,
               y_ref, fin_ref, sre, sim, are, aim, bre, bim):
    u = u_ref[0]
    z = _dot(u, w1_ref[0])
    nq = u.shape[1]
    y_ref[0] = z[:, :nq] + d_ref[0] * u.astype(F32)
    sre[...] = z[:, nq:nq + 128]
    sim[...] = z[:, nq + 128:nq + 256]
    lr = lam_ref[0, 0:1, :]
    li = lam_ref[0, 1:2, :]

    def scan(row0, nc, b, xr0, xi0):
        is_f = lax.broadcasted_iota(jnp.int32, (b, 128), 1) < 64

        def body(s, carry):
            xr, xi = carry
            rf = pl.multiple_of(row0 + s * b, b)
            rb = pl.multiple_of(row0 + (nc - 1 - s) * b, b)
            are[pl.ds(rf, b), :] = xr
            aim[pl.ds(rf, b), :] = xi
            bre[pl.ds(rb, b), :] = xr
            bim[pl.ds(rb, b), :] = xi
            sr = jnp.where(is_f, sre[pl.ds(rf, b), :], sre[pl.ds(rb, b), :])
            si = jnp.where(is_f, sim[pl.ds(rf, b), :], sim[pl.ds(rb, b), :])
            return xr * lr - xi * li + sr, xi * lr + xr * li + si

        return lax.fori_loop(0, nc, body, (xr0, xi0))

    zero = jnp.zeros((bc, 128), F32)
    fr, fi = scan(0, ncc, bc, zero, zero)
    fin_ref[0, 0] = fr
    fin_ref[0, 1] = fi
    scan(rc, ncs, bs, x0_ref[0, 0], x0_ref[0, 1])
    y_ref[0] += (_dot(are[...].astype(BF16), v_ref[0, 0]) + _dot(aim[...].astype(BF16), v_ref[0, 1])
                 + _dot(bre[...].astype(BF16), v_ref[0, 2]) + _dot(bim[...].astype(BF16), v_ref[0, 3]))


def _s5_operators(a_re, a_im, log_dt, b_re, b_im, c_re, c_im):
    q = S5_Q
    hp = lax.Precision.HIGHEST
    lam = lax.complex(a_re.astype(F32), a_im.astype(F32))
    dt = jnp.exp(log_dt.astype(F32))[..., None]
    lamdt = lam * dt
    lam_bar = jnp.exp(lamdt)
    b_bar = ((lam_bar - 1.0) / lam)[..., None] * lax.complex(b_re.astype(F32), b_im.astype(F32))
    c_mat = lax.complex(c_re.astype(F32), c_im.astype(F32))
    g, p = lam.shape[1], lam.shape[2]
    cg = b_bar.shape[-1]
    steps = jnp.arange(q + 1, dtype=F32)
    pw = jnp.exp(lamdt[..., None] * steps)
    kern = jnp.einsum('dgcp,dgpt,dgpe->dgtce', c_mat, pw[..., :q], b_bar, precision=hp).real
    ti = jnp.arange(q)
    lag = ti[None, :] - ti[:, None]
    m_f = jnp.where((lag >= 0)[None, :, :, None, None], kern[0][:, jnp.clip(lag, 0, q - 1)], 0.0)
    m_b = jnp.where((lag <= 0)[None, :, :, None, None], kern[1][:, jnp.clip(-lag, 0, q - 1)], 0.0)
    m_tot = jnp.transpose(m_f + m_b, (0, 1, 4, 2, 3)).reshape(g, q * cg, q * cg)
    wf = jnp.einsum('gpj,gpe->gjep', pw[0][..., q - 1 - ti], b_bar[0]).reshape(g, q * cg, p)
    wb = jnp.einsum('gpj,gpe->gjep', pw[1][..., ti], b_bar[1]).reshape(g, q * cg, p)
    w1 = jnp.concatenate([m_tot, wf.real, wb.real, wf.imag, wb.imag], axis=-1)
    vf = jnp.einsum('gcp,gpi->gpic', c_mat[0], pw[0][..., ti + 1]).reshape(g, p, q * cg)
    vb = jnp.einsum('gcp,gpi->gpic', c_mat[1], pw[1][..., q - ti]).reshape(g, p, q * cg)
    zf = jnp.zeros_like(vf.real)
    v = jnp.stack([jnp.concatenate([vf.real, zf], axis=1), jnp.concatenate([-vf.imag, zf], axis=1),
                   jnp.concatenate([zf, vb.real], axis=1), jnp.concatenate([zf, -vb.imag], axis=1)], axis=1)
    lam_q = pw[..., q]
    lam_pack = jnp.stack([jnp.concatenate([lam_q[0].real, lam_q[1].real], axis=-1),
                          jnp.concatenate([lam_q[0].imag, lam_q[1].imag], axis=-1)], axis=1)
    return w1.astype(BF16), v.astype(BF16), lam_pack


def _s5(dm, u_tok, ops, d_skip, x0_re, x0_im):
    w1, v, lam_pack = ops
    g = w1.shape[0]
    q, cg = S5_Q, S5_GROUP
    ncc, ncs = dm.lc // q, dm.ls // q
    rc, rs = ncc * dm.bc, ncs * dm.bs
    r = rc + rs

    def to_group_major(x, b, nc):
        return x.reshape(b, nc, q, g, cg).transpose(3, 1, 0, 2, 4).reshape(g, nc * b, q * cg)

    u_g = jnp.concatenate([to_group_major(u_tok[:dm.tc], dm.bc, ncc),
                           to_group_major(u_tok[dm.tc:], dm.bs, ncs)], axis=1)
    d_g = jnp.tile(d_skip.astype(F32).reshape(g, 1, cg), (1, q, 1)).reshape(g, 1, q * cg)
    x0 = jnp.stack([jnp.concatenate([x0_re[:, 0], x0_re[:, 1]], axis=-1),
                    jnp.concatenate([x0_im[:, 0], x0_im[:, 1]], axis=-1)], axis=0)
    x0 = x0.transpose(2, 0, 1, 3).astype(F32)
    nw = w1.shape[2]
    y_g, fin = pl.pallas_call(
        functools.partial(_s5_kernel, rc, ncc, dm.bc, ncs, dm.bs),
        grid=(g,),
        in_specs=[pl.BlockSpec((1, r, q * cg), lambda i: (i, 0, 0)),
                  pl.BlockSpec((1, q * cg, nw), lambda i: (i, 0, 0)),
                  pl.BlockSpec((1, 4, 128, q * cg), lambda i: (i, 0, 0, 0)),
                  pl.BlockSpec((1, 2, 128), lambda i: (i, 0, 0)),
                  pl.BlockSpec((1, 1, q * cg), lambda i: (i, 0, 0)),
                  pl.BlockSpec((1, 2, dm.bs, 128), lambda i: (i, 0, 0, 0))],
        out_specs=[pl.BlockSpec((1, r, q * cg), lambda i: (i, 0, 0)),
                   pl.BlockSpec((1, 2, dm.bc, 128), lambda i: (i, 0, 0, 0))],
        out_shape=[jax.ShapeDtypeStruct((g, r, q * cg), F32),
                   jax.ShapeDtypeStruct((g, 2, dm.bc, 128), F32)],
        scratch_shapes=[pltpu.VMEM((r, 128), F32)] * 6,
        compiler_params=_cp(("parallel",)),
        name="s5",
    )(u_g, w1, v, lam_pack, d_g, x0)

    def to_token_major(x, b, nc):
        return x.reshape(g, nc, b, q, cg).transpose(2, 1, 3, 0, 4).reshape(b * nc * q, g * cg)

    y_tok = jnp.concatenate([to_token_major(y_g[:, :rc], dm.bc, ncc),
                             to_token_major(y_g[:, rc:], dm.bs, ncs)], axis=0)
    p = 64
    fin = fin.transpose(2, 1, 0, 3)
    fin_re = jnp.stack([fin[:, 0, :, :p], fin[:, 0, :, p:]], axis=1)
    fin_im = jnp.stack([fin[:, 1, :, :p], fin[:, 1, :, p:]], axis=1)
    return y_tok, fin_re, fin_im


def _glu_kernel(y_ref, w_ref, o_ref):
    z = _gelu_tanh(y_ref[...])
    o_ref[...] = (z * _sigmoid(_dot(z.astype(BF16), w_ref[...]))).astype(o_ref.dtype)


def _glu(y, w):
    t, n = y.shape
    return pl.pallas_call(
        _glu_kernel,
        grid=(t // TM,),
        in_specs=[pl.BlockSpec((TM, n), lambda i: (i, 0)),
                  pl.BlockSpec((n, n), lambda i: (0, 0))],
        out_specs=pl.BlockSpec((TM, n), lambda i: (i, 0)),
        out_shape=jax.ShapeDtypeStruct((t, n), BF16),
        compiler_params=_cp(("parallel",)),
        name="s5_glu",
    )(y, w)


def _rot(x, ct, st):
    return x * ct + pltpu.roll(x, 64, 1) * st


def _mla_q_kernel(nq, cq_ref, ckv_ref, kr_ref, qn_ref, kvn_ref, ct_ref, st_ref, w_ref,
                  q_ref, ckvn_ref, kro_ref):
    ct = ct_ref[...]
    st = st_ref[...]
    cq = cq_ref[...]
    cqn = (cq * lax.rsqrt(jnp.mean(cq * cq, axis=-1, keepdims=True) + EPS) * qn_ref[...]).astype(BF16)
    for h in range(MLA_HEADS):
        qh = _dot(cqn, w_ref[:, h * QK_PAD:(h + 1) * QK_PAD])
        q_ref[:, h * QK_PAD:h * QK_PAD + QK_NOPE] = qh[:, :QK_NOPE].astype(BF16)
        q_ref[:, h * QK_PAD + QK_NOPE:(h + 1) * QK_PAD] = _rot(qh[:, QK_NOPE:], ct, st).astype(BF16)
    ckv = ckv_ref[...]
    ckvn_ref[...] = ckv * lax.rsqrt(jnp.mean(ckv * ckv, axis=-1, keepdims=True) + EPS) * kvn_ref[...]
    kro_ref[...] = _rot(kr_ref[...], ct, st)


def _mla_q(a, q_norm, kv_norm, ct, st, w_uq_ext, q_lora, kv_lora):
    t = a.shape[0]
    nq = w_uq_ext.shape[1]
    kr_blk = (q_lora + kv_lora) // 128
    return pl.pallas_call(
        functools.partial(_mla_q_kernel, nq),
        grid=(t // TM,),
        in_specs=[pl.BlockSpec((TM, q_lora), lambda i: (i, 0)),
                  pl.BlockSpec((TM, kv_lora), lambda i: (i, q_lora // kv_lora)),
                  pl.BlockSpec((TM, 128), lambda i: (i, kr_blk)),
                  pl.BlockSpec((1, q_lora), lambda i: (0, 0)),
                  pl.BlockSpec((1, kv_lora), lambda i: (0, 0)),
                  pl.BlockSpec((TM, 128), lambda i: (i, 0)),
                  pl.BlockSpec((TM, 128), lambda i: (i, 0)),
                  pl.BlockSpec((q_lora, nq), lambda i: (0, 0))],
        out_specs=[pl.BlockSpec((TM, nq), lambda i: (i, 0)),
                   pl.BlockSpec((TM, kv_lora), lambda i: (i, 0)),
                   pl.BlockSpec((TM, 128), lambda i: (i, 0))],
        out_shape=[jax.ShapeDtypeStruct((t, nq), BF16),
                   jax.ShapeDtypeStruct((t, kv_lora), F32),
                   jax.ShapeDtypeStruct((t, 128), F32)],
        compiler_params=_cp(("parallel",)),
        name="mla_q",
    )(a, a, a, q_norm, kv_norm, ct, st, w_uq_ext)


def _kv_expand_kernel(c_ref, kr_ref, wk_ref, wv_ref, k_ref, v_ref):
    c = c_ref[...].astype(BF16)
    kn = _dot(c, wk_ref[...]).astype(BF16)
    kr = kr_ref[...].astype(BF16)
    for h in range(MLA_HEADS):
        k_ref[:, h * QK_PAD:h * QK_PAD + QK_NOPE] = kn[:, h * QK_NOPE:(h + 1) * QK_NOPE]
        k_ref[:, h * QK_PAD + QK_NOPE:(h + 1) * QK_PAD] = kr
    v_ref[...] = _dot(c, wv_ref[...]).astype(BF16)


def _kv_expand(ckv_keys, kr_keys, w_uk, w_uv):
    nk, kv_lora = ckv_keys.shape
    return pl.pallas_call(
        _kv_expand_kernel,
        grid=(nk // TM,),
        in_specs=[pl.BlockSpec((TM, kv_lora), lambda i: (i, 0)),
                  pl.BlockSpec((TM, 128), lambda i: (i, 0)),
                  pl.BlockSpec(w_uk.shape, lambda i: (0, 0)),
                  pl.BlockSpec(w_uv.shape, lambda i: (0, 0))],
        out_specs=[pl.BlockSpec((TM, MLA_HEADS * QK_PAD), lambda i: (i, 0)),
                   pl.BlockSpec((TM, MLA_HEADS * V_HEAD), lambda i: (i, 0))],
        out_shape=[jax.ShapeDtypeStruct((nk, MLA_HEADS * QK_PAD), BF16),
                   jax.ShapeDtypeStruct((nk, MLA_HEADS * V_HEAD), BF16)],
        compiler_params=_cp(("parallel",)),
        name="mla_kv_expand",
    )(ckv_keys, kr_keys, w_uk, w_uv)


def _attn_kernel(scale, q_ref, k_ref, v_ref, o_ref):
    s = _dot_nt(q_ref[...], k_ref[...]) * scale
    e = jnp.exp(s - jnp.max(s, axis=-1, keepdims=True))
    l = jnp.sum(e, axis=-1, keepdims=True)
    o_ref[...] = (_dot(e.astype(BF16), v_ref[...]) / l).astype(o_ref.dtype)


def _attention(q, k, v, nb, lq, lk, tq, q_row0, k_row0, name):
    scale = float(QK_NOPE + QK_ROPE) ** -0.5
    nqt = lq // tq
    qb0, kb0 = q_row0 // tq, k_row0 // lk
    return pl.pallas_call(
        functools.partial(_attn_kernel, scale),
        grid=(nb, MLA_HEADS, nqt),
        in_specs=[pl.BlockSpec((tq, QK_PAD), lambda b, h, i: (qb0 + b * nqt + i, h)),
                  pl.BlockSpec((lk, QK_PAD), lambda b, h, i: (kb0 + b, h)),
                  pl.BlockSpec((lk, V_HEAD), lambda b, h, i: (kb0 + b, h))],
        out_specs=pl.BlockSpec((tq, V_HEAD), lambda b, h, i: (b * nqt + i, h)),
        out_shape=jax.ShapeDtypeStruct((nb * lq, MLA_HEADS * V_HEAD), BF16),
        compiler_params=_cp(("parallel", "parallel", "arbitrary")),
        name=name,
    )(q, k, v)


def _route_kernel(n_exp, x_ref, g_ref, sh_ref, sc_ref, rw_ref, rb_ref, h_ref, idx_ref, wt_ref):
    x = x_ref[...]
    y = x * lax.rsqrt(jnp.mean(x * x, axis=-1, keepdims=True) + EPS) * g_ref[0]
    h = y * (1.0 + sc_ref[0]) + sh_ref[0]
    h_ref[...] = h
    scores = _sigmoid(_dot_nt(rw_ref[...], h.astype(BF16)))
    biased = scores + rb_ref[...]
    per = n_exp // N_EXPERT_GROUPS
    assert per == 4
    rows_b = [biased[e:e + 1, :] for e in range(n_exp)]
    rows_s = [scores[e:e + 1, :] for e in range(n_exp)]
    best_sum, best_g = None, None
    for gi in range(N_EXPERT_GROUPS):
        a, b, c, d = rows_b[per * gi:per * gi + per]
        hi1, lo1 = jnp.maximum(a, b), jnp.minimum(a, b)
        hi2, lo2 = jnp.maximum(c, d), jnp.minimum(c, d)
        top2 = jnp.maximum(hi1, hi2) + jnp.maximum(jnp.minimum(hi1, hi2), jnp.maximum(lo1, lo2))
        if gi == 0:
            best_sum, best_g = top2, jnp.zeros_like(top2, dtype=jnp.int32)
        else:
            upd = top2 > best_sum
            best_sum = jnp.where(upd, top2, best_sum)
            best_g = jnp.where(upd, gi, best_g)
    vb, vs = [], []
    for k in range(per):
        accb, accs = rows_b[k], rows_s[k]
        for gi in range(1, N_EXPERT_GROUPS):
            sel = best_g == gi
            accb = jnp.where(sel, rows_b[per * gi + k], accb)
            accs = jnp.where(sel, rows_s[per * gi + k], accs)
        vb.append(accb)
        vs.append(accs)

    def first_argmax(vals, exclude):
        bv, bi, bs = None, None, None
        for k in range(per):
            v = vals[k] if exclude is None else jnp.where(exclude == k, -jnp.inf, vals[k])
            if k == 0:
                bv, bi, bs = v, jnp.zeros_like(best_g), vs[0]
            else:
                upd = v > bv
                bv = jnp.where(upd, v, bv)
                bi = jnp.where(upd, k, bi)
                bs = jnp.where(upd, vs[k], bs)
        return bi, bs

    i1, s1 = first_argmax(vb, None)
    i2, s2 = first_argmax(vb, i1)
    tot = s1 + s2
    idx_ref[0:1, :] = best_g * per + i1
    idx_ref[1:2, :] = best_g * per + i2
    wt_ref[0:1, :] = s1 / tot
    wt_ref[1:2, :] = s2 / tot


def _route(dm, x, gain, mods, layer, rw_t, rb):
    t, d = x.shape
    n_exp = rw_t.shape[0]
    return pl.pallas_call(
        functools.partial(_route_kernel, n_exp),
        grid=(t // TM,),
        in_specs=[pl.BlockSpec((TM, d), lambda i: (i, 0)),
                  pl.BlockSpec((1, 1, d), lambda i: (layer, 0, 0)),
                  pl.BlockSpec((1, 1, d), _mod_idx(dm, layer, 3, TM)),
                  pl.BlockSpec((1, 1, d), _mod_idx(dm, layer, 4, TM)),
                  pl.BlockSpec((n_exp, d), lambda i: (0, 0)),
                  pl.BlockSpec((n_exp, 1), lambda i: (0, 0))],
        out_specs=[pl.BlockSpec((TM, d), lambda i: (i, 0)),
                   pl.BlockSpec((2, TM), lambda i: (0, i)),
                   pl.BlockSpec((2, TM), lambda i: (0, i))],
        out_shape=[jax.ShapeDtypeStruct((t, d), F32),
                   jax.ShapeDtypeStruct((2, t), jnp.int32),
                   jax.ShapeDtypeStruct((2, t), F32)],
        compiler_params=_cp(("parallel",)),
        name="moe_route",
    )(x, gain, mods, mods, rw_t, rb)


def _slot_plan(idx, wts, n_exp, n_slots):
    t = idx.shape[1]
    e_flat = idx.reshape(-1)
    onehot = (e_flat[:, None] == jnp.arange(n_exp, dtype=jnp.int32)[None, :]).astype(jnp.int32)
    counts = jnp.sum(onehot, axis=0)
    rank = jnp.sum((jnp.cumsum(onehot, axis=0) - onehot) * onehot, axis=1)
    padded = ((counts + TM_MOE - 1) // TM_MOE) * TM_MOE
    ends = jnp.cumsum(padded)
    starts = ends - padded
    dest = (starts[e_flat] + rank).astype(jnp.int32)
    tok = jnp.tile(jnp.arange(t, dtype=jnp.int32), 2)
    src = jnp.zeros((n_slots,), jnp.int32).at[dest].set(tok)
    w_sorted = jnp.zeros((n_slots,), F32).at[dest].set(wts.reshape(-1))
    tile_start = jnp.arange(n_slots // TM_MOE, dtype=jnp.int32) * TM_MOE
    tile_exp = jnp.minimum(jnp.searchsorted(ends, tile_start, side='right'), n_exp - 1).astype(jnp.int32)
    tile_ok = (tile_start < ends[-1]).astype(jnp.int32)
    return src, dest, w_sorted, tile_exp, tile_ok


def _gather_kernel(src_ref, h_ref, o_ref, sem):
    base = pl.program_id(0) * TM_MOE

    def row_copy(r):
        return pltpu.make_async_copy(h_ref.at[pl.ds(src_ref[base + r], 1)], o_ref.at[pl.ds(r, 1)], sem)

    def start(r, c):
        row_copy(r).start()
        return c

    def wait(r, c):
        row_copy(r).wait()
        return c

    lax.fori_loop(0, TM_MOE, start, 0)
    lax.fori_loop(0, TM_MOE, wait, 0)


def _gather_rows(h, src):
    n_slots = src.shape[0]
    d = h.shape[1]
    return pl.pallas_call(
        _gather_kernel,
        grid_spec=pltpu.PrefetchScalarGridSpec(
            num_scalar_prefetch=1,
            grid=(n_slots // TM_MOE,),
            in_specs=[pl.BlockSpec(memory_space=pl.ANY)],
            out_specs=pl.BlockSpec((TM_MOE, d), lambda i, src: (i, 0)),
            scratch_shapes=[pltpu.SemaphoreType.DMA(())]),
        out_shape=jax.ShapeDtypeStruct((n_slots, d), h.dtype),
        compiler_params=_cp(("arbitrary",)),
        name="moe_gather",
    )(src, h)


def _expert_kernel(te_ref, ok_ref, x_ref, wt_ref, wg_ref, wu_ref, wd_ref, o_ref, wg_b, wu_b, wd_b):
    i = pl.program_id(0)
    new_expert = jnp.logical_or(i == 0, te_ref[i] != te_ref[jnp.maximum(i - 1, 0)])

    @pl.when(new_expert)
    def _():
        wg_b[...] = wg_ref[0, 0].astype(BF16)
        wu_b[...] = wu_ref[0, 0].astype(BF16)
        wd_b[...] = wd_ref[0, 0].astype(BF16)

    @pl.when(ok_ref[i] == 1)
    def _():
        x = x_ref[...].astype(BF16)
        act = _silu(_dot(x, wg_b[...])) * _dot(x, wu_b[...]) * wt_ref[...]
        o_ref[...] = _dot(act.astype(BF16), wd_b[...])

    @pl.when(ok_ref[i] == 0)
    def _():
        o_ref[...] = jnp.zeros_like(o_ref)


def _experts(x_sorted, w_sorted, tile_exp, tile_ok, w_gate, w_up, w_down, layer):
    n_slots, d = x_sorted.shape
    f = w_gate.shape[-1]
    return pl.pallas_call(
        _expert_kernel,
        grid_spec=pltpu.PrefetchScalarGridSpec(
            num_scalar_prefetch=2,
            grid=(n_slots // TM_MOE,),
            in_specs=[pl.BlockSpec((TM_MOE, d), lambda i, te, ok: (i, 0)),
                      pl.BlockSpec((TM_MOE, 1), lambda i, te, ok: (i, 0)),
                      pl.BlockSpec((1, 1, d, f), lambda i, te, ok: (layer, te[i], 0, 0)),
                      pl.BlockSpec((1, 1, d, f), lambda i, te, ok: (layer, te[i], 0, 0)),
                      pl.BlockSpec((1, 1, f, d), lambda i, te, ok: (layer, te[i], 0, 0))],
            out_specs=pl.BlockSpec((TM_MOE, d), lambda i, te, ok: (i, 0)),
            scratch_shapes=[pltpu.VMEM((d, f), BF16), pltpu.VMEM((d, f), BF16), pltpu.VMEM((f, d), BF16)]),
        out_shape=jax.ShapeDtypeStruct((n_slots, d), F32),
        compiler_params=_cp(("arbitrary",)),
        name="moe_experts",
    )(tile_exp, tile_ok, x_sorted, w_sorted.reshape(n_slots, 1), w_gate, w_up, w_down)


def _combine_kernel(t_total, dest_ref, x_ref, gate_ref, y_ref, o_ref, y0, y1, sem):
    base = pl.program_id(0) * TM

    def copies(r):
        return (pltpu.make_async_copy(y_ref.at[pl.ds(dest_ref[base + r], 1)], y0.at[pl.ds(r, 1)], sem),
                pltpu.make_async_copy(y_ref.at[pl.ds(dest_ref[t_total + base + r], 1)], y1.at[pl.ds(r, 1)], sem))

    def start(r, c):
        for cp in copies(r):
            cp.start()
        return c

    def wait(r, c):
        for cp in copies(r):
            cp.wait()
        return c

    lax.fori_loop(0, TM, start, 0)
    lax.fori_loop(0, TM, wait, 0)
    o_ref[...] = x_ref[...] + gate_ref[0] * (y0[...] + y1[...])


def _combine(dm, x, mods, layer, y_sorted, dest):
    t, d = x.shape
    return pl.pallas_call(
        functools.partial(_combine_kernel, t),
        grid_spec=pltpu.PrefetchScalarGridSpec(
            num_scalar_prefetch=1,
            grid=(t // TM,),
            in_specs=[pl.BlockSpec((TM, d), lambda i, dst: (i, 0)),
                      pl.BlockSpec((1, 1, d), lambda i, dst: _mod_idx(dm, layer, 5, TM)(i)),
                      pl.BlockSpec(memory_space=pl.ANY)],
            out_specs=pl.BlockSpec((TM, d), lambda i, dst: (i, 0)),
            scratch_shapes=[pltpu.VMEM((TM, d), F32), pltpu.VMEM((TM, d), F32),
                            pltpu.SemaphoreType.DMA(())]),
        out_shape=jax.ShapeDtypeStruct((t, d), F32),
        compiler_params=_cp(("arbitrary",)),
        name="moe_combine",
    )(dest, x, mods, y_sorted)


def _moe(dm, x, gain, mods, layer, rw_t, rb, w_gate, w_up, w_down):
    n_exp = rw_t.shape[0]
    h, idx, wts = _route(dm, x, gain, mods, layer, rw_t, rb)
    n_slots = 2 * dm.t + n_exp * TM_MOE
    src, dest, w_sorted, tile_exp, tile_ok = _slot_plan(idx, wts, n_exp, n_slots)
    x_sorted = _gather_rows(h, src)
    y_sorted = _experts(x_sorted, w_sorted, tile_exp, tile_ok, w_gate, w_up, w_down, layer)
    return _combine(dm, x, mods, layer, y_sorted, dest)


def _final_norm_kernel(x_ref, g_ref, o_ref):
    x = x_ref[...]
    o_ref[...] = x * lax.rsqrt(jnp.mean(x * x, axis=-1, keepdims=True) + EPS) * g_ref[...]


def _final_norm(x, gain):
    t, d = x.shape
    return pl.pallas_call(
        _final_norm_kernel,
        grid=(t // TM,),
        in_specs=[pl.BlockSpec((TM, d), lambda i: (i, 0)),
                  pl.BlockSpec((1, d), lambda i: (0, 0))],
        out_specs=pl.BlockSpec((TM, d), lambda i: (i, 0)),
        out_shape=jax.ShapeDtypeStruct((t, d), F32),
        compiler_params=_cp(("parallel",)),
        name="final_norm",
    )(x, gain.reshape(1, d))


def _rope_tables(dm):
    rows = dm.ls // GRID_W
    row = jnp.repeat(jnp.arange(rows, dtype=F32), GRID_W)
    col = jnp.tile(jnp.arange(GRID_W, dtype=F32), rows)
    half = QK_ROPE // 2
    freqs = jnp.power(ROPE_BASE, -jnp.arange(0, half, 2, dtype=F32) / half)
    ar, ac = row[:, None] * freqs, col[:, None] * freqs
    zeros = jnp.zeros((dm.ls, 128 - QK_ROPE), F32)
    ct = jnp.concatenate([jnp.cos(ar), jnp.cos(ar), jnp.cos(ac), jnp.cos(ac), zeros], axis=-1)
    st = jnp.concatenate([-jnp.sin(ar), jnp.sin(ar), -jnp.sin(ac), jnp.sin(ac), zeros], axis=-1)
    ct_c = jnp.concatenate([jnp.ones((dm.tc, QK_ROPE), F32), jnp.zeros((dm.tc, 128 - QK_ROPE), F32)], axis=-1)
    ct = jnp.concatenate([ct_c, jnp.tile(ct, (dm.bs, 1))], axis=0)
    st = jnp.concatenate([jnp.zeros((dm.tc, 128), F32), jnp.tile(st, (dm.bs, 1))], axis=0)
    return ct, st


def _swap_halves_cols(w):
    qt = QK_ROPE // 4
    return jnp.concatenate([w[..., qt:2 * qt], w[..., :qt], w[..., 3 * qt:], w[..., 2 * qt:3 * qt]], axis=-1)


def kernel(x_prompt, x_sample, c, state_ret, state_s5_re, state_s5_im, cache_ckv, cache_krope, c_ctx, ada_w, ada_b, norm_mix, norm_ffn, norm_final, even_w_in, even_w_out, ret_decay, s5_a_re, s5_a_im, s5_log_dt, s5_b_re, s5_b_im, s5_c_re, s5_c_im, s5_d, s5_w_glu, mla_w_in, mla_q_norm, mla_w_uq, mla_kv_norm, mla_w_ukv, mla_w_out, router_w, router_bias, moe_w_gate, moe_w_up, moe_w_down):
    dm = _Dims(x_prompt, x_sample)
    d = dm.d
    depth = ada_w.shape[0]
    n_exp = router_w.shape[1]
    past = cache_ckv.shape[2]
    q_lora = mla_q_norm.shape[1]
    kv_lora = mla_kv_norm.shape[1]

    x = jnp.concatenate([x_prompt.reshape(dm.tc, d), x_sample.reshape(dm.ts, d)], axis=0)
    cond = jnp.zeros((dm.rp, d), F32).at[0].set(c_ctx).at[1:1 + dm.bs].set(c)
    mods = _adaln(cond, ada_w, ada_b, tn=d * N_MOD // 8).reshape(depth * dm.rp * N_MOD, 1, d)
    g_mix = norm_mix.reshape(depth, 1, d)
    g_ffn = norm_ffn.reshape(depth, 1, d)
    rw_t = router_w.T.astype(BF16)
    rb = router_bias.astype(F32).reshape(n_exp, 1)
    ct, st = _rope_tables(dm)

    rets, s5r, s5i, ckvs, krs = [], [], [], [], []
    for layer in range(depth):
        if layer % 2 == 0:
            i = layer // 2
            ret_w = even_w_out.shape[1] - s5_d.shape[1]
            z = _norm_mod_matmul(dm, x, g_mix, mods, layer, 0, even_w_in[i].astype(BF16), BF16,
                                 tn=even_w_in.shape[2] // 5, name="even_in_proj")
            log_gamma = -jnp.exp(ret_decay[i].astype(F32))
            ro_c, sfin = _retention(z, ret_w // RET_HEADS, log_gamma, None, dm.bc, dm.lc, 0, "retention_ctx")
            ro_s, _ = _retention(z, ret_w // RET_HEADS, log_gamma, state_ret[:, i].astype(F32), dm.bs, dm.ls, dm.tc // dm.ls,
                                 "retention_lat")
            ret_out = jnp.concatenate([ro_c, ro_s], axis=0)
            ops = _s5_operators(s5_a_re[i], s5_a_im[i], s5_log_dt[i], s5_b_re[i], s5_b_im[i],
                                s5_c_re[i], s5_c_im[i])
            y, f_re, f_im = _s5(dm, z[:, 4 * ret_w:], ops, s5_d[i], state_s5_re[:, i], state_s5_im[:, i])
            s5_out = _glu(y, s5_w_glu[i].astype(BF16))
            w_out = even_w_out[i].astype(BF16)
            x = _matmul_residual(dm, x, mods, layer, 2, [ret_out, s5_out], [w_out[:ret_w], w_out[ret_w:]],
                                 tn=d // 2, name="even_out_proj")
            rets.append(sfin)
            s5r.append(f_re)
            s5i.append(f_im)
        else:
            j = layer // 2
            w_in = mla_w_in[j]
            w_in_ext = jnp.concatenate([w_in, _swap_halves_cols(w_in[:, q_lora + kv_lora:])], axis=1).astype(BF16)
            a = _norm_mod_matmul(dm, x, g_mix, mods, layer, 0, w_in_ext, F32, tn=w_in_ext.shape[1],
                                 name="mla_in_proj")
            w_uq = mla_w_uq[j].reshape(q_lora, MLA_HEADS, QK_NOPE + QK_ROPE)
            w_uq_ext = jnp.concatenate([w_uq, _swap_halves_cols(w_uq[..., QK_NOPE:])], axis=-1)
            w_uq_ext = w_uq_ext.reshape(q_lora, MLA_HEADS * QK_PAD).astype(BF16)
            q, ckv_n, kr = _mla_q(a, mla_q_norm[j].reshape(1, q_lora), mla_kv_norm[j].reshape(1, kv_lora),
                                  ct, st, w_uq_ext, q_lora, kv_lora)
            ckv_lat = jnp.concatenate([cache_ckv[:, j].astype(F32), ckv_n[dm.tc:].reshape(dm.bs, dm.ls, kv_lora)], axis=1)
            kr_cache = jnp.concatenate([cache_krope[:, j].astype(F32),
                                        jnp.zeros((dm.bs, past, 128 - QK_ROPE), F32)], axis=-1)
            kr_lat = jnp.concatenate([kr_cache, kr[dm.tc:].reshape(dm.bs, dm.ls, 128)], axis=1)
            lk = past + dm.ls
            ckv_keys = jnp.concatenate([ckv_lat.reshape(dm.bs * lk, kv_lora), ckv_n[:dm.tc]], axis=0)
            kr_keys = jnp.concatenate([kr_lat.reshape(dm.bs * lk, 128), kr[:dm.tc]], axis=0)
            w_ukv = mla_w_ukv[j].reshape(kv_lora, MLA_HEADS, QK_NOPE + V_HEAD)
            w_uk = w_ukv[..., :QK_NOPE].reshape(kv_lora, MLA_HEADS * QK_NOPE).astype(BF16)
            w_uv = w_ukv[..., QK_NOPE:].reshape(kv_lora, MLA_HEADS * V_HEAD).astype(BF16)
            k_all, v_all = _kv_expand(ckv_keys, kr_keys, w_uk, w_uv)
            o_c = _attention(q, k_all, v_all, dm.bc, dm.lc, dm.lc, dm.lc, 0, dm.bs * lk, "attn_ctx")
            o_s = _attention(q, k_all, v_all, dm.bs, dm.ls, lk, min(TM, dm.ls), dm.tc, 0, "attn_lat")
            o = jnp.concatenate([o_c, o_s], axis=0)
            x = _matmul_residual(dm, x, mods, layer, 2, [o], [mla_w_out[j].astype(BF16)], tn=d // 2,
                                 name="mla_out_proj")
            ckvs.append(ckv_n[:dm.tc].reshape(dm.bc, dm.lc, kv_lora))
            krs.append(kr[:dm.tc, :QK_ROPE].reshape(dm.bc, dm.lc, QK_ROPE))
        x = _moe(dm, x, g_ffn, mods, layer, rw_t, rb, moe_w_gate, moe_w_up, moe_w_down)

    y = _final_norm(x, norm_final)
    y_prompt = y[:dm.tc].reshape(dm.bc, dm.lc, d)
    y_sample = y[dm.tc:].reshape(dm.bs, dm.ls, d)
    return (y_prompt, y_sample, jnp.stack(rets, axis=1), jnp.stack(s5r, axis=1), jnp.stack(s5i, axis=1),
            jnp.stack(ckvs, axis=1), jnp.stack(krs, axis=1))
```

```python
import functools
import math

import jax
import jax.numpy as jnp
import numpy as np
from jax import lax
from jax.experimental import pallas as pl
from jax.experimental.pallas import tpu as pltpu

F32 = jnp.float32
BF16 = jnp.bfloat16
U32 = jnp.uint32
EPS = 1e-6

RET_HEADS = 4
RET_CHUNK = 128
S5_GROUP = 16
S5_Q = 16
S5_LANES = 128
MLA_HEADS = 16
QK_NOPE = 128
QK_ROPE = 64
V_HEAD = 128
QK_PAD = 256
V_PAD = 256
GRID_W = 64
ROPE_BASE = 10000.0
N_EXPERT_GROUPS = 4
N_MOD = 6

VMEM_LIMIT = 56 * 1024 * 1024
TM_MAX = 1024
TQ_MAX = 512
KB_MAX = 768
TM_MOE = 256
DMA_UNROLL = 8
HI_MASK = np.uint32(0xFFFF0000)


def _cp(sem, vmem=VMEM_LIMIT):
    return pltpu.CompilerParams(dimension_semantics=sem, vmem_limit_bytes=vmem)


def _sigmoid(x):
    return 1.0 / (1.0 + jnp.exp(-x))


def _silu(x):
    return x * _sigmoid(x)


def _gelu_tanh(x):
    return 0.5 * x * (1.0 + jnp.tanh(math.sqrt(2.0 / math.pi) * (x + 0.044715 * (x * x * x))))


def _dot(a, b):
    return jnp.dot(a, b, preferred_element_type=F32)


def _dot_nt(a, b):
    return lax.dot_general(a, b, (((1,), (1,)), ((), ())), preferred_element_type=F32)


def _dot_tn(a, b):
    return lax.dot_general(a, b, (((0,), (0,)), ((), ())), preferred_element_type=F32)


def _pack_halves(x):
    half = x.shape[1] // 2
    xb = x.astype(BF16).astype(F32)
    lo = lax.bitcast_convert_type(xb[:, :half], U32) >> 16
    hi = lax.bitcast_convert_type(xb[:, half:], U32) & HI_MASK
    return hi | lo


def _unpack_halves(w):
    return (lax.bitcast_convert_type(w << 16, F32), lax.bitcast_convert_type(w & HI_MASK, F32))


class _Dims:
    def __init__(self, x_prompt, x_sample):
        self.bc, self.lc, self.d = x_prompt.shape
        self.bs, self.ls, _ = x_sample.shape
        self.tc = self.bc * self.lc
        self.ts = self.bs * self.ls
        self.t = self.tc + self.ts
        self.rp = -(-(1 + self.bs) // 8) * 8
        self.tm = min(TM_MAX, self.ls)
        assert self.tc % self.tm == 0 and self.ls % self.tm == 0

    def mod_row(self, i, tm):
        nct = self.tc // tm
        return jnp.where(i < nct, 0, 1 + (i - nct) // (self.ls // tm))


def _mod_idx(dm, layer, k, tm):
    def idx(i, *_):
        return ((layer * dm.rp + dm.mod_row(i, tm)) * N_MOD + k, 0, 0)
    return idx


def _adaln_kernel(c_ref, w_ref, b_ref, o_ref):
    cs = _silu(c_ref[...]).astype(BF16)
    o_ref[0] = _dot(cs, w_ref[0].astype(BF16)) + b_ref[0]


def _adaln(cond, ada_w, ada_b, tn):
    depth, d, n = ada_w.shape
    rp = cond.shape[0]
    return pl.pallas_call(
        _adaln_kernel,
        grid=(depth, n // tn),
        in_specs=[pl.BlockSpec((rp, d), lambda l, j: (0, 0)),
                  pl.BlockSpec((1, d, tn), lambda l, j: (l, 0, j)),
                  pl.BlockSpec((1, 1, tn), lambda l, j: (l, 0, j))],
        out_specs=pl.BlockSpec((1, rp, tn), lambda l, j: (l, 0, j)),
        out_shape=jax.ShapeDtypeStruct((depth, rp, n), F32),
        compiler_params=_cp(("parallel", "parallel")),
        name="adaln",
    )(cond, ada_w, ada_b.reshape(depth, 1, n))


def _norm_mod(x, g, sc, sh):
    y = x * lax.rsqrt(jnp.mean(x * x, axis=-1, keepdims=True) + EPS) * g
    return y * (1.0 + sc) + sh


def _nmm_kernel(x_ref, g_ref, sh_ref, sc_ref, w_ref, o_ref, hn_ref):
    @pl.when(pl.program_id(1) == 0)
    def _():
        hn_ref[...] = _norm_mod(x_ref[...], g_ref[0], sc_ref[0], sh_ref[0]).astype(BF16)

    o_ref[...] = _dot(hn_ref[...], w_ref[...]).astype(o_ref.dtype)


def _norm_mod_matmul(dm, x, gain, mods, layer, k_shift, w, out_dtype, tn, name):
    t, d = x.shape
    n = w.shape[1]
    tm = dm.tm
    return pl.pallas_call(
        _nmm_kernel,
        grid=(t // tm, n // tn),
        in_specs=[pl.BlockSpec((tm, d), lambda i, j: (i, 0)),
                  pl.BlockSpec((1, 1, d), lambda i, j: (layer, 0, 0)),
                  pl.BlockSpec((1, 1, d), _mod_idx(dm, layer, k_shift, tm)),
                  pl.BlockSpec((1, 1, d), _mod_idx(dm, layer, k_shift + 1, tm)),
                  pl.BlockSpec((d, tn), lambda i, j: (0, j))],
        out_specs=pl.BlockSpec((tm, tn), lambda i, j: (i, j)),
        out_shape=jax.ShapeDtypeStruct((t, n), out_dtype),
        scratch_shapes=[pltpu.VMEM((tm, d), BF16)],
        compiler_params=_cp(("parallel", "arbitrary")),
        name=name,
    )(x, gain, mods, mods, w)


def _mmres_kernel(n_in, x_ref, gate_ref, *refs):
    a_refs, w_refs, o_ref = refs[:n_in], refs[n_in:2 * n_in], refs[2 * n_in]
    acc = _dot(a_refs[0][...], w_refs[0][...])
    for a_ref, w_ref in zip(a_refs[1:], w_refs[1:]):
        acc += _dot(a_ref[...], w_ref[...])
    o_ref[...] = x_ref[...] + gate_ref[0] * acc


def _matmul_residual(dm, x, mods, layer, k_gate, acts, ws, tn, name):
    t, d = x.shape
    n_in = len(acts)
    tm = dm.tm

    def gate_idx(i, j):
        return ((layer * dm.rp + dm.mod_row(i, tm)) * N_MOD + k_gate, 0, j)

    in_specs = [pl.BlockSpec((tm, tn), lambda i, j: (i, j)),
                pl.BlockSpec((1, 1, tn), gate_idx)]
    in_specs += [pl.BlockSpec((tm, a.shape[1]), lambda i, j: (i, 0)) for a in acts]
    in_specs += [pl.BlockSpec((w.shape[0], tn), lambda i, j: (0, j)) for w in ws]
    return pl.pallas_call(
        functools.partial(_mmres_kernel, n_in),
        grid=(t // tm, d // tn),
        in_specs=in_specs,
        out_specs=pl.BlockSpec((tm, tn), lambda i, j: (i, j)),
        out_shape=jax.ShapeDtypeStruct((t, d), F32),
        compiler_params=_cp(("parallel", "parallel")),
        name=name,
    )(x, mods, *acts, *ws)


def _ret_kernel(has_s0, nch, scale, lg_ref, q_ref, k_ref, v_ref, g_ref, *refs):
    if has_s0:
        s0_ref, o_ref, sfin_ref, oacc_ref, st_ref = refs
    else:
        o_ref, sfin_ref, oacc_ref, st_ref = refs
    c = RET_CHUNK
    h = pl.program_id(1)
    lgf = lg_ref[0, h]
    lgb = lg_ref[1, h]
    dv = v_ref.shape[1]

    ii = lax.broadcasted_iota(jnp.int32, (c, c), 0).astype(F32)
    jj = lax.broadcasted_iota(jnp.int32, (c, c), 1).astype(F32)
    diff = ii - jj
    dtot = (jnp.where(diff >= 0, jnp.exp(lgf * jnp.maximum(diff, 0.0)), 0.0)
            + jnp.where(diff <= 0, jnp.exp(lgb * jnp.maximum(-diff, 0.0)), 0.0)) * scale
    pos = lax.broadcasted_iota(jnp.int32, (c, 1), 0).astype(F32)
    qdec_f = jnp.exp(lgf * (pos + 1.0)) * scale
    kdec_f = jnp.exp(lgf * (c - 1.0 - pos))
    qdec_b = jnp.exp(lgb * (c - pos)) * scale
    kdec_b = jnp.exp(lgb * pos)
    cdec_f = jnp.exp(jnp.full((1, dv), lgf * c, F32))
    cdec_b = jnp.exp(jnp.full((1, dv), lgb * c, F32))

    if has_s0:
        st_ref[...] = s0_ref[0, 0, 0]
    else:
        st_ref[...] = jnp.zeros_like(st_ref)

    def fwd(n, carry):
        r = pl.multiple_of(n * c, c)
        qn = q_ref[pl.ds(r, c), :]
        kn = k_ref[pl.ds(r, c), :]
        vn = v_ref[pl.ds(r, c), :]
        p = (_dot_nt(qn, kn) * dtot).astype(BF16)
        o = _dot(p, vn)
        o += _dot((qn.astype(F32) * qdec_f).astype(BF16), st_ref[...].astype(BF16))
        oacc_ref[pl.ds(r, c), :] = o
        kd = (kn.astype(F32) * kdec_f).astype(BF16)
        st_ref[...] = st_ref[...] * cdec_f + _dot_tn(kd, vn)
        return carry

    lax.fori_loop(0, nch, fwd, 0)
    sfin_ref[0, 0, 0] = st_ref[...]

    if has_s0:
        st_ref[...] = s0_ref[0, 1, 0]
    else:
        st_ref[...] = jnp.zeros_like(st_ref)

    def bwd(m, carry):
        r = pl.multiple_of((nch - 1 - m) * c, c)
        qn = q_ref[pl.ds(r, c), :]
        kn = k_ref[pl.ds(r, c), :]
        vn = v_ref[pl.ds(r, c), :]
        oacc_ref[pl.ds(r, c), :] += _dot((qn.astype(F32) * qdec_b).astype(BF16), st_ref[...].astype(BF16))
        kd = (kn.astype(F32) * kdec_b).astype(BF16)
        st_ref[...] = st_ref[...] * cdec_b + _dot_tn(kd, vn)
        return carry

    lax.fori_loop(0, nch, bwd, 0)
    sfin_ref[0, 1, 0] = st_ref[...]

    o = oacc_ref[...]
    o = o * lax.rsqrt(jnp.mean(o * o, axis=-1, keepdims=True) + EPS)
    o_ref[...] = (_silu(g_ref[...].astype(F32)) * o).astype(o_ref.dtype)


def _retention(z, dk, log_gamma, s0, nb, seq, row_blk0, name):
    hh = RET_HEADS
    dv = dk
    nch = seq // RET_CHUNK
    has_s0 = s0 is not None
    scale = float(dk) ** -0.5

    def col(off):
        return lambda b, h: (row_blk0 + b, off + h)

    in_specs = [pl.BlockSpec(memory_space=pltpu.SMEM),
                pl.BlockSpec((seq, dk), col(0)),
                pl.BlockSpec((seq, dk), col(hh)),
                pl.BlockSpec((seq, dv), col(2 * hh)),
                pl.BlockSpec((seq, dv), col(3 * hh))]
    args = [log_gamma, z, z, z, z]
    if has_s0:
        in_specs.append(pl.BlockSpec((1, 2, 1, dk, dv), lambda b, h: (b, 0, h, 0, 0)))
        args.append(s0)
    return pl.pallas_call(
        functools.partial(_ret_kernel, has_s0, nch, scale),
        grid=(nb, hh),
        in_specs=in_specs,
        out_specs=[pl.BlockSpec((seq, dv), lambda b, h: (b, h)),
                   pl.BlockSpec((1, 2, 1, dk, dv), lambda b, h: (b, 0, h, 0, 0))],
        out_shape=[jax.ShapeDtypeStruct((nb * seq, hh * dv), BF16),
                   jax.ShapeDtypeStruct((nb, 2, hh, dk, dv), F32)],
        scratch_shapes=[pltpu.VMEM((seq, dv), F32), pltpu.VMEM((dk, dv), F32)],
        compiler_params=_cp(("parallel", "parallel")),
        name=name,
    )(*args)


def _s5_kernel(rc, ncc, bc, ncs, bs, u_ref, w1_ref, v_ref, lam_ref, d_ref, x0_ref,
               y_ref, fin_ref, sre, sim, are, aim, bre, bim):
    u = u_ref[0]
    z = _dot(u, w1_ref[0])
    nq = u.shape[1]
    y0 = z[:, :nq] + d_ref[0] * u.astype(F32)
    sre[...] = z[:, nq:nq + S5_LANES]
    sim[...] = z[:, nq + S5_LANES:nq + 2 * S5_LANES]
    lr = lam_ref[0, 0:1, :]
    li = lam_ref[0, 1:2, :]

    def scan(row0, nc, b, xr0, xi0):
        is_f = lax.broadcasted_iota(jnp.int32, (b, S5_LANES), 1) < S5_LANES // 2

        def body(s, carry):
            xr, xi = carry
            rf = pl.multiple_of(row0 + s * b, b)
            rb = pl.multiple_of(row0 + (nc - 1 - s) * b, b)
            are[pl.ds(rf, b), :] = xr
            aim[pl.ds(rf, b), :] = xi
            bre[pl.ds(rb, b), :] = xr
            bim[pl.ds(rb, b), :] = xi
            sr = jnp.where(is_f, sre[pl.ds(rf, b), :], sre[pl.ds(rb, b), :])
            si = jnp.where(is_f, sim[pl.ds(rf, b), :], sim[pl.ds(rb, b), :])
            return xr * lr - xi * li + sr, xi * lr + xr * li + si

        return lax.fori_loop(0, nc, body, (xr0, xi0))

    zero = jnp.zeros((bc, S5_LANES), F32)
    fr, fi = scan(0, ncc, bc, zero, zero)
    fin_ref[0, 0] = fr
    fin_ref[0, 1] = fi
    scan(rc, ncs, bs, x0_ref[0, 0], x0_ref[0, 1])
    y_ref[0] = (y0 + _dot(are[...].astype(BF16), v_ref[0, 0]) + _dot(aim[...].astype(BF16), v_ref[0, 1])
                + _dot(bre[...].astype(BF16), v_ref[0, 2]) + _dot(bim[...].astype(BF16), v_ref[0, 3])
                ).astype(y_ref.dtype)


def _cmul(ar, ai, br, bi):
    return ar * br - ai * bi, ar * bi + ai * br


def _cexp(re, im):
    e = jnp.exp(re)
    return e * jnp.cos(im), e * jnp.sin(im)


def _s5_operators(a_re, a_im, log_dt, b_re, b_im, c_re, c_im):
    q = S5_Q
    hp = lax.Precision.HIGHEST
    a_re, a_im = a_re.astype(F32), a_im.astype(F32)
    dt = jnp.exp(log_dt.astype(F32))[..., None]
    ldr, ldi = a_re * dt, a_im * dt
    lbr, lbi = _cexp(ldr, ldi)
    den = a_re * a_re + a_im * a_im
    fr = ((lbr - 1.0) * a_re + lbi * a_im) / den
    fi = (lbi * a_re - (lbr - 1.0) * a_im) / den
    bbr, bbi = _cmul(fr[..., None], fi[..., None], b_re.astype(F32), b_im.astype(F32))
    c_re, c_im = c_re.astype(F32), c_im.astype(F32)
    g, p = a_re.shape[1], a_re.shape[2]
    cg = bbr.shape[-1]
    steps = jnp.arange(q + 1, dtype=F32)
    pwr, pwi = _cexp(ldr[..., None] * steps, ldi[..., None] * steps)
    cpr, cpi = _cmul(c_re[..., None], c_im[..., None], pwr[:, :, None, :, :q], pwi[:, :, None, :, :q])
    kern = (jnp.einsum('dgcpt,dgpe->dgtce', cpr, bbr, precision=hp)
            - jnp.einsum('dgcpt,dgpe->dgtce', cpi, bbi, precision=hp))
    ti = jnp.arange(q)
    lag = ti[None, :] - ti[:, None]
    m_f = jnp.where((lag >= 0)[None, :, :, None, None], kern[0][:, jnp.clip(lag, 0, q - 1)], 0.0)
    m_b = jnp.where((lag <= 0)[None, :, :, None, None], kern[1][:, jnp.clip(-lag, 0, q - 1)], 0.0)
    m_tot = jnp.transpose(m_f + m_b, (0, 1, 4, 2, 3)).reshape(g, q * cg, q * cg)

    def state_in(d, t_idx):
        wr, wi = _cmul(pwr[d][..., t_idx][..., None], pwi[d][..., t_idx][..., None],
                       bbr[d][:, :, None, :], bbi[d][:, :, None, :])
        tr = lambda w: jnp.transpose(w, (0, 2, 3, 1)).reshape(g, q * cg, p)
        return tr(wr), tr(wi)

    wfr, wfi = state_in(0, q - 1 - ti)
    wbr, wbi = state_in(1, ti)
    w1 = jnp.concatenate([m_tot, wfr, wbr, wfi, wbi], axis=-1)

    def state_out(d, t_idx):
        vr, vi = _cmul(c_re[d][..., None], c_im[d][..., None],
                       pwr[d][:, None, :, :][..., t_idx], pwi[d][:, None, :, :][..., t_idx])
        tr = lambda w: jnp.transpose(w, (0, 2, 3, 1)).reshape(g, p, q * cg)
        return tr(vr), tr(vi)

    vfr, vfi = state_out(0, ti + 1)
    vbr, vbi = state_out(1, q - ti)
    zf = jnp.zeros_like(vfr)
    v = jnp.stack([jnp.concatenate([vfr, zf], axis=1), jnp.concatenate([-vfi, zf], axis=1),
                   jnp.concatenate([zf, vbr], axis=1), jnp.concatenate([zf, -vbi], axis=1)], axis=1)
    lam_pack = jnp.stack([jnp.concatenate([pwr[0][..., q], pwr[1][..., q]], axis=-1),
                          jnp.concatenate([pwi[0][..., q], pwi[1][..., q]], axis=-1)], axis=1)
    return w1.astype(BF16), v.astype(BF16), lam_pack


def _s5(dm, u_tok, ops, d_skip, x0_re, x0_im):
    w1, v, lam_pack = ops
    g = w1.shape[0]
    p = lam_pack.shape[2] // 2
    assert 2 * p == S5_LANES
    q, cg = S5_Q, S5_GROUP
    ncc, ncs = dm.lc // q, dm.ls // q
    rc, rs = ncc * dm.bc, ncs * dm.bs
    r = rc + rs

    def to_group_major(x, b, nc):
        return x.reshape(b, nc, q, g, cg).transpose(3, 1, 0, 2, 4).reshape(g, nc * b, q * cg)

    u_g = jnp.concatenate([to_group_major(u_tok[:dm.tc], dm.bc, ncc),
                           to_group_major(u_tok[dm.tc:], dm.bs, ncs)], axis=1)
    d_g = jnp.tile(d_skip.astype(F32).reshape(g, 1, cg), (1, q, 1)).reshape(g, 1, q * cg)
    x0 = jnp.stack([jnp.concatenate([x0_re[:, 0], x0_re[:, 1]], axis=-1),
                    jnp.concatenate([x0_im[:, 0], x0_im[:, 1]], axis=-1)], axis=0)
    x0 = x0.transpose(2, 0, 1, 3).astype(F32)
    nw = w1.shape[2]
    y_g, fin = pl.pallas_call(
        functools.partial(_s5_kernel, rc, ncc, dm.bc, ncs, dm.bs),
        grid=(g,),
        in_specs=[pl.BlockSpec((1, r, q * cg), lambda i: (i, 0, 0)),
                  pl.BlockSpec((1, q * cg, nw), lambda i: (i, 0, 0)),
                  pl.BlockSpec((1, 4, S5_LANES, q * cg), lambda i: (i, 0, 0, 0)),
                  pl.BlockSpec((1, 2, S5_LANES), lambda i: (i, 0, 0)),
                  pl.BlockSpec((1, 1, q * cg), lambda i: (i, 0, 0)),
                  pl.BlockSpec((1, 2, dm.bs, S5_LANES), lambda i: (i, 0, 0, 0))],
        out_specs=[pl.BlockSpec((1, r, q * cg), lambda i: (i, 0, 0)),
                   pl.BlockSpec((1, 2, dm.bc, S5_LANES), lambda i: (i, 0, 0, 0))],
        out_shape=[jax.ShapeDtypeStruct((g, r, q * cg), BF16),
                   jax.ShapeDtypeStruct((g, 2, dm.bc, S5_LANES), F32)],
        scratch_shapes=[pltpu.VMEM((r, S5_LANES), F32)] * 6,
        compiler_params=_cp(("parallel",)),
        name="s5",
    )(u_g, w1, v, lam_pack, d_g, x0)

    def to_token_major(x, b, nc):
        return x.reshape(g, nc, b, q, cg).transpose(2, 1, 3, 0, 4).reshape(b * nc * q, g * cg)

    y_tok = jnp.concatenate([to_token_major(y_g[:, :rc], dm.bc, ncc),
                             to_token_major(y_g[:, rc:], dm.bs, ncs)], axis=0)
    fin = fin.transpose(2, 1, 0, 3)
    fin_re = jnp.stack([fin[:, 0, :, :p], fin[:, 0, :, p:]], axis=1)
    fin_im = jnp.stack([fin[:, 1, :, :p], fin[:, 1, :, p:]], axis=1)
    return y_tok, fin_re, fin_im


def _glu_kernel(y_ref, w_ref, o_ref):
    z = _gelu_tanh(y_ref[...].astype(F32))
    o_ref[...] = (z * _sigmoid(_dot(z.astype(BF16), w_ref[...]))).astype(o_ref.dtype)


def _glu(dm, y, w):
    t, n = y.shape
    tm = dm.tm
    return pl.pallas_call(
        _glu_kernel,
        grid=(t // tm,),
        in_specs=[pl.BlockSpec((tm, n), lambda i: (i, 0)),
                  pl.BlockSpec((n, n), lambda i: (0, 0))],
        out_specs=pl.BlockSpec((tm, n), lambda i: (i, 0)),
        out_shape=jax.ShapeDtypeStruct((t, n), BF16),
        compiler_params=_cp(("parallel",)),
        name="s5_glu",
    )(y, w)


def _rot(x, ct, st):
    return x * ct + pltpu.roll(x, 64, 1) * st


def _mla_q_kernel(qscale, cq_ref, ckv_ref, kr_ref, qn_ref, kvn_ref, ct_ref, st_ref, w_ref,
                  q_ref, ckvn_ref, kro_ref):
    ct = ct_ref[...]
    st = st_ref[...]
    cq = cq_ref[...]
    cqn = (cq * lax.rsqrt(jnp.mean(cq * cq, axis=-1, keepdims=True) + EPS) * qn_ref[...]).astype(BF16)
    for h in range(MLA_HEADS):
        qh = _dot(cqn, w_ref[:, h * QK_PAD:(h + 1) * QK_PAD])
        q_ref[:, h * QK_PAD:h * QK_PAD + QK_NOPE] = (qh[:, :QK_NOPE] * qscale).astype(BF16)
        q_ref[:, h * QK_PAD + QK_NOPE:(h + 1) * QK_PAD] = (_rot(qh[:, QK_NOPE:], ct, st) * qscale).astype(BF16)
    ckv = ckv_ref[...]
    ckvn_ref[...] = ckv * lax.rsqrt(jnp.mean(ckv * ckv, axis=-1, keepdims=True) + EPS) * kvn_ref[...]
    kro_ref[...] = _rot(kr_ref[...], ct, st)


def _mla_q(dm, a, q_norm, kv_norm, ct, st, w_uq_ext, q_lora, kv_lora):
    t = a.shape[0]
    nq = w_uq_ext.shape[1]
    kr_blk = (q_lora + kv_lora) // 128
    tm = min(dm.tm, 512)
    qscale = float(QK_NOPE + QK_ROPE) ** -0.5 * math.log2(math.e)
    return pl.pallas_call(
        functools.partial(_mla_q_kernel, qscale),
        grid=(t // tm,),
        in_specs=[pl.BlockSpec((tm, q_lora), lambda i: (i, 0)),
                  pl.BlockSpec((tm, kv_lora), lambda i: (i, q_lora // kv_lora)),
                  pl.BlockSpec((tm, 128), lambda i: (i, kr_blk)),
                  pl.BlockSpec((1, q_lora), lambda i: (0, 0)),
                  pl.BlockSpec((1, kv_lora), lambda i: (0, 0)),
                  pl.BlockSpec((tm, 128), lambda i: (i, 0)),
                  pl.BlockSpec((tm, 128), lambda i: (i, 0)),
                  pl.BlockSpec((q_lora, nq), lambda i: (0, 0))],
        out_specs=[pl.BlockSpec((tm, nq), lambda i: (i, 0)),
                   pl.BlockSpec((tm, kv_lora), lambda i: (i, 0)),
                   pl.BlockSpec((tm, 128), lambda i: (i, 0))],
        out_shape=[jax.ShapeDtypeStruct((t, nq), BF16),
                   jax.ShapeDtypeStruct((t, kv_lora), F32),
                   jax.ShapeDtypeStruct((t, 128), F32)],
        compiler_params=_cp(("parallel",)),
        name="mla_q",
    )(a, a, a, q_norm, kv_norm, ct, st, w_uq_ext)


def _kv_expand_kernel(c_ref, kr_ref, wk_ref, wv_ref, k_ref, v_ref):
    c = c_ref[...].astype(BF16)
    kn = _dot(c, wk_ref[...]).astype(BF16)
    vv = _dot(c, wv_ref[...]).astype(BF16)
    kr = kr_ref[...].astype(BF16)
    ones = jnp.ones((c.shape[0], V_PAD - V_HEAD), BF16)
    for h in range(MLA_HEADS):
        k_ref[:, h * QK_PAD:h * QK_PAD + QK_NOPE] = kn[:, h * QK_NOPE:(h + 1) * QK_NOPE]
        k_ref[:, h * QK_PAD + QK_NOPE:(h + 1) * QK_PAD] = kr
        v_ref[:, h * V_PAD:h * V_PAD + V_HEAD] = vv[:, h * V_HEAD:(h + 1) * V_HEAD]
        v_ref[:, h * V_PAD + V_HEAD:(h + 1) * V_PAD] = ones


def _kv_expand(ckv_keys, kr_keys, w_uk, w_uv):
    nk, kv_lora = ckv_keys.shape
    tm = 512
    return pl.pallas_call(
        _kv_expand_kernel,
        grid=(nk // tm,),
        in_specs=[pl.BlockSpec((tm, kv_lora), lambda i: (i, 0)),
                  pl.BlockSpec((tm, 128), lambda i: (i, 0)),
                  pl.BlockSpec(w_uk.shape, lambda i: (0, 0)),
                  pl.BlockSpec(w_uv.shape, lambda i: (0, 0))],
        out_specs=[pl.BlockSpec((tm, MLA_HEADS * QK_PAD), lambda i: (i, 0)),
                   pl.BlockSpec((tm, MLA_HEADS * V_PAD), lambda i: (i, 0))],
        out_shape=[jax.ShapeDtypeStruct((nk, MLA_HEADS * QK_PAD), BF16),
                   jax.ShapeDtypeStruct((nk, MLA_HEADS * V_PAD), BF16)],
        compiler_params=_cp(("parallel",)),
        name="mla_kv_expand",
    )(ckv_keys, kr_keys, w_uk, w_uv)


def _attn_kernel(nh, nkb, q_ref, k_ref, v_ref, o_ref):
    kb = k_ref.shape[0] // nkb
    for h in range(nh):
        q = q_ref[:, h * QK_PAD:(h + 1) * QK_PAD]
        m, acc = None, None
        for j in range(nkb):
            s = _dot_nt(q, k_ref[j * kb:(j + 1) * kb, h * QK_PAD:(h + 1) * QK_PAD])
            mj = jnp.max(s, axis=-1, keepdims=True)
            m_new = mj if j == 0 else jnp.maximum(m, mj)
            pv = _dot(jnp.exp2(s - m_new).astype(BF16), v_ref[j * kb:(j + 1) * kb, h * V_PAD:(h + 1) * V_PAD])
            acc = pv if j == 0 else acc * jnp.exp2(m - m_new) + pv
            m = m_new
        o_ref[:, h * V_HEAD:(h + 1) * V_HEAD] = (acc[:, :V_HEAD] / acc[:, V_HEAD:V_HEAD + 1]).astype(o_ref.dtype)


def _attention(q, k, v, nb, lq, lk, tq, nh, q_row0, k_row0, name):
    nqt = lq // tq
    qb0, kb0 = q_row0 // tq, k_row0 // lk
    nkb = -(-lk // KB_MAX)
    assert lk % nkb == 0
    return pl.pallas_call(
        functools.partial(_attn_kernel, nh, nkb),
        grid=(nb, MLA_HEADS // nh, nqt),
        in_specs=[pl.BlockSpec((tq, nh * QK_PAD), lambda b, h, i: (qb0 + b * nqt + i, h)),
                  pl.BlockSpec((lk, nh * QK_PAD), lambda b, h, i: (kb0 + b, h)),
                  pl.BlockSpec((lk, nh * V_PAD), lambda b, h, i: (kb0 + b, h))],
        out_specs=pl.BlockSpec((tq, nh * V_HEAD), lambda b, h, i: (b * nqt + i, h)),
        out_shape=jax.ShapeDtypeStruct((nb * lq, MLA_HEADS * V_HEAD), BF16),
        compiler_params=_cp(("parallel", "parallel", "arbitrary")),
        name=name,
    )(q, k, v)


def _route_kernel(n_exp, x_ref, g_ref, sh_ref, sc_ref, rw_ref, rb_ref, tri_ref,
                  hp_ref, idx_ref, rank_ref, wt_ref, cnt_ref, run_ref):
    @pl.when(pl.program_id(0) == 0)
    def _():
        run_ref[...] = jnp.zeros_like(run_ref)

    h = _norm_mod(x_ref[...], g_ref[0], sc_ref[0], sh_ref[0])
    hp_ref[...] = _pack_halves(h)
    scores = _sigmoid(_dot_nt(rw_ref[...], h.astype(BF16)))
    biased = scores + rb_ref[...]
    per = n_exp // N_EXPERT_GROUPS
    assert per == 4
    rows_b = [biased[e:e + 1, :] for e in range(n_exp)]
    rows_s = [scores[e:e + 1, :] for e in range(n_exp)]
    best_sum, best_g = None, None
    for gi in range(N_EXPERT_GROUPS):
        a, b, c, d = rows_b[per * gi:per * gi + per]
        hi1, lo1 = jnp.maximum(a, b), jnp.minimum(a, b)
        hi2, lo2 = jnp.maximum(c, d), jnp.minimum(c, d)
        top2 = jnp.maximum(hi1, hi2) + jnp.maximum(jnp.minimum(hi1, hi2), jnp.maximum(lo1, lo2))
        if gi == 0:
            best_sum, best_g = top2, jnp.zeros_like(top2, dtype=jnp.int32)
        else:
            upd = top2 > best_sum
            best_sum = jnp.where(upd, top2, best_sum)
            best_g = jnp.where(upd, gi, best_g)
    vb, vs = [], []
    for k in range(per):
        accb, accs = rows_b[k], rows_s[k]
        for gi in range(1, N_EXPERT_GROUPS):
            sel = best_g == gi
            accb = jnp.where(sel, rows_b[per * gi + k], accb)
            accs = jnp.where(sel, rows_s[per * gi + k], accs)
        vb.append(accb)
        vs.append(accs)

    def first_argmax(vals, exclude):
        bv, bi, bs = None, None, None
        for k in range(per):
            v = vals[k] if exclude is None else jnp.where(exclude == k, -jnp.inf, vals[k])
            if k == 0:
                bv, bi, bs = v, jnp.zeros_like(best_g), vs[0]
            else:
                upd = v > bv
                bv = jnp.where(upd, v, bv)
                bi = jnp.where(upd, k, bi)
                bs = jnp.where(upd, vs[k], bs)
        return bi, bs

    i1, s1 = first_argmax(vb, None)
    i2, s2 = first_argmax(vb, i1)
    tot = s1 + s2
    e1 = best_g * per + i1
    e2 = best_g * per + i2
    idx_ref[0:1, :] = e1
    idx_ref[1:2, :] = e2
    wt_ref[0:1, :] = s1 / tot
    wt_ref[1:2, :] = s2 / tot
    eids = lax.broadcasted_iota(jnp.int32, (n_exp, e1.shape[1]), 0)
    hit1 = eids == e1
    hit2 = eids == e2
    oh1 = jnp.where(hit1, 1.0, 0.0)
    oh2 = jnp.where(hit2, 1.0, 0.0)
    p1 = _dot(oh1.astype(BF16), tri_ref[...])
    p2 = _dot(oh2.astype(BF16), tri_ref[...])
    c1 = jnp.sum(oh1, axis=1, keepdims=True)
    c2 = jnp.sum(oh2, axis=1, keepdims=True)
    run = run_ref[...]
    rank_ref[0:1, :] = jnp.sum(jnp.where(hit1, run + p1, 0.0), axis=0, keepdims=True).astype(jnp.int32)
    rank_ref[1:2, :] = jnp.sum(jnp.where(hit2, run + c1 + p2, 0.0), axis=0, keepdims=True).astype(jnp.int32)
    run_ref[...] = run + c1 + c2
    cnt_ref[...] = run + c1 + c2


def _route(dm, x, gain, mods, layer, rw_t, rb):
    t, d = x.shape
    n_exp = rw_t.shape[0]
    tm = dm.tm
    tri = (jnp.arange(tm)[:, None] < jnp.arange(tm)[None, :]).astype(BF16)
    return pl.pallas_call(
        functools.partial(_route_kernel, n_exp),
        grid=(t // tm,),
        in_specs=[pl.BlockSpec((tm, d), lambda i: (i, 0)),
                  pl.BlockSpec((1, 1, d), lambda i: (layer, 0, 0)),
                  pl.BlockSpec((1, 1, d), _mod_idx(dm, layer, 3, tm)),
                  pl.BlockSpec((1, 1, d), _mod_idx(dm, layer, 4, tm)),
                  pl.BlockSpec((n_exp, d), lambda i: (0, 0)),
                  pl.BlockSpec((n_exp, 1), lambda i: (0, 0)),
                  pl.BlockSpec((tm, tm), lambda i: (0, 0))],
        out_specs=[pl.BlockSpec((tm, d // 2), lambda i: (i, 0)),
                   pl.BlockSpec((2, tm), lambda i: (0, i)),
                   pl.BlockSpec((2, tm), lambda i: (0, i)),
                   pl.BlockSpec((2, tm), lambda i: (0, i)),
                   pl.BlockSpec((n_exp, 1), lambda i: (0, 0))],
        out_shape=[jax.ShapeDtypeStruct((t, d // 2), U32),
                   jax.ShapeDtypeStruct((2, t), jnp.int32),
                   jax.ShapeDtypeStruct((2, t), jnp.int32),
                   jax.ShapeDtypeStruct((2, t), F32),
                   jax.ShapeDtypeStruct((n_exp, 1), F32)],
        scratch_shapes=[pltpu.VMEM((n_exp, 1), F32)],
        compiler_params=_cp(("arbitrary",)),
        name="moe_route",
    )(x, gain, mods, mods, rw_t, rb, tri)


def _slot_plan(idx, rank, counts, n_exp, n_slots):
    counts = counts.reshape(n_exp).astype(jnp.int32)
    padded = ((counts + TM_MOE - 1) // TM_MOE) * TM_MOE
    ends = jnp.cumsum(padded)
    starts = ends - padded
    dest = (starts[idx] + rank).reshape(-1).astype(jnp.int32)
    tile_start = jnp.arange(n_slots // TM_MOE, dtype=jnp.int32) * TM_MOE
    tile_exp = jnp.minimum(jnp.sum((ends[None, :] <= tile_start[:, None]).astype(jnp.int32), axis=1), n_exp - 1)
    tile_ok = (tile_start < ends[-1]).astype(jnp.int32)
    pad_lo = jnp.concatenate([starts + counts, ends[-1:]]).astype(jnp.int32)
    pad_n = (padded - counts).astype(jnp.int32)
    return dest, tile_exp.astype(jnp.int32), tile_ok, pad_lo, pad_n


def _scatter_kernel(n_exp, t_total, rows, dest_ref, padlo_ref, padn_ref, hp_ref, xs_ref, zrow, sem):
    i = pl.program_id(0)
    base = i * rows

    @pl.when(i == 0)
    def _():
        zrow[...] = jnp.zeros_like(zrow)
        for e in range(n_exp):
            lo = padlo_ref[e]

            def pad_copy(r, lo=lo):
                return pltpu.make_async_copy(zrow.at[pl.ds(0, 1)], xs_ref.at[pl.ds(lo + r, 1)], sem)

            def start(r, c, pad_copy=pad_copy):
                pad_copy(r).start()
                return c

            def wait(r, c, pad_copy=pad_copy):
                pad_copy(r).wait()
                return c

            lax.fori_loop(0, padn_ref[e], start, 0)
            lax.fori_loop(0, padn_ref[e], wait, 0)

        tail0 = padlo_ref[n_exp]

        def tail_copy(k):
            row = pl.multiple_of(tail0 + k * TM_MOE, TM_MOE)
            return pltpu.make_async_copy(zrow, xs_ref.at[pl.ds(row, TM_MOE)], sem)

        def tail_start(k, c):
            tail_copy(k).start()
            return c

        def tail_wait(k, c):
            tail_copy(k).wait()
            return c

        n_tail = (xs_ref.shape[0] - tail0) // TM_MOE
        lax.fori_loop(0, n_tail, tail_start, 0)
        lax.fori_loop(0, n_tail, tail_wait, 0)

    def copies(r):
        src = hp_ref.at[pl.ds(r, 1)]
        return (pltpu.make_async_copy(src, xs_ref.at[pl.ds(dest_ref[r], 1)], sem),
                pltpu.make_async_copy(src, xs_ref.at[pl.ds(dest_ref[t_total + r], 1)], sem))

    def start(gi, c):
        for u in range(DMA_UNROLL):
            for cp in copies(base + gi * DMA_UNROLL + u):
                cp.start()
        return c

    def wait(gi, c):
        for u in range(DMA_UNROLL):
            for cp in copies(base + gi * DMA_UNROLL + u):
                cp.wait()
        return c

    lax.fori_loop(0, rows // DMA_UNROLL, start, 0)
    lax.fori_loop(0, rows // DMA_UNROLL, wait, 0)


def _scatter_rows(hp, dest, pad_lo, pad_n, n_slots, rows):
    t, hw = hp.shape
    n_exp = pad_n.shape[0]
    return pl.pallas_call(
        functools.partial(_scatter_kernel, n_exp, t, rows),
        grid_spec=pltpu.PrefetchScalarGridSpec(
            num_scalar_prefetch=3,
            grid=(t // rows,),
            in_specs=[pl.BlockSpec(memory_space=pl.ANY)],
            out_specs=pl.BlockSpec(memory_space=pl.ANY),
            scratch_shapes=[pltpu.VMEM((TM_MOE, hw), U32), pltpu.SemaphoreType.DMA(())]),
        out_shape=jax.ShapeDtypeStruct((n_slots, hw), U32),
        compiler_params=_cp(("arbitrary",)),
        name="moe_scatter",
    )(dest, pad_lo, pad_n, hp)


def _expert_kernel(te_ref, ok_ref, x_ref, wg_ref, wu_ref, wd_ref, o_ref, wg_b, wu_b, wd_b):
    i = pl.program_id(0)
    new_expert = jnp.logical_or(i == 0, te_ref[i] != te_ref[jnp.maximum(i - 1, 0)])

    @pl.when(new_expert)
    def _():
        wg_b[...] = wg_ref[0, 0].astype(BF16)
        wu_b[...] = wu_ref[0, 0].astype(BF16)
        wd_b[...] = wd_ref[0, 0].astype(BF16)

    @pl.when(ok_ref[i] == 1)
    def _():
        x_lo, x_hi = _unpack_halves(x_ref[...])
        x_lo, x_hi = x_lo.astype(BF16), x_hi.astype(BF16)
        half = x_lo.shape[1]
        h1 = _dot(x_lo, wg_b[:half, :]) + _dot(x_hi, wg_b[half:, :])
        h2 = _dot(x_lo, wu_b[:half, :]) + _dot(x_hi, wu_b[half:, :])
        act = (_silu(h1) * h2).astype(BF16)
        o_ref[...] = _pack_halves(_dot(act, wd_b[...]))

    @pl.when(ok_ref[i] == 0)
    def _():
        o_ref[...] = jnp.zeros_like(o_ref)


def _experts(x_sorted, tile_exp, tile_ok, w_gate, w_up, w_down, layer):
    n_slots, hw = x_sorted.shape
    d = 2 * hw
    f = w_gate.shape[-1]
    return pl.pallas_call(
        _expert_kernel,
        grid_spec=pltpu.PrefetchScalarGridSpec(
            num_scalar_prefetch=2,
            grid=(n_slots // TM_MOE,),
            in_specs=[pl.BlockSpec((TM_MOE, hw), lambda i, te, ok: (i, 0)),
                      pl.BlockSpec((1, 1, d, f), lambda i, te, ok: (layer, te[i], 0, 0)),
                      pl.BlockSpec((1, 1, d, f), lambda i, te, ok: (layer, te[i], 0, 0)),
                      pl.BlockSpec((1, 1, f, d), lambda i, te, ok: (layer, te[i], 0, 0))],
            out_specs=pl.BlockSpec((TM_MOE, hw), lambda i, te, ok: (i, 0)),
            scratch_shapes=[pltpu.VMEM((d, f), BF16), pltpu.VMEM((d, f), BF16), pltpu.VMEM((f, d), BF16)]),
        out_shape=jax.ShapeDtypeStruct((n_slots, hw), U32),
        compiler_params=_cp(("arbitrary",)),
        name="moe_experts",
    )(tile_exp, tile_ok, x_sorted, w_gate, w_up, w_down)


def _combine_kernel(t_total, rows, dest_ref, x_ref, gate_ref, wt_ref, y_ref, o_ref, y0, y1, sem):
    base = pl.program_id(0) * rows

    def copies(r):
        return (pltpu.make_async_copy(y_ref.at[pl.ds(dest_ref[base + r], 1)], y0.at[pl.ds(r, 1)], sem),
                pltpu.make_async_copy(y_ref.at[pl.ds(dest_ref[t_total + base + r], 1)], y1.at[pl.ds(r, 1)], sem))

    def start(gi, c):
        for u in range(DMA_UNROLL):
            for cp in copies(gi * DMA_UNROLL + u):
                cp.start()
        return c

    def wait(gi, c):
        for u in range(DMA_UNROLL):
            for cp in copies(gi * DMA_UNROLL + u):
                cp.wait()
        return c

    lax.fori_loop(0, rows // DMA_UNROLL, start, 0)
    lax.fori_loop(0, rows // DMA_UNROLL, wait, 0)
    w0 = wt_ref[:, 0:1]
    w1 = wt_ref[:, 1:2]
    a_lo, a_hi = _unpack_halves(y0[...])
    b_lo, b_hi = _unpack_halves(y1[...])
    half = a_lo.shape[1]
    gate = gate_ref[0]
    o_ref[:, :half] = x_ref[:, :half] + gate[:, :half] * (w0 * a_lo + w1 * b_lo)
    o_ref[:, half:] = x_ref[:, half:] + gate[:, half:] * (w0 * a_hi + w1 * b_hi)


def _combine(dm, x, mods, layer, y_sorted, dest, wts_t):
    t, d = x.shape
    rows = min(dm.tm, 512)
    return pl.pallas_call(
        functools.partial(_combine_kernel, t, rows),
        grid_spec=pltpu.PrefetchScalarGridSpec(
            num_scalar_prefetch=1,
            grid=(t // rows,),
            in_specs=[pl.BlockSpec((rows, d), lambda i, dst: (i, 0)),
                      pl.BlockSpec((1, 1, d), lambda i, dst: _mod_idx(dm, layer, 5, rows)(i)),
                      pl.BlockSpec((rows, 2), lambda i, dst: (i, 0)),
                      pl.BlockSpec(memory_space=pl.ANY)],
            out_specs=pl.BlockSpec((rows, d), lambda i, dst: (i, 0)),
            scratch_shapes=[pltpu.VMEM((rows, d // 2), U32), pltpu.VMEM((rows, d // 2), U32),
                            pltpu.SemaphoreType.DMA(())]),
        out_shape=jax.ShapeDtypeStruct((t, d), F32),
        compiler_params=_cp(("arbitrary",)),
        name="moe_combine",
    )(dest, x, mods, wts_t, y_sorted)


def _moe(dm, x, gain, mods, layer, rw_t, rb, w_gate, w_up, w_down):
    n_exp = rw_t.shape[0]
    hp, idx, rank, wts, counts = _route(dm, x, gain, mods, layer, rw_t, rb)
    n_slots = 2 * dm.t + n_exp * TM_MOE
    dest, tile_exp, tile_ok, pad_lo, pad_n = _slot_plan(idx, rank, counts, n_exp, n_slots)
    x_sorted = _scatter_rows(hp, dest, pad_lo, pad_n, n_slots, min(dm.tm, 512))
    y_sorted = _experts(x_sorted, tile_exp, tile_ok, w_gate, w_up, w_down, layer)
    return _combine(dm, x, mods, layer, y_sorted, dest, wts.T)


def _final_norm_kernel(x_ref, g_ref, o_ref):
    x = x_ref[...]
    o_ref[...] = x * lax.rsqrt(jnp.mean(x * x, axis=-1, keepdims=True) + EPS) * g_ref[...]


def _final_norm(dm, x, gain):
    t, d = x.shape
    tm = dm.tm
    return pl.pallas_call(
        _final_norm_kernel,
        grid=(t // tm,),
        in_specs=[pl.BlockSpec((tm, d), lambda i: (i, 0)),
                  pl.BlockSpec((1, d), lambda i: (0, 0))],
        out_specs=pl.BlockSpec((tm, d), lambda i: (i, 0)),
        out_shape=jax.ShapeDtypeStruct((t, d), F32),
        compiler_params=_cp(("parallel",)),
        name="final_norm",
    )(x, gain.reshape(1, d))


def _rope_tables(dm):
    rows = dm.ls // GRID_W
    row = jnp.repeat(jnp.arange(rows, dtype=F32), GRID_W)
    col = jnp.tile(jnp.arange(GRID_W, dtype=F32), rows)
    half = QK_ROPE // 2
    freqs = jnp.power(ROPE_BASE, -jnp.arange(0, half, 2, dtype=F32) / half)
    ar, ac = row[:, None] * freqs, col[:, None] * freqs
    zeros = jnp.zeros((dm.ls, 128 - QK_ROPE), F32)
    ct = jnp.concatenate([jnp.cos(ar), jnp.cos(ar), jnp.cos(ac), jnp.cos(ac), zeros], axis=-1)
    st = jnp.concatenate([-jnp.sin(ar), jnp.sin(ar), -jnp.sin(ac), jnp.sin(ac), zeros], axis=-1)
    ct_c = jnp.concatenate([jnp.ones((dm.tc, QK_ROPE), F32), jnp.zeros((dm.tc, 128 - QK_ROPE), F32)], axis=-1)
    ct = jnp.concatenate([ct_c, jnp.tile(ct, (dm.bs, 1))], axis=0)
    st = jnp.concatenate([jnp.zeros((dm.tc, 128), F32), jnp.tile(st, (dm.bs, 1))], axis=0)
    return ct, st


def _swap_halves_cols(w):
    qt = QK_ROPE // 4
    return jnp.concatenate([w[..., qt:2 * qt], w[..., :qt], w[..., 3 * qt:], w[..., 2 * qt:3 * qt]], axis=-1)


def kernel(x_prompt, x_sample, c, state_ret, state_s5_re, state_s5_im, cache_ckv, cache_krope, c_ctx, ada_w, ada_b, norm_mix, norm_ffn, norm_final, even_w_in, even_w_out, ret_decay, s5_a_re, s5_a_im, s5_log_dt, s5_b_re, s5_b_im, s5_c_re, s5_c_im, s5_d, s5_w_glu, mla_w_in, mla_q_norm, mla_w_uq, mla_kv_norm, mla_w_ukv, mla_w_out, router_w, router_bias, moe_w_gate, moe_w_up, moe_w_down):
    dm = _Dims(x_prompt, x_sample)
    d = dm.d
    depth = ada_w.shape[0]
    n_exp = router_w.shape[1]
    past = cache_ckv.shape[2]
    q_lora = mla_q_norm.shape[1]
    kv_lora = mla_kv_norm.shape[1]

    x = jnp.concatenate([x_prompt.reshape(dm.tc, d), x_sample.reshape(dm.ts, d)], axis=0)
    cond = jnp.zeros((dm.rp, d), F32).at[0].set(c_ctx).at[1:1 + dm.bs].set(c)
    mods = _adaln(cond, ada_w, ada_b, tn=d * N_MOD // 8).reshape(depth * dm.rp * N_MOD, 1, d)
    g_mix = norm_mix.reshape(depth, 1, d)
    g_ffn = norm_ffn.reshape(depth, 1, d)
    rw_t = router_w.T.astype(BF16)
    rb = router_bias.astype(F32).reshape(n_exp, 1)
    ct, st = _rope_tables(dm)

    rets, s5r, s5i, ckvs, krs = [], [], [], [], []
    for layer in range(depth):
        if layer % 2 == 0:
            i = layer // 2
            ret_w = even_w_out.shape[1] - s5_d.shape[1]
            z = _norm_mod_matmul(dm, x, g_mix, mods, layer, 0, even_w_in[i].astype(BF16), BF16,
                                 tn=even_w_in.shape[2] // 5, name="even_in_proj")
            log_gamma = -jnp.exp(ret_decay[i].astype(F32))
            ro_c, sfin = _retention(z, ret_w // RET_HEADS, log_gamma, None, dm.bc, dm.lc, 0, "retention_ctx")
            ro_s, _ = _retention(z, ret_w // RET_HEADS, log_gamma, state_ret[:, i].astype(F32), dm.bs, dm.ls,
                                 dm.tc // dm.ls, "retention_lat")
            ret_out = jnp.concatenate([ro_c, ro_s], axis=0)
            ops = _s5_operators(s5_a_re[i], s5_a_im[i], s5_log_dt[i], s5_b_re[i], s5_b_im[i],
                                s5_c_re[i], s5_c_im[i])
            y, f_re, f_im = _s5(dm, z[:, 4 * ret_w:], ops, s5_d[i], state_s5_re[:, i], state_s5_im[:, i])
            s5_out = _glu(dm, y, s5_w_glu[i].astype(BF16))
            w_out = even_w_out[i].astype(BF16)
            x = _matmul_residual(dm, x, mods, layer, 2, [ret_out, s5_out], [w_out[:ret_w], w_out[ret_w:]],
                                 tn=d // 2, name="even_out_proj")
            rets.append(sfin)
            s5r.append(f_re)
            s5i.append(f_im)
        else:
            j = layer // 2
            w_in = mla_w_in[j]
            w_in_ext = jnp.concatenate([w_in, _swap_halves_cols(w_in[:, q_lora + kv_lora:])], axis=1).astype(BF16)
            a = _norm_mod_matmul(dm, x, g_mix, mods, layer, 0, w_in_ext, F32, tn=w_in_ext.shape[1],
                                 name="mla_in_proj")
            w_uq = mla_w_uq[j].reshape(q_lora, MLA_HEADS, QK_NOPE + QK_ROPE)
            w_uq_ext = jnp.concatenate([w_uq, _swap_halves_cols(w_uq[..., QK_NOPE:])], axis=-1)
            w_uq_ext = w_uq_ext.reshape(q_lora, MLA_HEADS * QK_PAD).astype(BF16)
            q, ckv_n, kr = _mla_q(dm, a, mla_q_norm[j].reshape(1, q_lora), mla_kv_norm[j].reshape(1, kv_lora),
                                  ct, st, w_uq_ext, q_lora, kv_lora)
            ckv_lat = jnp.concatenate([cache_ckv[:, j].astype(F32), ckv_n[dm.tc:].reshape(dm.bs, dm.ls, kv_lora)], axis=1)
            kr_cache = jnp.concatenate([cache_krope[:, j].astype(F32),
                                        jnp.zeros((dm.bs, past, 128 - QK_ROPE), F32)], axis=-1)
            kr_lat = jnp.concatenate([kr_cache, kr[dm.tc:].reshape(dm.bs, dm.ls, 128)], axis=1)
            lk = past + dm.ls
            ckv_keys = jnp.concatenate([ckv_lat.reshape(dm.bs * lk, kv_lora), ckv_n[:dm.tc]], axis=0)
            kr_keys = jnp.concatenate([kr_lat.reshape(dm.bs * lk, 128), kr[:dm.tc]], axis=0)
            w_ukv = mla_w_ukv[j].reshape(kv_lora, MLA_HEADS, QK_NOPE + V_HEAD)
            w_uk = w_ukv[..., :QK_NOPE].reshape(kv_lora, MLA_HEADS * QK_NOPE).astype(BF16)
            w_uv = w_ukv[..., QK_NOPE:].reshape(kv_lora, MLA_HEADS * V_HEAD).astype(BF16)
            k_all, v_all = _kv_expand(ckv_keys, kr_keys, w_uk, w_uv)
            o_c = _attention(q, k_all, v_all, dm.bc, dm.lc, dm.lc, dm.lc, MLA_HEADS, 0, dm.bs * lk, "attn_ctx")
            o_s = _attention(q, k_all, v_all, dm.bs, dm.ls, lk, min(TQ_MAX, dm.ls), 1, dm.tc, 0, "attn_lat")
            o = jnp.concatenate([o_c, o_s], axis=0)
            x = _matmul_residual(dm, x, mods, layer, 2, [o], [mla_w_out[j].astype(BF16)], tn=d // 2,
                                 name="mla_out_proj")
            ckvs.append(ckv_n[:dm.tc].reshape(dm.bc, dm.lc, kv_lora))
            krs.append(kr[:dm.tc, :QK_ROPE].reshape(dm.bc, dm.lc, QK_ROPE))
        x = _moe(dm, x, g_ffn, mods, layer, rw_t, rb, moe_w_gate, moe_w_up, moe_w_down)

    y = _final_norm(dm, x, norm_final)
    y_prompt = y[:dm.tc].reshape(dm.bc, dm.lc, d)
    y_sample = y[dm.tc:].reshape(dm.bs, dm.ls, d)
    return (y_prompt, y_sample, jnp.stack(rets, axis=1), jnp.stack(s5r, axis=1), jnp.stack(s5i, axis=1),
            jnp.stack(ckvs, axis=1), jnp.stack(krs, axis=1))
```

```python
import functools
import math

import jax
import jax.numpy as jnp
import numpy as np
from jax import lax
from jax.experimental import pallas as pl
from jax.experimental.pallas import tpu as pltpu

F32 = jnp.float32
BF16 = jnp.bfloat16
U32 = jnp.uint32
EPS = 1e-6

RET_HEADS = 4
RET_CHUNK = 128
S5_GROUP = 16
S5_Q = 16
S5_LANES = 128
LANES = 128
MLA_HEADS = 16
QK_NOPE = 128
QK_ROPE = 64
V_HEAD = 128
QK_PAD = 256
V_PAD = 256
GRID_W = 64
ROPE_BASE = 10000.0
N_EXPERT_GROUPS = 4
N_MOD = 6

VMEM_LIMIT = 56 * 1024 * 1024
TM_MAX = 1024
TQ_MAX = 512
KB_MAX = 768
TM_MOE = 256
DMA_UNROLL = 8
HI_MASK = np.uint32(0xFFFF0000)


def _cp(sem, vmem=VMEM_LIMIT):
    return pltpu.CompilerParams(dimension_semantics=sem, vmem_limit_bytes=vmem)


def _sigmoid(x):
    return 1.0 / (1.0 + jnp.exp(-x))


def _silu(x):
    return x * _sigmoid(x)


def _gelu_tanh(x):
    return 0.5 * x * (1.0 + jnp.tanh(math.sqrt(2.0 / math.pi) * (x + 0.044715 * (x * x * x))))


def _dot(a, b):
    return jnp.dot(a, b, preferred_element_type=F32)


def _dot_nt(a, b):
    return lax.dot_general(a, b, (((1,), (1,)), ((), ())), preferred_element_type=F32)


def _dot_tn(a, b):
    return lax.dot_general(a, b, (((0,), (0,)), ((), ())), preferred_element_type=F32)


def _pack_halves(x):
    half = x.shape[1] // 2
    xb = x.astype(BF16).astype(F32)
    lo = lax.bitcast_convert_type(xb[:, :half], U32) >> 16
    hi = lax.bitcast_convert_type(xb[:, half:], U32) & HI_MASK
    return hi | lo


def _unpack_halves(w):
    return (lax.bitcast_convert_type(w << 16, F32), lax.bitcast_convert_type(w & HI_MASK, F32))


def _store_row_tiles(ref, w):
    r, rt = w.shape[0], w.shape[1] // LANES
    for s in range(rt):
        ref[pl.ds(s, r, stride=rt), :] = w[:, s * LANES:(s + 1) * LANES]


def _load_row_tiles(ref, rt):
    r = ref.shape[0] // rt
    return jnp.concatenate([ref[pl.ds(s, r, stride=rt), :] for s in range(rt)], axis=1)


class _Dims:
    def __init__(self, x_prompt, x_sample):
        self.bc, self.lc, self.d = x_prompt.shape
        self.bs, self.ls, _ = x_sample.shape
        self.tc = self.bc * self.lc
        self.ts = self.bs * self.ls
        self.t = self.tc + self.ts
        self.rp = -(-(1 + self.bs) // 8) * 8
        self.tm = min(TM_MAX, self.ls)
        assert self.tc % self.tm == 0 and self.ls % self.tm == 0

    def mod_row(self, i, tm):
        nct = self.tc // tm
        return jnp.where(i < nct, 0, 1 + (i - nct) // (self.ls // tm))


def _mod_idx(dm, layer, k, tm):
    def idx(i, *_):
        return ((layer * dm.rp + dm.mod_row(i, tm)) * N_MOD + k, 0, 0)
    return idx


def _adaln_kernel(c_ref, w_ref, b_ref, o_ref):
    cs = _silu(c_ref[...]).astype(BF16)
    o_ref[0] = _dot(cs, w_ref[0].astype(BF16)) + b_ref[0]


def _adaln(cond, ada_w, ada_b, tn):
    depth, d, n = ada_w.shape
    rp = cond.shape[0]
    return pl.pallas_call(
        _adaln_kernel,
        grid=(depth, n // tn),
        in_specs=[pl.BlockSpec((rp, d), lambda l, j: (0, 0)),
                  pl.BlockSpec((1, d, tn), lambda l, j: (l, 0, j)),
                  pl.BlockSpec((1, 1, tn), lambda l, j: (l, 0, j))],
        out_specs=pl.BlockSpec((1, rp, tn), lambda l, j: (l, 0, j)),
        out_shape=jax.ShapeDtypeStruct((depth, rp, n), F32),
        compiler_params=_cp(("parallel", "parallel")),
        name="adaln",
    )(cond, ada_w, ada_b.reshape(depth, 1, n))


def _norm_mod(x, g, sc, sh):
    y = x * lax.rsqrt(jnp.mean(x * x, axis=-1, keepdims=True) + EPS) * g
    return y * (1.0 + sc) + sh


def _nmm_kernel(x_ref, g_ref, sh_ref, sc_ref, w_ref, o_ref, hn_ref):
    @pl.when(pl.program_id(1) == 0)
    def _():
        hn_ref[...] = _norm_mod(x_ref[...], g_ref[0], sc_ref[0], sh_ref[0]).astype(BF16)

    o_ref[...] = _dot(hn_ref[...], w_ref[...]).astype(o_ref.dtype)


def _norm_mod_matmul(dm, x, gain, mods, layer, k_shift, w, out_dtype, tn, name):
    t, d = x.shape
    n = w.shape[1]
    tm = dm.tm
    return pl.pallas_call(
        _nmm_kernel,
        grid=(t // tm, n // tn),
        in_specs=[pl.BlockSpec((tm, d), lambda i, j: (i, 0)),
                  pl.BlockSpec((1, 1, d), lambda i, j: (layer, 0, 0)),
                  pl.BlockSpec((1, 1, d), _mod_idx(dm, layer, k_shift, tm)),
                  pl.BlockSpec((1, 1, d), _mod_idx(dm, layer, k_shift + 1, tm)),
                  pl.BlockSpec((d, tn), lambda i, j: (0, j))],
        out_specs=pl.BlockSpec((tm, tn), lambda i, j: (i, j)),
        out_shape=jax.ShapeDtypeStruct((t, n), out_dtype),
        scratch_shapes=[pltpu.VMEM((tm, d), BF16)],
        compiler_params=_cp(("parallel", "arbitrary")),
        name=name,
    )(x, gain, mods, mods, w)


def _mmres_kernel(n_in, x_ref, gate_ref, *refs):
    a_refs, w_refs, o_ref = refs[:n_in], refs[n_in:2 * n_in], refs[2 * n_in]
    acc = _dot(a_refs[0][...], w_refs[0][...])
    for a_ref, w_ref in zip(a_refs[1:], w_refs[1:]):
        acc += _dot(a_ref[...], w_ref[...])
    o_ref[...] = x_ref[...] + gate_ref[0] * acc


def _matmul_residual(dm, x, mods, layer, k_gate, acts, ws, tn, name):
    t, d = x.shape
    n_in = len(acts)
    tm = dm.tm

    def gate_idx(i, j):
        return ((layer * dm.rp + dm.mod_row(i, tm)) * N_MOD + k_gate, 0, j)

    in_specs = [pl.BlockSpec((tm, tn), lambda i, j: (i, j)),
                pl.BlockSpec((1, 1, tn), gate_idx)]
    in_specs += [pl.BlockSpec((tm, a.shape[1]), lambda i, j: (i, 0)) for a in acts]
    in_specs += [pl.BlockSpec((w.shape[0], tn), lambda i, j: (0, j)) for w in ws]
    return pl.pallas_call(
        functools.partial(_mmres_kernel, n_in),
        grid=(t // tm, d // tn),
        in_specs=in_specs,
        out_specs=pl.BlockSpec((tm, tn), lambda i, j: (i, j)),
        out_shape=jax.ShapeDtypeStruct((t, d), F32),
        compiler_params=_cp(("parallel", "parallel")),
        name=name,
    )(x, mods, *acts, *ws)


def _ret_kernel(has_s0, nch, scale, lg_ref, q_ref, k_ref, v_ref, g_ref, *refs):
    if has_s0:
        s0_ref, o_ref, sfin_ref, oacc_ref, st_ref = refs
    else:
        o_ref, sfin_ref, oacc_ref, st_ref = refs
    c = RET_CHUNK
    h = pl.program_id(1)
    lgf = lg_ref[0, h]
    lgb = lg_ref[1, h]
    dv = v_ref.shape[1]

    ii = lax.broadcasted_iota(jnp.int32, (c, c), 0).astype(F32)
    jj = lax.broadcasted_iota(jnp.int32, (c, c), 1).astype(F32)
    diff = ii - jj
    dtot = (jnp.where(diff >= 0, jnp.exp(lgf * jnp.maximum(diff, 0.0)), 0.0)
            + jnp.where(diff <= 0, jnp.exp(lgb * jnp.maximum(-diff, 0.0)), 0.0)) * scale
    pos = lax.broadcasted_iota(jnp.int32, (c, 1), 0).astype(F32)
    qdec_f = jnp.exp(lgf * (pos + 1.0)) * scale
    kdec_f = jnp.exp(lgf * (c - 1.0 - pos))
    qdec_b = jnp.exp(lgb * (c - pos)) * scale
    kdec_b = jnp.exp(lgb * pos)
    cdec_f = jnp.exp(jnp.full((1, dv), lgf * c, F32))
    cdec_b = jnp.exp(jnp.full((1, dv), lgb * c, F32))

    if has_s0:
        st_ref[...] = s0_ref[0, 0, 0]
    else:
        st_ref[...] = jnp.zeros_like(st_ref)

    def fwd(n, carry):
        r = pl.multiple_of(n * c, c)
        qn = q_ref[pl.ds(r, c), :]
        kn = k_ref[pl.ds(r, c), :]
        vn = v_ref[pl.ds(r, c), :]
        p = (_dot_nt(qn, kn) * dtot).astype(BF16)
        o = _dot(p, vn)
        o += _dot((qn.astype(F32) * qdec_f).astype(BF16), st_ref[...].astype(BF16))
        oacc_ref[pl.ds(r, c), :] = o
        kd = (kn.astype(F32) * kdec_f).astype(BF16)
        st_ref[...] = st_ref[...] * cdec_f + _dot_tn(kd, vn)
        return carry

    lax.fori_loop(0, nch, fwd, 0)
    sfin_ref[0, 0, 0] = st_ref[...]

    if has_s0:
        st_ref[...] = s0_ref[0, 1, 0]
    else:
        st_ref[...] = jnp.zeros_like(st_ref)

    def bwd(m, carry):
        r = pl.multiple_of((nch - 1 - m) * c, c)
        qn = q_ref[pl.ds(r, c), :]
        kn = k_ref[pl.ds(r, c), :]
        vn = v_ref[pl.ds(r, c), :]
        oacc_ref[pl.ds(r, c), :] += _dot((qn.astype(F32) * qdec_b).astype(BF16), st_ref[...].astype(BF16))
        kd = (kn.astype(F32) * kdec_b).astype(BF16)
        st_ref[...] = st_ref[...] * cdec_b + _dot_tn(kd, vn)
        return carry

    lax.fori_loop(0, nch, bwd, 0)
    sfin_ref[0, 1, 0] = st_ref[...]

    o = oacc_ref[...]
    o = o * lax.rsqrt(jnp.mean(o * o, axis=-1, keepdims=True) + EPS)
    o_ref[...] = (_silu(g_ref[...].astype(F32)) * o).astype(o_ref.dtype)


def _retention(z, dk, log_gamma, s0, nb, seq, row_blk0, name):
    hh = RET_HEADS
    dv = dk
    nch = seq // RET_CHUNK
    has_s0 = s0 is not None
    scale = float(dk) ** -0.5

    def col(off):
        return lambda b, h: (row_blk0 + b, off + h)

    in_specs = [pl.BlockSpec(memory_space=pltpu.SMEM),
                pl.BlockSpec((seq, dk), col(0)),
                pl.BlockSpec((seq, dk), col(hh)),
                pl.BlockSpec((seq, dv), col(2 * hh)),
                pl.BlockSpec((seq, dv), col(3 * hh))]
    args = [log_gamma, z, z, z, z]
    if has_s0:
        in_specs.append(pl.BlockSpec((1, 2, 1, dk, dv), lambda b, h: (b, 0, h, 0, 0)))
        args.append(s0)
    return pl.pallas_call(
        functools.partial(_ret_kernel, has_s0, nch, scale),
        grid=(nb, hh),
        in_specs=in_specs,
        out_specs=[pl.BlockSpec((seq, dv), lambda b, h: (b, h)),
                   pl.BlockSpec((1, 2, 1, dk, dv), lambda b, h: (b, 0, h, 0, 0))],
        out_shape=[jax.ShapeDtypeStruct((nb * seq, hh * dv), BF16),
                   jax.ShapeDtypeStruct((nb, 2, hh, dk, dv), F32)],
        scratch_shapes=[pltpu.VMEM((seq, dv), F32), pltpu.VMEM((dk, dv), F32)],
        compiler_params=_cp(("parallel", "parallel")),
        name=name,
    )(*args)


def _s5_kernel(rc, ncc, bc, ncs, bs, u_ref, w1_ref, v_ref, lam_ref, d_ref, x0_ref,
               y_ref, fin_ref, sre, sim, are, aim, bre, bim):
    u = u_ref[0]
    z = _dot(u, w1_ref[0])
    nq = u.shape[1]
    y0 = z[:, :nq] + d_ref[0] * u.astype(F32)
    sre[...] = z[:, nq:nq + S5_LANES]
    sim[...] = z[:, nq + S5_LANES:nq + 2 * S5_LANES]
    lr = lam_ref[0, 0:1, :]
    li = lam_ref[0, 1:2, :]

    def scan(row0, nc, b, xr0, xi0):
        is_f = lax.broadcasted_iota(jnp.int32, (b, S5_LANES), 1) < S5_LANES // 2

        def body(s, carry):
            xr, xi = carry
            rf = pl.multiple_of(row0 + s * b, b)
            rb = pl.multiple_of(row0 + (nc - 1 - s) * b, b)
            are[pl.ds(rf, b), :] = xr
            aim[pl.ds(rf, b), :] = xi
            bre[pl.ds(rb, b), :] = xr
            bim[pl.ds(rb, b), :] = xi
            sr = jnp.where(is_f, sre[pl.ds(rf, b), :], sre[pl.ds(rb, b), :])
            si = jnp.where(is_f, sim[pl.ds(rf, b), :], sim[pl.ds(rb, b), :])
            return xr * lr - xi * li + sr, xi * lr + xr * li + si

        return lax.fori_loop(0, nc, body, (xr0, xi0))

    zero = jnp.zeros((bc, S5_LANES), F32)
    fr, fi = scan(0, ncc, bc, zero, zero)
    fin_ref[0, 0] = fr
    fin_ref[0, 1] = fi
    scan(rc, ncs, bs, x0_ref[0, 0], x0_ref[0, 1])
    y_ref[0] = (y0 + _dot(are[...].astype(BF16), v_ref[0, 0]) + _dot(aim[...].astype(BF16), v_ref[0, 1])
                + _dot(bre[...].astype(BF16), v_ref[0, 2]) + _dot(bim[...].astype(BF16), v_ref[0, 3])
                ).astype(y_ref.dtype)


def _cmul(ar, ai, br, bi):
    return ar * br - ai * bi, ar * bi + ai * br


def _cexp(re, im):
    e = jnp.exp(re)
    return e * jnp.cos(im), e * jnp.sin(im)


def _s5_operators(a_re, a_im, log_dt, b_re, b_im, c_re, c_im):
    q = S5_Q
    hp = lax.Precision.HIGHEST
    a_re, a_im = a_re.astype(F32), a_im.astype(F32)
    dt = jnp.exp(log_dt.astype(F32))[..., None]
    ldr, ldi = a_re * dt, a_im * dt
    lbr, lbi = _cexp(ldr, ldi)
    den = a_re * a_re + a_im * a_im
    fr = ((lbr - 1.0) * a_re + lbi * a_im) / den
    fi = (lbi * a_re - (lbr - 1.0) * a_im) / den
    bbr, bbi = _cmul(fr[..., None], fi[..., None], b_re.astype(F32), b_im.astype(F32))
    c_re, c_im = c_re.astype(F32), c_im.astype(F32)
    g, p = a_re.shape[1], a_re.shape[2]
    cg = bbr.shape[-1]
    steps = jnp.arange(q + 1, dtype=F32)
    pwr, pwi = _cexp(ldr[..., None] * steps, ldi[..., None] * steps)
    cpr, cpi = _cmul(c_re[..., None], c_im[..., None], pwr[:, :, None, :, :q], pwi[:, :, None, :, :q])
    kern = (jnp.einsum('dgcpt,dgpe->dgtce', cpr, bbr, precision=hp)
            - jnp.einsum('dgcpt,dgpe->dgtce', cpi, bbi, precision=hp))
    ti = jnp.arange(q)
    lag = ti[None, :] - ti[:, None]
    m_f = jnp.where((lag >= 0)[None, :, :, None, None], kern[0][:, jnp.clip(lag, 0, q - 1)], 0.0)
    m_b = jnp.where((lag <= 0)[None, :, :, None, None], kern[1][:, jnp.clip(-lag, 0, q - 1)], 0.0)
    m_tot = jnp.transpose(m_f + m_b, (0, 1, 4, 2, 3)).reshape(g, q * cg, q * cg)

    def state_in(d, t_idx):
        wr, wi = _cmul(pwr[d][..., t_idx][..., None], pwi[d][..., t_idx][..., None],
                       bbr[d][:, :, None, :], bbi[d][:, :, None, :])
        tr = lambda w: jnp.transpose(w, (0, 2, 3, 1)).reshape(g, q * cg, p)
        return tr(wr), tr(wi)

    wfr, wfi = state_in(0, q - 1 - ti)
    wbr, wbi = state_in(1, ti)
    w1 = jnp.concatenate([m_tot, wfr, wbr, wfi, wbi], axis=-1)

    def state_out(d, t_idx):
        vr, vi = _cmul(c_re[d][..., None], c_im[d][..., None],
                       pwr[d][:, None, :, :][..., t_idx], pwi[d][:, None, :, :][..., t_idx])
        tr = lambda w: jnp.transpose(w, (0, 2, 3, 1)).reshape(g, p, q * cg)
        return tr(vr), tr(vi)

    vfr, vfi = state_out(0, ti + 1)
    vbr, vbi = state_out(1, q - ti)
    zf = jnp.zeros_like(vfr)
    v = jnp.stack([jnp.concatenate([vfr, zf], axis=1), jnp.concatenate([-vfi, zf], axis=1),
                   jnp.concatenate([zf, vbr], axis=1), jnp.concatenate([zf, -vbi], axis=1)], axis=1)
    lam_pack = jnp.stack([jnp.concatenate([pwr[0][..., q], pwr[1][..., q]], axis=-1),
                          jnp.concatenate([pwi[0][..., q], pwi[1][..., q]], axis=-1)], axis=1)
    return w1.astype(BF16), v.astype(BF16), lam_pack


def _s5(dm, u_tok, ops, d_skip, x0_re, x0_im):
    w1, v, lam_pack = ops
    g = w1.shape[0]
    p = lam_pack.shape[2] // 2
    assert 2 * p == S5_LANES
    q, cg = S5_Q, S5_GROUP
    ncc, ncs = dm.lc // q, dm.ls // q
    rc, rs = ncc * dm.bc, ncs * dm.bs
    r = rc + rs

    def to_group_major(x, b, nc):
        return x.reshape(b, nc, q, g, cg).transpose(3, 1, 0, 2, 4).reshape(g, nc * b, q * cg)

    u_g = jnp.concatenate([to_group_major(u_tok[:dm.tc], dm.bc, ncc),
                           to_group_major(u_tok[dm.tc:], dm.bs, ncs)], axis=1)
    d_g = jnp.tile(d_skip.astype(F32).reshape(g, 1, cg), (1, q, 1)).reshape(g, 1, q * cg)
    x0 = jnp.stack([jnp.concatenate([x0_re[:, 0], x0_re[:, 1]], axis=-1),
                    jnp.concatenate([x0_im[:, 0], x0_im[:, 1]], axis=-1)], axis=0)
    x0 = x0.transpose(2, 0, 1, 3).astype(F32)
    nw = w1.shape[2]
    y_g, fin = pl.pallas_call(
        functools.partial(_s5_kernel, rc, ncc, dm.bc, ncs, dm.bs),
        grid=(g,),
        in_specs=[pl.BlockSpec((1, r, q * cg), lambda i: (i, 0, 0)),
                  pl.BlockSpec((1, q * cg, nw), lambda i: (i, 0, 0)),
                  pl.BlockSpec((1, 4, S5_LANES, q * cg), lambda i: (i, 0, 0, 0)),
                  pl.BlockSpec((1, 2, S5_LANES), lambda i: (i, 0, 0)),
                  pl.BlockSpec((1, 1, q * cg), lambda i: (i, 0, 0)),
                  pl.BlockSpec((1, 2, dm.bs, S5_LANES), lambda i: (i, 0, 0, 0))],
        out_specs=[pl.BlockSpec((1, r, q * cg), lambda i: (i, 0, 0)),
                   pl.BlockSpec((1, 2, dm.bc, S5_LANES), lambda i: (i, 0, 0, 0))],
        out_shape=[jax.ShapeDtypeStruct((g, r, q * cg), BF16),
                   jax.ShapeDtypeStruct((g, 2, dm.bc, S5_LANES), F32)],
        scratch_shapes=[pltpu.VMEM((r, S5_LANES), F32)] * 6,
        compiler_params=_cp(("parallel",)),
        name="s5",
    )(u_g, w1, v, lam_pack, d_g, x0)

    def to_token_major(x, b, nc):
        return x.reshape(g, nc, b, q, cg).transpose(2, 1, 3, 0, 4).reshape(b * nc * q, g * cg)

    y_tok = jnp.concatenate([to_token_major(y_g[:, :rc], dm.bc, ncc),
                             to_token_major(y_g[:, rc:], dm.bs, ncs)], axis=0)
    fin = fin.transpose(2, 1, 0, 3)
    fin_re = jnp.stack([fin[:, 0, :, :p], fin[:, 0, :, p:]], axis=1)
    fin_im = jnp.stack([fin[:, 1, :, :p], fin[:, 1, :, p:]], axis=1)
    return y_tok, fin_re, fin_im


def _glu_kernel(y_ref, w_ref, o_ref):
    z = _gelu_tanh(y_ref[...].astype(F32))
    o_ref[...] = (z * _sigmoid(_dot(z.astype(BF16), w_ref[...]))).astype(o_ref.dtype)


def _glu(dm, y, w):
    t, n = y.shape
    tm = dm.tm
    return pl.pallas_call(
        _glu_kernel,
        grid=(t // tm,),
        in_specs=[pl.BlockSpec((tm, n), lambda i: (i, 0)),
                  pl.BlockSpec((n, n), lambda i: (0, 0))],
        out_specs=pl.BlockSpec((tm, n), lambda i: (i, 0)),
        out_shape=jax.ShapeDtypeStruct((t, n), BF16),
        compiler_params=_cp(("parallel",)),
        name="s5_glu",
    )(y, w)


def _rot(x, ct, st):
    return x * ct + pltpu.roll(x, 64, 1) * st


def _mla_q_kernel(qscale, cq_ref, ckv_ref, kr_ref, qn_ref, kvn_ref, ct_ref, st_ref, w_ref,
                  q_ref, ckvn_ref, kro_ref):
    ct = ct_ref[...]
    st = st_ref[...]
    cq = cq_ref[...]
    cqn = (cq * lax.rsqrt(jnp.mean(cq * cq, axis=-1, keepdims=True) + EPS) * qn_ref[...]).astype(BF16)
    for h in range(MLA_HEADS):
        qh = _dot(cqn, w_ref[:, h * QK_PAD:(h + 1) * QK_PAD])
        q_ref[:, h * QK_PAD:h * QK_PAD + QK_NOPE] = (qh[:, :QK_NOPE] * qscale).astype(BF16)
        q_ref[:, h * QK_PAD + QK_NOPE:(h + 1) * QK_PAD] = (_rot(qh[:, QK_NOPE:], ct, st) * qscale).astype(BF16)
    ckv = ckv_ref[...]
    ckvn_ref[...] = ckv * lax.rsqrt(jnp.mean(ckv * ckv, axis=-1, keepdims=True) + EPS) * kvn_ref[...]
    kro_ref[...] = _rot(kr_ref[...], ct, st)


def _mla_q(dm, a, q_norm, kv_norm, ct, st, w_uq_ext, q_lora, kv_lora):
    t = a.shape[0]
    nq = w_uq_ext.shape[1]
    kr_blk = (q_lora + kv_lora) // 128
    tm = min(dm.tm, 512)
    qscale = float(QK_NOPE + QK_ROPE) ** -0.5 * math.log2(math.e)
    return pl.pallas_call(
        functools.partial(_mla_q_kernel, qscale),
        grid=(t // tm,),
        in_specs=[pl.BlockSpec((tm, q_lora), lambda i: (i, 0)),
                  pl.BlockSpec((tm, kv_lora), lambda i: (i, q_lora // kv_lora)),
                  pl.BlockSpec((tm, 128), lambda i: (i, kr_blk)),
                  pl.BlockSpec((1, q_lora), lambda i: (0, 0)),
                  pl.BlockSpec((1, kv_lora), lambda i: (0, 0)),
                  pl.BlockSpec((tm, 128), lambda i: (i, 0)),
                  pl.BlockSpec((tm, 128), lambda i: (i, 0)),
                  pl.BlockSpec((q_lora, nq), lambda i: (0, 0))],
        out_specs=[pl.BlockSpec((tm, nq), lambda i: (i, 0)),
                   pl.BlockSpec((tm, kv_lora), lambda i: (i, 0)),
                   pl.BlockSpec((tm, 128), lambda i: (i, 0))],
        out_shape=[jax.ShapeDtypeStruct((t, nq), BF16),
                   jax.ShapeDtypeStruct((t, kv_lora), F32),
                   jax.ShapeDtypeStruct((t, 128), F32)],
        compiler_params=_cp(("parallel",)),
        name="mla_q",
    )(a, a, a, q_norm, kv_norm, ct, st, w_uq_ext)


def _kv_expand_kernel(c_ref, kr_ref, wk_ref, wv_ref, k_ref, v_ref):
    c = c_ref[...].astype(BF16)
    kn = _dot(c, wk_ref[...]).astype(BF16)
    vv = _dot(c, wv_ref[...]).astype(BF16)
    kr = kr_ref[...].astype(BF16)
    ones = jnp.ones((c.shape[0], V_PAD - V_HEAD), BF16)
    for h in range(MLA_HEADS):
        k_ref[:, h * QK_PAD:h * QK_PAD + QK_NOPE] = kn[:, h * QK_NOPE:(h + 1) * QK_NOPE]
        k_ref[:, h * QK_PAD + QK_NOPE:(h + 1) * QK_PAD] = kr
        v_ref[:, h * V_PAD:h * V_PAD + V_HEAD] = vv[:, h * V_HEAD:(h + 1) * V_HEAD]
        v_ref[:, h * V_PAD + V_HEAD:(h + 1) * V_PAD] = ones


def _kv_expand(ckv_keys, kr_keys, w_uk, w_uv):
    nk, kv_lora = ckv_keys.shape
    tm = 512
    return pl.pallas_call(
        _kv_expand_kernel,
        grid=(nk // tm,),
        in_specs=[pl.BlockSpec((tm, kv_lora), lambda i: (i, 0)),
                  pl.BlockSpec((tm, 128), lambda i: (i, 0)),
                  pl.BlockSpec(w_uk.shape, lambda i: (0, 0)),
                  pl.BlockSpec(w_uv.shape, lambda i: (0, 0))],
        out_specs=[pl.BlockSpec((tm, MLA_HEADS * QK_PAD), lambda i: (i, 0)),
                   pl.BlockSpec((tm, MLA_HEADS * V_PAD), lambda i: (i, 0))],
        out_shape=[jax.ShapeDtypeStruct((nk, MLA_HEADS * QK_PAD), BF16),
                   jax.ShapeDtypeStruct((nk, MLA_HEADS * V_PAD), BF16)],
        compiler_params=_cp(("parallel",)),
        name="mla_kv_expand",
    )(ckv_keys, kr_keys, w_uk, w_uv)


def _attn_kernel(nh, nkb, q_ref, k_ref, v_ref, o_ref):
    kb = k_ref.shape[0] // nkb
    for h in range(nh):
        q = q_ref[:, h * QK_PAD:(h + 1) * QK_PAD]
        m, acc = None, None
        for j in range(nkb):
            s = _dot_nt(q, k_ref[j * kb:(j + 1) * kb, h * QK_PAD:(h + 1) * QK_PAD])
            mj = jnp.max(s, axis=-1, keepdims=True)
            m_new = mj if j == 0 else jnp.maximum(m, mj)
            pv = _dot(jnp.exp2(s - m_new).astype(BF16), v_ref[j * kb:(j + 1) * kb, h * V_PAD:(h + 1) * V_PAD])
            acc = pv if j == 0 else acc * jnp.exp2(m - m_new) + pv
            m = m_new
        o_ref[:, h * V_HEAD:(h + 1) * V_HEAD] = (acc[:, :V_HEAD] / acc[:, V_HEAD:V_HEAD + 1]).astype(o_ref.dtype)


def _attention(q, k, v, nb, lq, lk, tq, nh, q_row0, k_row0, name):
    nqt = lq // tq
    qb0, kb0 = q_row0 // tq, k_row0 // lk
    nkb = -(-lk // KB_MAX)
    assert lk % nkb == 0
    return pl.pallas_call(
        functools.partial(_attn_kernel, nh, nkb),
        grid=(nb, MLA_HEADS // nh, nqt),
        in_specs=[pl.BlockSpec((tq, nh * QK_PAD), lambda b, h, i: (qb0 + b * nqt + i, h)),
                  pl.BlockSpec((lk, nh * QK_PAD), lambda b, h, i: (kb0 + b, h)),
                  pl.BlockSpec((lk, nh * V_PAD), lambda b, h, i: (kb0 + b, h))],
        out_specs=pl.BlockSpec((tq, nh * V_HEAD), lambda b, h, i: (b * nqt + i, h)),
        out_shape=jax.ShapeDtypeStruct((nb * lq, MLA_HEADS * V_HEAD), BF16),
        compiler_params=_cp(("parallel", "parallel", "arbitrary")),
        name=name,
    )(q, k, v)


def _route_kernel(n_exp, x_ref, g_ref, sh_ref, sc_ref, rw_ref, rb_ref, tri_ref,
                  hp_ref, idx_ref, rank_ref, wt_ref, cnt_ref, run_ref):
    @pl.when(pl.program_id(0) == 0)
    def _():
        run_ref[...] = jnp.zeros_like(run_ref)

    h = _norm_mod(x_ref[...], g_ref[0], sc_ref[0], sh_ref[0])
    _store_row_tiles(hp_ref, _pack_halves(h))
    scores =_sigmoid(_dot_nt(rw_ref[...], h.astype(BF16)))
    biased = scores + rb_ref[...]
    per = n_exp // N_EXPERT_GROUPS
    assert per == 4
    rows_b = [biased[e:e + 1, :] for e in range(n_exp)]
    rows_s = [scores[e:e + 1, :] for e in range(n_exp)]
    best_sum, best_g = None, None
    for gi in range(N_EXPERT_GROUPS):
        a, b, c, d = rows_b[per * gi:per * gi + per]
        hi1, lo1 = jnp.maximum(a, b), jnp.minimum(a, b)
        hi2, lo2 = jnp.maximum(c, d), jnp.minimum(c, d)
        top2 = jnp.maximum(hi1, hi2) + jnp.maximum(jnp.minimum(hi1, hi2), jnp.maximum(lo1, lo2))
        if gi == 0:
            best_sum, best_g = top2, jnp.zeros_like(top2, dtype=jnp.int32)
        else:
            upd = top2 > best_sum
            best_sum = jnp.where(upd, top2, best_sum)
            best_g = jnp.where(upd, gi, best_g)
    vb, vs = [], []
    for k in range(per):
        accb, accs = rows_b[k], rows_s[k]
        for gi in range(1, N_EXPERT_GROUPS):
            sel = best_g == gi
            accb = jnp.where(sel, rows_b[per * gi + k], accb)
            accs = jnp.where(sel, rows_s[per * gi + k], accs)
        vb.append(accb)
        vs.append(accs)

    def first_argmax(vals, exclude):
        bv, bi, bs = None, None, None
        for k in range(per):
            v = vals[k] if exclude is None else jnp.where(exclude == k, -jnp.inf, vals[k])
            if k == 0:
                bv, bi, bs = v, jnp.zeros_like(best_g), vs[0]
            else:
                upd = v > bv
                bv = jnp.where(upd, v, bv)
                bi = jnp.where(upd, k, bi)
                bs = jnp.where(upd, vs[k], bs)
        return bi, bs

    i1, s1 = first_argmax(vb, None)
    i2, s2 = first_argmax(vb, i1)
    tot = s1 + s2
    e1 = best_g * per + i1
    e2 = best_g * per + i2
    idx_ref[0:1, :] = e1
    idx_ref[1:2, :] = e2
    wt_ref[0:1, :] = s1 / tot
    wt_ref[1:2, :] = s2 / tot
    eids = lax.broadcasted_iota(jnp.int32, (n_exp, e1.shape[1]), 0)
    hit1 = eids == e1
    hit2 = eids == e2
    oh1 = jnp.where(hit1, 1.0, 0.0)
    oh2 = jnp.where(hit2, 1.0, 0.0)
    p1 = _dot(oh1.astype(BF16), tri_ref[...])
    p2 = _dot(oh2.astype(BF16), tri_ref[...])
    c1 = jnp.sum(oh1, axis=1, keepdims=True)
    c2 = jnp.sum(oh2, axis=1, keepdims=True)
    run = run_ref[...]
    rank_ref[0:1, :] = jnp.sum(jnp.where(hit1, run + p1, 0.0), axis=0, keepdims=True).astype(jnp.int32)
    rank_ref[1:2, :] = jnp.sum(jnp.where(hit2, run + c1 + p2, 0.0), axis=0, keepdims=True).astype(jnp.int32)
    run_ref[...] = run + c1 + c2
    cnt_ref[...] = run + c1 + c2


def _route(dm, x, gain, mods, layer, rw_t, rb):
    t, d = x.shape
    n_exp = rw_t.shape[0]
    tm = dm.tm
    rt = d // 2 // LANES
    tri = (jnp.arange(tm)[:, None] < jnp.arange(tm)[None, :]).astype(BF16)
    return pl.pallas_call(
        functools.partial(_route_kernel, n_exp),
        grid=(t // tm,),
        in_specs=[pl.BlockSpec((tm, d), lambda i: (i, 0)),
                  pl.BlockSpec((1, 1, d), lambda i: (layer, 0, 0)),
                  pl.BlockSpec((1, 1, d), _mod_idx(dm, layer, 3, tm)),
                  pl.BlockSpec((1, 1, d), _mod_idx(dm, layer, 4, tm)),
                  pl.BlockSpec((n_exp, d), lambda i: (0, 0)),
                  pl.BlockSpec((n_exp, 1), lambda i: (0, 0)),
                  pl.BlockSpec((tm, tm), lambda i: (0, 0))],
        out_specs=[pl.BlockSpec((tm * rt, LANES), lambda i: (i, 0)),
                   pl.BlockSpec((2, tm), lambda i: (0, i)),
                   pl.BlockSpec((2, tm), lambda i: (0, i)),
                   pl.BlockSpec((2, tm), lambda i: (0, i)),
                   pl.BlockSpec((n_exp, 1), lambda i: (0, 0))],
        out_shape=[jax.ShapeDtypeStruct((t * rt, LANES), U32),
                   jax.ShapeDtypeStruct((2, t), jnp.int32),
                   jax.ShapeDtypeStruct((2, t), jnp.int32),
                   jax.ShapeDtypeStruct((2, t), F32),
                   jax.ShapeDtypeStruct((n_exp, 1), F32)],
        scratch_shapes=[pltpu.VMEM((n_exp, 1), F32)],
        compiler_params=_cp(("arbitrary",)),
        name="moe_route",
    )(x, gain, mods, mods, rw_t, rb, tri)


def _slot_plan(idx, rank, counts, n_exp, n_slots):
    counts = counts.reshape(n_exp).astype(jnp.int32)
    padded = ((counts + TM_MOE - 1) // TM_MOE) * TM_MOE
    ends = jnp.cumsum(padded)
    starts = ends - padded
    start_of = jnp.sum(jnp.where(idx[..., None] == jnp.arange(n_exp, dtype=jnp.int32), starts, 0), axis=-1)
    dest = (start_of + rank).reshape(-1).astype(jnp.int32)
    tile_start = jnp.arange(n_slots // TM_MOE, dtype=jnp.int32) * TM_MOE
    tile_exp = jnp.minimum(jnp.sum((ends[None, :] <= tile_start[:, None]).astype(jnp.int32), axis=1), n_exp - 1)
    tile_ok = (tile_start < ends[-1]).astype(jnp.int32)
    pad_lo = jnp.concatenate([starts + counts, ends[-1:]]).astype(jnp.int32)
    pad_n = (padded - counts).astype(jnp.int32)
    return dest, tile_exp.astype(jnp.int32), tile_ok, pad_lo, pad_n


def _scatter_kernel(n_exp, t_total, rows, rt, dest_ref, padlo_ref, padn_ref, hp_ref, xs_ref, zrow, sem):
    i = pl.program_id(0)
    base = i * rows
    tile_rows = TM_MOE * rt

    def slot(s):
        return xs_ref.at[pl.ds(pl.multiple_of(s * rt, rt), rt)]

    @pl.when(i == 0)
    def _():
        zrow[...] = jnp.zeros_like(zrow)
        for e in range(n_exp):
            lo = padlo_ref[e]

            def pad_copy(r, lo=lo):
                return pltpu.make_async_copy(zrow.at[pl.ds(0, rt)], slot(lo + r), sem)

            def start(r, c, pad_copy=pad_copy):
                pad_copy(r).start()
                return c

            def wait(r, c, pad_copy=pad_copy):
                pad_copy(r).wait()
                return c

            lax.fori_loop(0, padn_ref[e], start, 0)
            lax.fori_loop(0, padn_ref[e], wait, 0)

        tail0 = padlo_ref[n_exp]

        def tail_copy(k):
            row = pl.multiple_of((tail0 + k * TM_MOE) * rt, tile_rows)
            return pltpu.make_async_copy(zrow, xs_ref.at[pl.ds(row, tile_rows)], sem)

        def tail_start(k, c):
            tail_copy(k).start()
            return c

        def tail_wait(k, c):
            tail_copy(k).wait()
            return c

        n_tail = (xs_ref.shape[0] // rt - tail0) // TM_MOE
        lax.fori_loop(0, n_tail, tail_start, 0)
        lax.fori_loop(0, n_tail, tail_wait, 0)

    def copies(r):
        src = hp_ref.at[pl.ds(pl.multiple_of(r * rt, rt), rt)]
        return (pltpu.make_async_copy(src, slot(dest_ref[base + r]), sem),
                pltpu.make_async_copy(src, slot(dest_ref[t_total + base + r]), sem))

    def start(gi, c):
        for u in range(DMA_UNROLL):
            for cp in copies(gi * DMA_UNROLL + u):
                cp.start()
        return c

    def wait(gi, c):
        for u in range(DMA_UNROLL):
            for cp in copies(gi * DMA_UNROLL + u):
                cp.wait()
        return c

    lax.fori_loop(0, rows // DMA_UNROLL, start, 0)
    lax.fori_loop(0, rows // DMA_UNROLL, wait, 0)


def _scatter_rows(hp, dest, pad_lo, pad_n, n_slots, rows, rt):
    t = hp.shape[0] // rt
    n_exp = pad_n.shape[0]
    return pl.pallas_call(
        functools.partial(_scatter_kernel, n_exp, t, rows, rt),
        grid_spec=pltpu.PrefetchScalarGridSpec(
            num_scalar_prefetch=3,
            grid=(t // rows,),
            in_specs=[pl.BlockSpec((rows * rt, LANES), lambda i, *_: (i, 0))],
            out_specs=pl.BlockSpec(memory_space=pl.ANY),
            scratch_shapes=[pltpu.VMEM((TM_MOE * rt, LANES), U32), pltpu.SemaphoreType.DMA(())]),
        out_shape=jax.ShapeDtypeStruct((n_slots * rt, LANES), U32),
        compiler_params=_cp(("arbitrary",)),
        name="moe_scatter",
    )(dest, pad_lo, pad_n, hp)


def _expert_kernel(rt, te_ref, ok_ref, x_ref, wg_ref, wu_ref, wd_ref, o_ref, wg_b, wu_b, wd_b):
    i = pl.program_id(0)
    new_expert = jnp.logical_or(i == 0, te_ref[i] != te_ref[jnp.maximum(i - 1, 0)])

    @pl.when(new_expert)
    def _():
        wg_b[...] = wg_ref[0, 0].astype(BF16)
        wu_b[...] = wu_ref[0, 0].astype(BF16)
        wd_b[...] = wd_ref[0, 0].astype(BF16)

    @pl.when(ok_ref[i] == 1)
    def _():
        x_lo, x_hi = _unpack_halves(_load_row_tiles(x_ref, rt))
        x_lo, x_hi = x_lo.astype(BF16), x_hi.astype(BF16)
        half = x_lo.shape[1]
        h1 = _dot(x_lo, wg_b[:half, :]) + _dot(x_hi, wg_b[half:, :])
        h2 = _dot(x_lo, wu_b[:half, :]) + _dot(x_hi, wu_b[half:, :])
        act = (_silu(h1) * h2).astype(BF16)
        _store_row_tiles(o_ref, _pack_halves(_dot(act, wd_b[...])))

    @pl.when(ok_ref[i] == 0)
    def _():
        o_ref[...] = jnp.zeros_like(o_ref)


def _experts(x_sorted, tile_exp, tile_ok, w_gate, w_up, w_down, layer, rt):
    n_slots = x_sorted.shape[0] // rt
    d, f = w_gate.shape[-2:]
    return pl.pallas_call(
        functools.partial(_expert_kernel, rt),
        grid_spec=pltpu.PrefetchScalarGridSpec(
            num_scalar_prefetch=2,
            grid=(n_slots // TM_MOE,),
            in_specs=[pl.BlockSpec((TM_MOE * rt, LANES), lambda i, te, ok: (i, 0)),
                      pl.BlockSpec((1, 1, d, f), lambda i, te, ok: (layer, te[i], 0, 0)),
                      pl.BlockSpec((1, 1, d, f), lambda i, te, ok: (layer, te[i], 0, 0)),
                      pl.BlockSpec((1, 1, f, d), lambda i, te, ok: (layer, te[i], 0, 0))],
            out_specs=pl.BlockSpec((TM_MOE * rt, LANES), lambda i, te, ok: (i, 0)),
            scratch_shapes=[pltpu.VMEM((d, f), BF16), pltpu.VMEM((d, f), BF16), pltpu.VMEM((f, d), BF16)]),
        out_shape=jax.ShapeDtypeStruct((n_slots * rt, LANES), U32),
        compiler_params=_cp(("arbitrary",)),
        name="moe_experts",
    )(tile_exp, tile_ok, x_sorted, w_gate, w_up, w_down)


def _combine_kernel(t_total, rows, rt, dest_ref, x_ref, gate_ref, wt_ref, y_ref, o_ref, y0, y1, sem):
    base = pl.program_id(0) * rows

    def slot(s):
        return y_ref.at[pl.ds(pl.multiple_of(s * rt, rt), rt)]

    def copies(r):
        dst = pl.ds(pl.multiple_of(r * rt, rt), rt)
        return (pltpu.make_async_copy(slot(dest_ref[base + r]), y0.at[dst], sem),
                pltpu.make_async_copy(slot(dest_ref[t_total + base + r]), y1.at[dst], sem))

    def start(gi, c):
        for u in range(DMA_UNROLL):
            for cp in copies(gi * DMA_UNROLL + u):
                cp.start()
        return c

    def wait(gi, c):
        for u in range(DMA_UNROLL):
            for cp in copies(gi * DMA_UNROLL + u):
                cp.wait()
        return c

    lax.fori_loop(0, rows // DMA_UNROLL, start, 0)
    lax.fori_loop(0, rows // DMA_UNROLL, wait, 0)
    w0 = wt_ref[:, 0:1]
    w1 = wt_ref[:, 1:2]
    a_lo, a_hi = _unpack_halves(_load_row_tiles(y0, rt))
    b_lo, b_hi = _unpack_halves(_load_row_tiles(y1, rt))
    half = a_lo.shape[1]
    gate = gate_ref[0]
    o_ref[:, :half] = x_ref[:, :half] + gate[:, :half] * (w0 * a_lo + w1 * b_lo)
    o_ref[:, half:] = x_ref[:, half:] + gate[:, half:] * (w0 * a_hi + w1 * b_hi)


def _combine(dm, x, mods, layer, y_sorted, dest, wts_t):
    t, d = x.shape
    rows = min(dm.tm, 512)
    rt = d // 2 // LANES
    return pl.pallas_call(
        functools.partial(_combine_kernel, t, rows, rt),
        grid_spec=pltpu.PrefetchScalarGridSpec(
            num_scalar_prefetch=1,
            grid=(t // rows,),
            in_specs=[pl.BlockSpec((rows, d), lambda i, dst: (i, 0)),
                      pl.BlockSpec((1, 1, d), lambda i, dst: _mod_idx(dm, layer, 5, rows)(i)),
                      pl.BlockSpec((rows, 2), lambda i, dst: (i, 0)),
                      pl.BlockSpec(memory_space=pl.ANY)],
            out_specs=pl.BlockSpec((rows, d), lambda i, dst: (i, 0)),
            scratch_shapes=[pltpu.VMEM((rows * rt, LANES), U32), pltpu.VMEM((rows * rt, LANES), U32),
                            pltpu.SemaphoreType.DMA(())]),
        out_shape=jax.ShapeDtypeStruct((t, d), F32),
        compiler_params=_cp(("arbitrary",)),
        name="moe_combine",
    )(dest, x, mods, wts_t, y_sorted)


def _moe(dm, x, gain, mods, layer, rw_t, rb, w_gate, w_up, w_down):
    n_exp = rw_t.shape[0]
    hp, idx, rank, wts, counts = _route(dm, x, gain, mods, layer, rw_t, rb)
    n_slots = 2 * dm.t + n_exp * TM_MOE
    dest, tile_exp, tile_ok, pad_lo, pad_n = _slot_plan(idx, rank, counts, n_exp, n_slots)
    rt = dm.d // 2 // LANES
    x_sorted = _scatter_rows(hp, dest, pad_lo, pad_n, n_slots, min(dm.tm, 512), rt)
    y_sorted = _experts(x_sorted, tile_exp, tile_ok, w_gate, w_up, w_down, layer, rt)
    return _combine(dm, x, mods, layer, y_sorted, dest, wts.T)


def _final_norm_kernel(x_ref, g_ref, o_ref):
    x = x_ref[...]
    o_ref[...] = x * lax.rsqrt(jnp.mean(x * x, axis=-1, keepdims=True) + EPS) * g_ref[...]


def _final_norm(dm, x, gain):
    t, d = x.shape
    tm = dm.tm
    return pl.pallas_call(
        _final_norm_kernel,
        grid=(t // tm,),
        in_specs=[pl.BlockSpec((tm, d), lambda i: (i, 0)),
                  pl.BlockSpec((1, d), lambda i: (0, 0))],
        out_specs=pl.BlockSpec((tm, d), lambda i: (i, 0)),
        out_shape=jax.ShapeDtypeStruct((t, d), F32),
        compiler_params=_cp(("parallel",)),
        name="final_norm",
    )(x, gain.reshape(1, d))


def _rope_tables(dm):
    rows = dm.ls // GRID_W
    row = jnp.repeat(jnp.arange(rows, dtype=F32), GRID_W)
    col = jnp.tile(jnp.arange(GRID_W, dtype=F32), rows)
    half = QK_ROPE // 2
    freqs = jnp.power(ROPE_BASE, -jnp.arange(0, half, 2, dtype=F32) / half)
    ar, ac = row[:, None] * freqs, col[:, None] * freqs
    zeros = jnp.zeros((dm.ls, 128 - QK_ROPE), F32)
    ct = jnp.concatenate([jnp.cos(ar), jnp.cos(ar), jnp.cos(ac), jnp.cos(ac), zeros], axis=-1)
    st = jnp.concatenate([-jnp.sin(ar), jnp.sin(ar), -jnp.sin(ac), jnp.sin(ac), zeros], axis=-1)
    ct_c = jnp.concatenate([jnp.ones((dm.tc, QK_ROPE), F32), jnp.zeros((dm.tc, 128 - QK_ROPE), F32)], axis=-1)
    ct = jnp.concatenate([ct_c, jnp.tile(ct, (dm.bs, 1))], axis=0)
    st = jnp.concatenate([jnp.zeros((dm.tc, 128), F32), jnp.tile(st, (dm.bs, 1))], axis=0)
    return ct, st


def _swap_halves_cols(w):
    qt = QK_ROPE // 4
    return jnp.concatenate([w[..., qt:2 * qt], w[..., :qt], w[..., 3 * qt:], w[..., 2 * qt:3 * qt]], axis=-1)


def kernel(x_prompt, x_sample, c, state_ret, state_s5_re, state_s5_im, cache_ckv, cache_krope, c_ctx, ada_w, ada_b, norm_mix, norm_ffn, norm_final, even_w_in, even_w_out, ret_decay, s5_a_re, s5_a_im, s5_log_dt, s5_b_re, s5_b_im, s5_c_re, s5_c_im, s5_d, s5_w_glu, mla_w_in, mla_q_norm, mla_w_uq, mla_kv_norm, mla_w_ukv, mla_w_out, router_w, router_bias, moe_w_gate, moe_w_up, moe_w_down):
    dm = _Dims(x_prompt, x_sample)
    d = dm.d
    depth = ada_w.shape[0]
    n_exp = router_w.shape[1]
    past = cache_ckv.shape[2]
    q_lora = mla_q_norm.shape[1]
    kv_lora = mla_kv_norm.shape[1]

    x = jnp.concatenate([x_prompt.reshape(dm.tc, d), x_sample.reshape(dm.ts, d)], axis=0)
    cond = jnp.zeros((dm.rp, d), F32).at[0].set(c_ctx).at[1:1 + dm.bs].set(c)
    mods = _adaln(cond, ada_w, ada_b, tn=d * N_MOD // 8).reshape(depth * dm.rp * N_MOD, 1, d)
    g_mix = norm_mix.reshape(depth, 1, d)
    g_ffn = norm_ffn.reshape(depth, 1, d)
    rw_t = router_w.T.astype(BF16)
    rb = router_bias.astype(F32).reshape(n_exp, 1)
    ct, st = _rope_tables(dm)

    rets, s5r, s5i, ckvs, krs = [], [], [], [], []
    for layer in range(depth):
        if layer % 2 == 0:
            i = layer // 2
            ret_w = even_w_out.shape[1] - s5_d.shape[1]
            z = _norm_mod_matmul(dm, x, g_mix, mods, layer, 0, even_w_in[i].astype(BF16), BF16,
                                 tn=even_w_in.shape[2] // 5, name="even_in_proj")
            log_gamma = -jnp.exp(ret_decay[i].astype(F32))
            ro_c, sfin = _retention(z, ret_w // RET_HEADS, log_gamma, None, dm.bc, dm.lc, 0, "retention_ctx")
            ro_s, _ = _retention(z, ret_w // RET_HEADS, log_gamma, state_ret[:, i].astype(F32), dm.bs, dm.ls,
                                 dm.tc // dm.ls, "retention_lat")
            ret_out = jnp.concatenate([ro_c, ro_s], axis=0)
            ops = _s5_operators(s5_a_re[i], s5_a_im[i], s5_log_dt[i], s5_b_re[i], s5_b_im[i],
                                s5_c_re[i], s5_c_im[i])
            y, f_re, f_im = _s5(dm, z[:, 4 * ret_w:], ops, s5_d[i], state_s5_re[:, i], state_s5_im[:, i])
            s5_out = _glu(dm, y, s5_w_glu[i].astype(BF16))
            w_out = even_w_out[i].astype(BF16)
            x = _matmul_residual(dm, x, mods, layer, 2, [ret_out, s5_out], [w_out[:ret_w], w_out[ret_w:]],
                                 tn=d // 2, name="even_out_proj")
            rets.append(sfin)
            s5r.append(f_re)
            s5i.append(f_im)
        else:
            j = layer // 2
            w_in = mla_w_in[j]
            w_in_ext = jnp.concatenate([w_in, _swap_halves_cols(w_in[:, q_lora + kv_lora:])], axis=1).astype(BF16)
            a = _norm_mod_matmul(dm, x, g_mix, mods, layer, 0, w_in_ext, F32, tn=w_in_ext.shape[1],
                                 name="mla_in_proj")
            w_uq = mla_w_uq[j].reshape(q_lora, MLA_HEADS, QK_NOPE + QK_ROPE)
            w_uq_ext = jnp.concatenate([w_uq, _swap_halves_cols(w_uq[..., QK_NOPE:])], axis=-1)
            w_uq_ext = w_uq_ext.reshape(q_lora, MLA_HEADS * QK_PAD).astype(BF16)
            q, ckv_n, kr = _mla_q(dm, a, mla_q_norm[j].reshape(1, q_lora), mla_kv_norm[j].reshape(1, kv_lora),
                                  ct, st, w_uq_ext, q_lora, kv_lora)
            ckv_lat = jnp.concatenate([cache_ckv[:, j].astype(F32), ckv_n[dm.tc:].reshape(dm.bs, dm.ls, kv_lora)], axis=1)
            kr_cache = jnp.concatenate([cache_krope[:, j].astype(F32),
                                        jnp.zeros((dm.bs, past, 128 - QK_ROPE), F32)], axis=-1)
            kr_lat = jnp.concatenate([kr_cache, kr[dm.tc:].reshape(dm.bs, dm.ls, 128)], axis=1)
            lk = past + dm.ls
            ckv_keys = jnp.concatenate([ckv_lat.reshape(dm.bs * lk, kv_lora), ckv_n[:dm.tc]], axis=0)
            kr_keys = jnp.concatenate([kr_lat.reshape(dm.bs * lk, 128), kr[:dm.tc]], axis=0)
            w_ukv = mla_w_ukv[j].reshape(kv_lora, MLA_HEADS, QK_NOPE + V_HEAD)
            w_uk = w_ukv[..., :QK_NOPE].reshape(kv_lora, MLA_HEADS * QK_NOPE).astype(BF16)
            w_uv = w_ukv[..., QK_NOPE:].reshape(kv_lora, MLA_HEADS * V_HEAD).astype(BF16)
            k_all, v_all = _kv_expand(ckv_keys, kr_keys, w_uk, w_uv)
            o_c = _attention(q, k_all, v_all, dm.bc, dm.lc, dm.lc, dm.lc, MLA_HEADS, 0, dm.bs * lk, "attn_ctx")
            o_s = _attention(q, k_all, v_all, dm.bs, dm.ls, lk, min(TQ_MAX, dm.ls), 1, dm.tc, 0, "attn_lat")
            o = jnp.concatenate([o_c, o_s], axis=0)
            x = _matmul_residual(dm, x, mods, layer, 2, [o], [mla_w_out[j].astype(BF16)], tn=d // 2,
                                 name="mla_out_proj")
            ckvs.append(ckv_n[:dm.tc].reshape(dm.bc, dm.lc, kv_lora))
            krs.append(kr[:dm.tc, :QK_ROPE].reshape(dm.bc, dm.lc, QK_ROPE))
        x = _moe(dm, x, g_ffn, mods, layer, rw_t, rb, moe_w_gate, moe_w_up, moe_w_down)

    y = _final_norm(dm, x, norm_final)
    y_prompt = y[:dm.tc].reshape(dm.bc, dm.lc, d)
    y_sample = y[dm.tc:].reshape(dm.bs, dm.ls, d)
    return (y_prompt, y_sample, jnp.stack(rets, axis=1), jnp.stack(s5r, axis=1), jnp.stack(s5i, axis=1),
            jnp.stack(ckvs, axis=1), jnp.stack(krs, axis=1))
```

```python
import functools
import math

import jax
import jax.numpy as jnp
import numpy as np
from jax import lax
from jax.experimental import pallas as pl
from jax.experimental.pallas import tpu as pltpu

F32 = jnp.float32
BF16 = jnp.bfloat16
U32 = jnp.uint32
EPS = 1e-6

RET_HEADS = 4
RET_CHUNK = 128
RET_UNROLL = 4
S5_GROUP = 16
S5_Q = 16
S5_LANES = 128
LANES = 128
MLA_HEADS = 16
QK_NOPE = 128
QK_ROPE = 64
V_HEAD = 128
QK_PAD = 256
V_PAD = 256
GRID_W = 64
ROPE_BASE = 10000.0
N_EXPERT_GROUPS = 4
N_MOD = 6

VMEM_LIMIT = 56 * 1024 * 1024
TM_MAX = 1024
TQ_MAX = 512
KB_MAX = 768
LAT_HEADS_PER_STEP = 2
TM_MOE = 256
DMA_UNROLL = 8
HI_MASK = np.uint32(0xFFFF0000)


def _cp(sem, vmem=VMEM_LIMIT):
    return pltpu.CompilerParams(dimension_semantics=sem, vmem_limit_bytes=vmem)


def _sigmoid(x):
    return 1.0 / (1.0 + jnp.exp(-x))


def _silu(x):
    return x * _sigmoid(x)


def _gelu_tanh(x):
    return 0.5 * x * (1.0 + jnp.tanh(math.sqrt(2.0 / math.pi) * (x + 0.044715 * (x * x * x))))


def _dot(a, b):
    return jnp.dot(a, b, preferred_element_type=F32)


def _dot_nt(a, b):
    return lax.dot_general(a, b, (((1,), (1,)), ((), ())), preferred_element_type=F32)


def _dot_tn(a, b):
    return lax.dot_general(a, b, (((0,), (0,)), ((), ())), preferred_element_type=F32)


def _pack_halves(x):
    half = x.shape[1] // 2
    xb = x.astype(BF16).astype(F32)
    lo = lax.bitcast_convert_type(xb[:, :half], U32) >> 16
    hi = lax.bitcast_convert_type(xb[:, half:], U32) & HI_MASK
    return hi | lo


def _unpack_halves(w):
    return (lax.bitcast_convert_type(w << 16, F32), lax.bitcast_convert_type(w & HI_MASK, F32))


def _store_row_tiles(ref, w):
    r, rt = w.shape[0], w.shape[1] // LANES
    for s in range(rt):
        ref[pl.ds(s, r, stride=rt), :] = w[:, s * LANES:(s + 1) * LANES]


def _load_row_tiles(ref, rt):
    r = ref.shape[0] // rt
    return jnp.concatenate([ref[pl.ds(s, r, stride=rt), :] for s in range(rt)], axis=1)


class _Dims:
    def __init__(self, x_prompt, x_sample):
        self.bc, self.lc, self.d = x_prompt.shape
        self.bs, self.ls, _ = x_sample.shape
        self.tc = self.bc * self.lc
        self.ts = self.bs * self.ls
        self.t = self.tc + self.ts
        self.rp = -(-(1 + self.bs) // 8) * 8
        self.tm = min(TM_MAX, self.ls)
        assert self.tc % self.tm == 0 and self.ls % self.tm == 0

    def mod_row(self, i, tm):
        nct = self.tc // tm
        return jnp.where(i < nct, 0, 1 + (i - nct) // (self.ls // tm))


def _mod_idx(dm, layer, k, tm):
    def idx(i, *_):
        return ((layer * dm.rp + dm.mod_row(i, tm)) * N_MOD + k, 0, 0)
    return idx


def _adaln_kernel(c_ref, w_ref, b_ref, o_ref):
    cs = _silu(c_ref[...]).astype(BF16)
    o_ref[0] = _dot(cs, w_ref[0].astype(BF16)) + b_ref[0]


def _adaln(cond, ada_w, ada_b, tn):
    depth, d, n = ada_w.shape
    rp = cond.shape[0]
    return pl.pallas_call(
        _adaln_kernel,
        grid=(depth, n // tn),
        in_specs=[pl.BlockSpec((rp, d), lambda l, j: (0, 0)),
                  pl.BlockSpec((1, d, tn), lambda l, j: (l, 0, j)),
                  pl.BlockSpec((1, 1, tn), lambda l, j: (l, 0, j))],
        out_specs=pl.BlockSpec((1, rp, tn), lambda l, j: (l, 0, j)),
        out_shape=jax.ShapeDtypeStruct((depth, rp, n), F32),
        compiler_params=_cp(("parallel", "parallel")),
        name="adaln",
    )(cond, ada_w, ada_b.reshape(depth, 1, n))


def _norm_mod(x, g, sc, sh):
    y = x * lax.rsqrt(jnp.mean(x * x, axis=-1, keepdims=True) + EPS) * g
    return y * (1.0 + sc) + sh


def _nmm_kernel(x_ref, g_ref, sh_ref, sc_ref, w_ref, o_ref, hn_ref):
    @pl.when(pl.program_id(1) == 0)
    def _():
        hn_ref[...] = _norm_mod(x_ref[...], g_ref[0], sc_ref[0], sh_ref[0]).astype(BF16)

    o_ref[...] = _dot(hn_ref[...], w_ref[...]).astype(o_ref.dtype)


def _norm_mod_matmul(dm, x, gain, mods, layer, k_shift, w, out_dtype, tn, name):
    t, d = x.shape
    n = w.shape[1]
    tm = dm.tm
    return pl.pallas_call(
        _nmm_kernel,
        grid=(t // tm, n // tn),
        in_specs=[pl.BlockSpec((tm, d), lambda i, j: (i, 0)),
                  pl.BlockSpec((1, 1, d), lambda i, j: (layer, 0, 0)),
                  pl.BlockSpec((1, 1, d), _mod_idx(dm, layer, k_shift, tm)),
                  pl.BlockSpec((1, 1, d), _mod_idx(dm, layer, k_shift + 1, tm)),
                  pl.BlockSpec((d, tn), lambda i, j: (0, j))],
        out_specs=pl.BlockSpec((tm, tn), lambda i, j: (i, j)),
        out_shape=jax.ShapeDtypeStruct((t, n), out_dtype),
        scratch_shapes=[pltpu.VMEM((tm, d), BF16)],
        compiler_params=_cp(("parallel", "arbitrary")),
        name=name,
    )(x, gain, mods, mods, w)


def _mmres_kernel(n_in, x_ref, gate_ref, *refs):
    a_refs, w_refs, o_ref = refs[:n_in], refs[n_in:2 * n_in], refs[2 * n_in]
    acc = _dot(a_refs[0][...], w_refs[0][...])
    for a_ref, w_ref in zip(a_refs[1:], w_refs[1:]):
        acc += _dot(a_ref[...], w_ref[...])
    o_ref[...] = x_ref[...] + gate_ref[0] * acc


def _matmul_residual(dm, x, mods, layer, k_gate, acts, ws, tn, name):
    t, d = x.shape
    n_in = len(acts)
    tm = dm.tm

    def gate_idx(i, j):
        return ((layer * dm.rp + dm.mod_row(i, tm)) * N_MOD + k_gate, 0, j)

    in_specs = [pl.BlockSpec((tm, tn), lambda i, j: (i, j)),
                pl.BlockSpec((1, 1, tn), gate_idx)]
    in_specs += [pl.BlockSpec((tm, a.shape[1]), lambda i, j: (i, 0)) for a in acts]
    in_specs += [pl.BlockSpec((w.shape[0], tn), lambda i, j: (0, j)) for w in ws]
    return pl.pallas_call(
        functools.partial(_mmres_kernel, n_in),
        grid=(t // tm, d // tn),
        in_specs=in_specs,
        out_specs=pl.BlockSpec((tm, tn), lambda i, j: (i, j)),
        out_shape=jax.ShapeDtypeStruct((t, d), F32),
        compiler_params=_cp(("parallel", "parallel")),
        name=name,
    )(x, mods, *acts, *ws)


def _ret_kernel(has_s0, nch, scale, lg_ref, q_ref, k_ref, v_ref, g_ref, *refs):
    if has_s0:
        s0_ref, o_ref, sfin_ref, oacc_ref, st_ref = refs
    else:
        o_ref, sfin_ref, oacc_ref, st_ref = refs
    c = RET_CHUNK
    h = pl.program_id(1)
    lgf = lg_ref[0, h]
    lgb = lg_ref[1, h]
    dv = v_ref.shape[1]

    ii = lax.broadcasted_iota(jnp.int32, (c, c), 0).astype(F32)
    jj = lax.broadcasted_iota(jnp.int32, (c, c), 1).astype(F32)
    diff = ii - jj
    dtot = (jnp.where(diff >= 0, jnp.exp(lgf * jnp.maximum(diff, 0.0)), 0.0)
            + jnp.where(diff <= 0, jnp.exp(lgb * jnp.maximum(-diff, 0.0)), 0.0)) * scale
    pos = lax.broadcasted_iota(jnp.int32, (c, 1), 0).astype(F32)
    qdec_f = jnp.exp(lgf * (pos + 1.0)) * scale
    kdec_f = jnp.exp(lgf * (c - 1.0 - pos))
    qdec_b = jnp.exp(lgb * (c - pos)) * scale
    kdec_b = jnp.exp(lgb * pos)
    cdec_f = jnp.exp(jnp.full((1, dv), lgf * c, F32))
    cdec_b = jnp.exp(jnp.full((1, dv), lgb * c, F32))

    if has_s0:
        st_ref[...] = s0_ref[0, 0, 0]
    else:
        st_ref[...] = jnp.zeros_like(st_ref)

    def fwd(n, carry):
        r = pl.multiple_of(n * c, c)
        qn = q_ref[pl.ds(r, c), :]
        kn = k_ref[pl.ds(r, c), :]
        vn = v_ref[pl.ds(r, c), :]
        p = (_dot_nt(qn, kn) * dtot).astype(BF16)
        o = _dot(p, vn)
        o += _dot((qn.astype(F32) * qdec_f).astype(BF16), st_ref[...].astype(BF16))
        oacc_ref[pl.ds(r, c), :] = o
        kd = (kn.astype(F32) * kdec_f).astype(BF16)
        st_ref[...] = st_ref[...] * cdec_f + _dot_tn(kd, vn)
        return carry

    lax.fori_loop(0, nch, fwd, 0, unroll=min(nch, RET_UNROLL))
    sfin_ref[0, 0, 0] = st_ref[...]

    if has_s0:
        st_ref[...] = s0_ref[0, 1, 0]
    else:
        st_ref[...] = jnp.zeros_like(st_ref)

    def bwd(m, carry):
        r = pl.multiple_of((nch - 1 - m) * c, c)
        qn = q_ref[pl.ds(r, c), :]
        kn = k_ref[pl.ds(r, c), :]
        vn = v_ref[pl.ds(r, c), :]
        oacc_ref[pl.ds(r, c), :] += _dot((qn.astype(F32) * qdec_b).astype(BF16), st_ref[...].astype(BF16))
        kd = (kn.astype(F32) * kdec_b).astype(BF16)
        st_ref[...] = st_ref[...] * cdec_b + _dot_tn(kd, vn)
        return carry

    lax.fori_loop(0, nch, bwd, 0, unroll=min(nch, RET_UNROLL))
    sfin_ref[0, 1, 0] = st_ref[...]

    o = oacc_ref[...]
    o = o * lax.rsqrt(jnp.mean(o * o, axis=-1, keepdims=True) + EPS)
    o_ref[...] = (_silu(g_ref[...].astype(F32)) * o).astype(o_ref.dtype)


def _retention(z, dk, log_gamma, s0, nb, seq, row_blk0, name):
    hh = RET_HEADS
    dv = dk
    nch = seq // RET_CHUNK
    has_s0 = s0 is not None
    scale = float(dk) ** -0.5

    def col(off):
        return lambda b, h: (row_blk0 + b, off + h)

    in_specs = [pl.BlockSpec(memory_space=pltpu.SMEM),
                pl.BlockSpec((seq, dk), col(0)),
                pl.BlockSpec((seq, dk), col(hh)),
                pl.BlockSpec((seq, dv), col(2 * hh)),
                pl.BlockSpec((seq, dv), col(3 * hh))]
    args = [log_gamma, z, z, z, z]
    if has_s0:
        in_specs.append(pl.BlockSpec((1, 2, 1, dk, dv), lambda b, h: (b, 0, h, 0, 0)))
        args.append(s0)
    return pl.pallas_call(
        functools.partial(_ret_kernel, has_s0, nch, scale),
        grid=(nb, hh),
        in_specs=in_specs,
        out_specs=[pl.BlockSpec((seq, dv), lambda b, h: (b, h)),
                   pl.BlockSpec((1, 2, 1, dk, dv), lambda b, h: (b, 0, h, 0, 0))],
        out_shape=[jax.ShapeDtypeStruct((nb * seq, hh * dv), BF16),
                   jax.ShapeDtypeStruct((nb, 2, hh, dk, dv), F32)],
        scratch_shapes=[pltpu.VMEM((seq, dv), F32), pltpu.VMEM((dk, dv), F32)],
        compiler_params=_cp(("parallel", "parallel")),
        name=name,
    )(*args)


def _s5_kernel(rc, ncc, bc, ncs, bs, u_ref, w1_ref, v_ref, lam_ref, d_ref, x0_ref,
               y_ref, fin_ref, sre, sim, are, aim, bre, bim):
    u = u_ref[0]
    z = _dot(u, w1_ref[0])
    nq = u.shape[1]
    y0 = z[:, :nq] + d_ref[0] * u.astype(F32)
    sre[...] = z[:, nq:nq + S5_LANES]
    sim[...] = z[:, nq + S5_LANES:nq + 2 * S5_LANES]
    lr = lam_ref[0, 0:1, :]
    li = lam_ref[0, 1:2, :]

    def scan(row0, nc, b, xr0, xi0):
        is_f = lax.broadcasted_iota(jnp.int32, (b, S5_LANES), 1) < S5_LANES // 2

        def body(s, carry):
            xr, xi = carry
            rf = pl.multiple_of(row0 + s * b, b)
            rb = pl.multiple_of(row0 + (nc - 1 - s) * b, b)
            are[pl.ds(rf, b), :] = xr
            aim[pl.ds(rf, b), :] = xi
            bre[pl.ds(rb, b), :] = xr
            bim[pl.ds(rb, b), :] = xi
            sr = jnp.where(is_f, sre[pl.ds(rf, b), :], sre[pl.ds(rb, b), :])
            si = jnp.where(is_f, sim[pl.ds(rf, b), :], sim[pl.ds(rb, b), :])
            return xr * lr - xi * li + sr, xi * lr + xr * li + si

        return lax.fori_loop(0, nc, body, (xr0, xi0))

    zero = jnp.zeros((bc, S5_LANES), F32)
    fr, fi = scan(0, ncc, bc, zero, zero)
    fin_ref[0, 0] = fr
    fin_ref[0, 1] = fi
    scan(rc, ncs, bs, x0_ref[0, 0], x0_ref[0, 1])
    y_ref[0] = (y0 + _dot(are[...].astype(BF16), v_ref[0, 0]) + _dot(aim[...].astype(BF16), v_ref[0, 1])
                + _dot(bre[...].astype(BF16), v_ref[0, 2]) + _dot(bim[...].astype(BF16), v_ref[0, 3])
                ).astype(y_ref.dtype)


def _cmul(ar, ai, br, bi):
    return ar * br - ai * bi, ar * bi + ai * br


def _cexp(re, im):
    e = jnp.exp(re)
    return e * jnp.cos(im), e * jnp.sin(im)


def _s5_operators(a_re, a_im, log_dt, b_re, b_im, c_re, c_im):
    q = S5_Q
    hp = lax.Precision.HIGHEST
    a_re, a_im = a_re.astype(F32), a_im.astype(F32)
    dt = jnp.exp(log_dt.astype(F32))[..., None]
    ldr, ldi = a_re * dt, a_im * dt
    lbr, lbi = _cexp(ldr, ldi)
    den = a_re * a_re + a_im * a_im
    fr = ((lbr - 1.0) * a_re + lbi * a_im) / den
    fi = (lbi * a_re - (lbr - 1.0) * a_im) / den
    bbr, bbi = _cmul(fr[..., None], fi[..., None], b_re.astype(F32), b_im.astype(F32))
    c_re, c_im = c_re.astype(F32), c_im.astype(F32)
    g, p = a_re.shape[1], a_re.shape[2]
    cg = bbr.shape[-1]
    steps = jnp.arange(q + 1, dtype=F32)
    pwr, pwi = _cexp(ldr[..., None] * steps, ldi[..., None] * steps)
    cpr, cpi = _cmul(c_re[..., None], c_im[..., None], pwr[:, :, None, :, :q], pwi[:, :, None, :, :q])
    kern = (jnp.einsum('dgcpt,dgpe->dgtce', cpr, bbr, precision=hp)
            - jnp.einsum('dgcpt,dgpe->dgtce', cpi, bbi, precision=hp))
    ti = jnp.arange(q)
    lag = ti[None, :] - ti[:, None]
    m_f = jnp.where((lag >= 0)[None, :, :, None, None], kern[0][:, jnp.clip(lag, 0, q - 1)], 0.0)
    m_b = jnp.where((lag <= 0)[None, :, :, None, None], kern[1][:, jnp.clip(-lag, 0, q - 1)], 0.0)
    m_tot = jnp.transpose(m_f + m_b, (0, 1, 4, 2, 3)).reshape(g, q * cg, q * cg)

    def state_in(d, t_idx):
        wr, wi = _cmul(pwr[d][..., t_idx][..., None], pwi[d][..., t_idx][..., None],
                       bbr[d][:, :, None, :], bbi[d][:, :, None, :])
        tr = lambda w: jnp.transpose(w, (0, 2, 3, 1)).reshape(g, q * cg, p)
        return tr(wr), tr(wi)

    wfr, wfi = state_in(0, q - 1 - ti)
    wbr, wbi = state_in(1, ti)
    w1 = jnp.concatenate([m_tot, wfr, wbr, wfi, wbi], axis=-1)

    def state_out(d, t_idx):
        vr, vi = _cmul(c_re[d][..., None], c_im[d][..., None],
                       pwr[d][:, None, :, :][..., t_idx], pwi[d][:, None, :, :][..., t_idx])
        tr = lambda w: jnp.transpose(w, (0, 2, 3, 1)).reshape(g, p, q * cg)
        return tr(vr), tr(vi)

    vfr, vfi = state_out(0, ti + 1)
    vbr, vbi = state_out(1, q - ti)
    zf = jnp.zeros_like(vfr)
    v = jnp.stack([jnp.concatenate([vfr, zf], axis=1), jnp.concatenate([-vfi, zf], axis=1),
                   jnp.concatenate([zf, vbr], axis=1), jnp.concatenate([zf, -vbi], axis=1)], axis=1)
    lam_pack = jnp.stack([jnp.concatenate([pwr[0][..., q], pwr[1][..., q]], axis=-1),
                          jnp.concatenate([pwi[0][..., q], pwi[1][..., q]], axis=-1)], axis=1)
    return w1.astype(BF16), v.astype(BF16), lam_pack


def _s5(dm, u_tok, ops, d_skip, x0_re, x0_im):
    w1, v, lam_pack = ops
    g = w1.shape[0]
    p = lam_pack.shape[2] // 2
    assert 2 * p == S5_LANES
    q, cg = S5_Q, S5_GROUP
    ncc, ncs = dm.lc // q, dm.ls // q
    rc, rs = ncc * dm.bc, ncs * dm.bs
    r = rc + rs

    def to_group_major(x, b, nc):
        return x.reshape(b, nc, q, g, cg).transpose(3, 1, 0, 2, 4).reshape(g, nc * b, q * cg)

    u_g = jnp.concatenate([to_group_major(u_tok[:dm.tc], dm.bc, ncc),
                           to_group_major(u_tok[dm.tc:], dm.bs, ncs)], axis=1)
    d_g = jnp.tile(d_skip.astype(F32).reshape(g, 1, cg), (1, q, 1)).reshape(g, 1, q * cg)
    x0 = jnp.stack([jnp.concatenate([x0_re[:, 0], x0_re[:, 1]], axis=-1),
                    jnp.concatenate([x0_im[:, 0], x0_im[:, 1]], axis=-1)], axis=0)
    x0 = x0.transpose(2, 0, 1, 3).astype(F32)
    nw = w1.shape[2]
    y_g, fin = pl.pallas_call(
        functools.partial(_s5_kernel, rc, ncc, dm.bc, ncs, dm.bs),
        grid=(g,),
        in_specs=[pl.BlockSpec((1, r, q * cg), lambda i: (i, 0, 0)),
                  pl.BlockSpec((1, q * cg, nw), lambda i: (i, 0, 0)),
                  pl.BlockSpec((1, 4, S5_LANES, q * cg), lambda i: (i, 0, 0, 0)),
                  pl.BlockSpec((1, 2, S5_LANES), lambda i: (i, 0, 0)),
                  pl.BlockSpec((1, 1, q * cg), lambda i: (i, 0, 0)),
                  pl.BlockSpec((1, 2, dm.bs, S5_LANES), lambda i: (i, 0, 0, 0))],
        out_specs=[pl.BlockSpec((1, r, q * cg), lambda i: (i, 0, 0)),
                   pl.BlockSpec((1, 2, dm.bc, S5_LANES), lambda i: (i, 0, 0, 0))],
        out_shape=[jax.ShapeDtypeStruct((g, r, q * cg), BF16),
                   jax.ShapeDtypeStruct((g, 2, dm.bc, S5_LANES), F32)],
        scratch_shapes=[pltpu.VMEM((r, S5_LANES), F32)] * 6,
        compiler_params=_cp(("parallel",)),
        name="s5",
    )(u_g, w1, v, lam_pack, d_g, x0)

    def to_token_major(x, b, nc):
        return x.reshape(g, nc, b, q, cg).transpose(2, 1, 3, 0, 4).reshape(b * nc * q, g * cg)

    y_tok = jnp.concatenate([to_token_major(y_g[:, :rc], dm.bc, ncc),
                             to_token_major(y_g[:, rc:], dm.bs, ncs)], axis=0)
    fin = fin.transpose(2, 1, 0, 3)
    fin_re = jnp.stack([fin[:, 0, :, :p], fin[:, 0, :, p:]], axis=1)
    fin_im = jnp.stack([fin[:, 1, :, :p], fin[:, 1, :, p:]], axis=1)
    return y_tok, fin_re, fin_im


def _glu_kernel(y_ref, w_ref, o_ref):
    z = _gelu_tanh(y_ref[...].astype(F32))
    o_ref[...] = (z * _sigmoid(_dot(z.astype(BF16), w_ref[...]))).astype(o_ref.dtype)


def _glu(dm, y, w):
    t, n = y.shape
    tm = dm.tm
    return pl.pallas_call(
        _glu_kernel,
        grid=(t // tm,),
        in_specs=[pl.BlockSpec((tm, n), lambda i: (i, 0)),
                  pl.BlockSpec((n, n), lambda i: (0, 0))],
        out_specs=pl.BlockSpec((tm, n), lambda i: (i, 0)),
        out_shape=jax.ShapeDtypeStruct((t, n), BF16),
        compiler_params=_cp(("parallel",)),
        name="s5_glu",
    )(y, w)


def _rot(x, ct, st):
    return x * ct + pltpu.roll(x, 64, 1) * st


def _mla_q_kernel(qscale, cq_ref, ckv_ref, kr_ref, qn_ref, kvn_ref, ct_ref, st_ref, w_ref,
                  q_ref, ckvn_ref, kro_ref):
    ct = ct_ref[...]
    st = st_ref[...]
    cq = cq_ref[...]
    cqn = (cq * lax.rsqrt(jnp.mean(cq * cq, axis=-1, keepdims=True) + EPS) * qn_ref[...]).astype(BF16)
    for h in range(MLA_HEADS):
        qh = _dot(cqn, w_ref[:, h * QK_PAD:(h + 1) * QK_PAD])
        q_ref[:, h * QK_PAD:h * QK_PAD + QK_NOPE] = (qh[:, :QK_NOPE] * qscale).astype(BF16)
        q_ref[:, h * QK_PAD + QK_NOPE:(h + 1) * QK_PAD] = (_rot(qh[:, QK_NOPE:], ct, st) * qscale).astype(BF16)
    ckv = ckv_ref[...]
    ckvn_ref[...] = ckv * lax.rsqrt(jnp.mean(ckv * ckv, axis=-1, keepdims=True) + EPS) * kvn_ref[...]
    kro_ref[...] = _rot(kr_ref[...], ct, st)


def _mla_q(dm, a, q_norm, kv_norm, ct, st, w_uq_ext, q_lora, kv_lora):
    t = a.shape[0]
    nq = w_uq_ext.shape[1]
    kr_blk = (q_lora + kv_lora) // 128
    tm = min(dm.tm, 512)
    qscale = float(QK_NOPE + QK_ROPE) ** -0.5 * math.log2(math.e)
    return pl.pallas_call(
        functools.partial(_mla_q_kernel, qscale),
        grid=(t // tm,),
        in_specs=[pl.BlockSpec((tm, q_lora), lambda i: (i, 0)),
                  pl.BlockSpec((tm, kv_lora), lambda i: (i, q_lora // kv_lora)),
                  pl.BlockSpec((tm, 128), lambda i: (i, kr_blk)),
                  pl.BlockSpec((1, q_lora), lambda i: (0, 0)),
                  pl.BlockSpec((1, kv_lora), lambda i: (0, 0)),
                  pl.BlockSpec((tm, 128), lambda i: (i, 0)),
                  pl.BlockSpec((tm, 128), lambda i: (i, 0)),
                  pl.BlockSpec((q_lora, nq), lambda i: (0, 0))],
        out_specs=[pl.BlockSpec((tm, nq), lambda i: (i, 0)),
                   pl.BlockSpec((tm, kv_lora), lambda i: (i, 0)),
                   pl.BlockSpec((tm, 128), lambda i: (i, 0))],
        out_shape=[jax.ShapeDtypeStruct((t, nq), BF16),
                   jax.ShapeDtypeStruct((t, kv_lora), F32),
                   jax.ShapeDtypeStruct((t, 128), F32)],
        compiler_params=_cp(("parallel",)),
        name="mla_q",
    )(a, a, a, q_norm, kv_norm, ct, st, w_uq_ext)


def _kv_expand_kernel(c_ref, kr_ref, wk_ref, wv_ref, k_ref, v_ref):
    c = c_ref[...].astype(BF16)
    kn = _dot(c, wk_ref[...]).astype(BF16)
    vv = _dot(c, wv_ref[...]).astype(BF16)
    kr = kr_ref[...].astype(BF16)
    ones = jnp.ones((c.shape[0], V_PAD - V_HEAD), BF16)
    for h in range(MLA_HEADS):
        k_ref[:, h * QK_PAD:h * QK_PAD + QK_NOPE] = kn[:, h * QK_NOPE:(h + 1) * QK_NOPE]
        k_ref[:, h * QK_PAD + QK_NOPE:(h + 1) * QK_PAD] = kr
        v_ref[:, h * V_PAD:h * V_PAD + V_HEAD] = vv[:, h * V_HEAD:(h + 1) * V_HEAD]
        v_ref[:, h * V_PAD + V_HEAD:(h + 1) * V_PAD] = ones


def _kv_expand(ckv_keys, kr_keys, w_uk, w_uv):
    nk, kv_lora = ckv_keys.shape
    tm = 512
    return pl.pallas_call(
        _kv_expand_kernel,
        grid=(nk // tm,),
        in_specs=[pl.BlockSpec((tm, kv_lora), lambda i: (i, 0)),
                  pl.BlockSpec((tm, 128), lambda i: (i, 0)),
                  pl.BlockSpec(w_uk.shape, lambda i: (0, 0)),
                  pl.BlockSpec(w_uv.shape, lambda i: (0, 0))],
        out_specs=[pl.BlockSpec((tm, MLA_HEADS * QK_PAD), lambda i: (i, 0)),
                   pl.BlockSpec((tm, MLA_HEADS * V_PAD), lambda i: (i, 0))],
        out_shape=[jax.ShapeDtypeStruct((nk, MLA_HEADS * QK_PAD), BF16),
                   jax.ShapeDtypeStruct((nk, MLA_HEADS * V_PAD), BF16)],
        compiler_params=_cp(("parallel",)),
        name="mla_kv_expand",
    )(ckv_keys, kr_keys, w_uk, w_uv)


def _attn_kernel(nh, nkb, q_ref, k_ref, v_ref, o_ref):
    kb = k_ref.shape[0] // nkb
    for h in range(nh):
        q = q_ref[:, h * QK_PAD:(h + 1) * QK_PAD]
        m, acc = None, None
        for j in range(nkb):
            s = _dot_nt(q, k_ref[j * kb:(j + 1) * kb, h * QK_PAD:(h + 1) * QK_PAD])
            mj = jnp.max(s, axis=-1, keepdims=True)
            m_new = mj if j == 0 else jnp.maximum(m, mj)
            pv = _dot(jnp.exp2(s - m_new).astype(BF16), v_ref[j * kb:(j + 1) * kb, h * V_PAD:(h + 1) * V_PAD])
            acc = pv if j == 0 else acc * jnp.exp2(m - m_new) + pv
            m = m_new
        o_ref[:, h * V_HEAD:(h + 1) * V_HEAD] = (acc[:, :V_HEAD] / acc[:, V_HEAD:V_HEAD + 1]).astype(o_ref.dtype)


def _attention(q, k, v, nb, lq, lk, tq, nh, q_row0, k_row0, name):
    nqt = lq // tq
    qb0, kb0 = q_row0 // tq, k_row0 // lk
    nkb = -(-lk // KB_MAX)
    assert lk % nkb == 0
    return pl.pallas_call(
        functools.partial(_attn_kernel, nh, nkb),
        grid=(nb, MLA_HEADS // nh, nqt),
        in_specs=[pl.BlockSpec((tq, nh * QK_PAD), lambda b, h, i: (qb0 + b * nqt + i, h)),
                  pl.BlockSpec((lk, nh * QK_PAD), lambda b, h, i: (kb0 + b, h)),
                  pl.BlockSpec((lk, nh * V_PAD), lambda b, h, i: (kb0 + b, h))],
        out_specs=pl.BlockSpec((tq, nh * V_HEAD), lambda b, h, i: (b * nqt + i, h)),
        out_shape=jax.ShapeDtypeStruct((nb * lq, MLA_HEADS * V_HEAD), BF16),
        compiler_params=_cp(("parallel", "parallel", "arbitrary")),
        name=name,
    )(q, k, v)


def _route_kernel(n_exp, x_ref, g_ref, sh_ref, sc_ref, rw_ref, rb_ref, tri_ref,
                  hp_ref, idx_ref, rank_ref, wt_ref, cnt_ref, run_ref):
    @pl.when(pl.program_id(0) == 0)
    def _():
        run_ref[...] = jnp.zeros_like(run_ref)

    h = _norm_mod(x_ref[...], g_ref[0], sc_ref[0], sh_ref[0])
    _store_row_tiles(hp_ref, _pack_halves(h))
    scores =_sigmoid(_dot_nt(rw_ref[...], h.astype(BF16)))
    biased = scores + rb_ref[...]
    per = n_exp // N_EXPERT_GROUPS
    assert per == 4
    rows_b = [biased[e:e + 1, :] for e in range(n_exp)]
    rows_s = [scores[e:e + 1, :] for e in range(n_exp)]
    best_sum, best_g = None, None
    for gi in range(N_EXPERT_GROUPS):
        a, b, c, d = rows_b[per * gi:per * gi + per]
        hi1, lo1 = jnp.maximum(a, b), jnp.minimum(a, b)
        hi2, lo2 = jnp.maximum(c, d), jnp.minimum(c, d)
        top2 = jnp.maximum(hi1, hi2) + jnp.maximum(jnp.minimum(hi1, hi2), jnp.maximum(lo1, lo2))
        if gi == 0:
            best_sum, best_g = top2, jnp.zeros_like(top2, dtype=jnp.int32)
        else:
            upd = top2 > best_sum
            best_sum = jnp.where(upd, top2, best_sum)
            best_g = jnp.where(upd, gi, best_g)
    vb, vs = [], []
    for k in range(per):
        accb, accs = rows_b[k], rows_s[k]
        for gi in range(1, N_EXPERT_GROUPS):
            sel = best_g == gi
            accb = jnp.where(sel, rows_b[per * gi + k], accb)
            accs = jnp.where(sel, rows_s[per * gi + k], accs)
        vb.append(accb)
        vs.append(accs)

    def first_argmax(vals, exclude):
        bv, bi, bs = None, None, None
        for k in range(per):
            v = vals[k] if exclude is None else jnp.where(exclude == k, -jnp.inf, vals[k])
            if k == 0:
                bv, bi, bs = v, jnp.zeros_like(best_g), vs[0]
            else:
                upd = v > bv
                bv = jnp.where(upd, v, bv)
                bi = jnp.where(upd, k, bi)
                bs = jnp.where(upd, vs[k], bs)
        return bi, bs

    i1, s1 = first_argmax(vb, None)
    i2, s2 = first_argmax(vb, i1)
    tot = s1 + s2
    e1 = best_g * per + i1
    e2 = best_g * per + i2
    idx_ref[0:1, :] = e1
    idx_ref[1:2, :] = e2
    wt_ref[0:1, :] = s1 / tot
    wt_ref[1:2, :] = s2 / tot
    eids = lax.broadcasted_iota(jnp.int32, (n_exp, e1.shape[1]), 0)
    hit1 = eids == e1
    hit2 = eids == e2
    oh1 = jnp.where(hit1, 1.0, 0.0)
    oh2 = jnp.where(hit2, 1.0, 0.0)
    p1 = _dot(oh1.astype(BF16), tri_ref[...])
    p2 = _dot(oh2.astype(BF16), tri_ref[...])
    c1 = jnp.sum(oh1, axis=1, keepdims=True)
    c2 = jnp.sum(oh2, axis=1, keepdims=True)
    run = run_ref[...]
    rank_ref[0:1, :] = jnp.sum(jnp.where(hit1, run + p1, 0.0), axis=0, keepdims=True).astype(jnp.int32)
    rank_ref[1:2, :] = jnp.sum(jnp.where(hit2, run + c1 + p2, 0.0), axis=0, keepdims=True).astype(jnp.int32)
    run_ref[...] = run + c1 + c2
    cnt_ref[...] = run + c1 + c2


def _route(dm, x, gain, mods, layer, rw_t, rb):
    t, d = x.shape
    n_exp = rw_t.shape[0]
    tm = dm.tm
    rt = d // 2 // LANES
    tri = (jnp.arange(tm)[:, None] < jnp.arange(tm)[None, :]).astype(BF16)
    return pl.pallas_call(
        functools.partial(_route_kernel, n_exp),
        grid=(t // tm,),
        in_specs=[pl.BlockSpec((tm, d), lambda i: (i, 0)),
                  pl.BlockSpec((1, 1, d), lambda i: (layer, 0, 0)),
                  pl.BlockSpec((1, 1, d), _mod_idx(dm, layer, 3, tm)),
                  pl.BlockSpec((1, 1, d), _mod_idx(dm, layer, 4, tm)),
                  pl.BlockSpec((n_exp, d), lambda i: (0, 0)),
                  pl.BlockSpec((n_exp, 1), lambda i: (0, 0)),
                  pl.BlockSpec((tm, tm), lambda i: (0, 0))],
        out_specs=[pl.BlockSpec((tm * rt, LANES), lambda i: (i, 0)),
                   pl.BlockSpec((2, tm), lambda i: (0, i)),
                   pl.BlockSpec((2, tm), lambda i: (0, i)),
                   pl.BlockSpec((2, tm), lambda i: (0, i)),
                   pl.BlockSpec((n_exp, 1), lambda i: (0, 0))],
        out_shape=[jax.ShapeDtypeStruct((t * rt, LANES), U32),
                   jax.ShapeDtypeStruct((2, t), jnp.int32),
                   jax.ShapeDtypeStruct((2, t), jnp.int32),
                   jax.ShapeDtypeStruct((2, t), F32),
                   jax.ShapeDtypeStruct((n_exp, 1), F32)],
        scratch_shapes=[pltpu.VMEM((n_exp, 1), F32)],
        compiler_params=_cp(("arbitrary",)),
        name="moe_route",
    )(x, gain, mods, mods, rw_t, rb, tri)


def _slot_plan(idx, rank, counts, n_exp, n_slots):
    counts = counts.reshape(n_exp).astype(jnp.int32)
    padded = ((counts + TM_MOE - 1) // TM_MOE) * TM_MOE
    ends = jnp.cumsum(padded)
    starts = ends - padded
    start_of = jnp.sum(jnp.where(idx[..., None] == jnp.arange(n_exp, dtype=jnp.int32), starts, 0), axis=-1)
    dest = (start_of + rank).reshape(-1).astype(jnp.int32)
    tile_start = jnp.arange(n_slots // TM_MOE, dtype=jnp.int32) * TM_MOE
    tile_exp = jnp.minimum(jnp.sum((ends[None, :] <= tile_start[:, None]).astype(jnp.int32), axis=1), n_exp - 1)
    tile_ok = (tile_start < ends[-1]).astype(jnp.int32)
    pad_lo = jnp.concatenate([starts + counts, ends[-1:]]).astype(jnp.int32)
    pad_n = (padded - counts).astype(jnp.int32)
    return dest, tile_exp.astype(jnp.int32), tile_ok, pad_lo, pad_n


def _scatter_kernel(n_exp, t_total, rows, rt, dest_ref, padlo_ref, padn_ref, hp_ref, xs_ref, zrow, sem):
    i = pl.program_id(0)
    base = i * rows
    tile_rows = TM_MOE * rt

    def slot(s):
        return xs_ref.at[pl.ds(pl.multiple_of(s * rt, rt), rt)]

    @pl.when(i == 0)
    def _():
        zrow[...] = jnp.zeros_like(zrow)
        for e in range(n_exp):
            lo = padlo_ref[e]

            def pad_copy(r, lo=lo):
                return pltpu.make_async_copy(zrow.at[pl.ds(0, rt)], slot(lo + r), sem)

            def start(r, c, pad_copy=pad_copy):
                pad_copy(r).start()
                return c

            def wait(r, c, pad_copy=pad_copy):
                pad_copy(r).wait()
                return c

            lax.fori_loop(0, padn_ref[e], start, 0)
            lax.fori_loop(0, padn_ref[e], wait, 0)

        tail0 = padlo_ref[n_exp]

        def tail_copy(k):
            row = pl.multiple_of((tail0 + k * TM_MOE) * rt, tile_rows)
            return pltpu.make_async_copy(zrow, xs_ref.at[pl.ds(row, tile_rows)], sem)

        def tail_start(k, c):
            tail_copy(k).start()
            return c

        def tail_wait(k, c):
            tail_copy(k).wait()
            return c

        n_tail = (xs_ref.shape[0] // rt - tail0) // TM_MOE
        lax.fori_loop(0, n_tail, tail_start, 0)
        lax.fori_loop(0, n_tail, tail_wait, 0)

    def copies(r):
        src = hp_ref.at[pl.ds(pl.multiple_of(r * rt, rt), rt)]
        return (pltpu.make_async_copy(src, slot(dest_ref[base + r]), sem),
                pltpu.make_async_copy(src, slot(dest_ref[t_total + base + r]), sem))

    def start(gi, c):
        for u in range(DMA_UNROLL):
            for prio, cp in enumerate(copies(gi * DMA_UNROLL + u)):
                cp.start(priority=prio)
        return c

    def wait(gi, c):
        for u in range(DMA_UNROLL):
            for cp in copies(gi * DMA_UNROLL + u):
                cp.wait()
        return c

    lax.fori_loop(0, rows // DMA_UNROLL, start, 0)
    lax.fori_loop(0, rows // DMA_UNROLL, wait, 0)


def _scatter_rows(hp, dest, pad_lo, pad_n, n_slots, rows, rt):
    t = hp.shape[0] // rt
    n_exp = pad_n.shape[0]
    return pl.pallas_call(
        functools.partial(_scatter_kernel, n_exp, t, rows, rt),
        grid_spec=pltpu.PrefetchScalarGridSpec(
            num_scalar_prefetch=3,
            grid=(t // rows,),
            in_specs=[pl.BlockSpec((rows * rt, LANES), lambda i, *_: (i, 0))],
            out_specs=pl.BlockSpec(memory_space=pl.ANY),
            scratch_shapes=[pltpu.VMEM((TM_MOE * rt, LANES), U32), pltpu.SemaphoreType.DMA(())]),
        out_shape=jax.ShapeDtypeStruct((n_slots * rt, LANES), U32),
        compiler_params=_cp(("arbitrary",)),
        name="moe_scatter",
    )(dest, pad_lo, pad_n, hp)


def _expert_kernel(rt, te_ref, ok_ref, x_ref, wg_ref, wu_ref, wd_ref, o_ref, wg_b, wu_b, wd_b):
    i = pl.program_id(0)
    new_expert = jnp.logical_or(i == 0, te_ref[i] != te_ref[jnp.maximum(i - 1, 0)])

    @pl.when(new_expert)
    def _():
        wg_b[...] = wg_ref[0, 0].astype(BF16)
        wu_b[...] = wu_ref[0, 0].astype(BF16)
        wd_b[...] = wd_ref[0, 0].astype(BF16)

    @pl.when(ok_ref[i] == 1)
    def _():
        x_lo, x_hi = _unpack_halves(_load_row_tiles(x_ref, rt))
        x_lo, x_hi = x_lo.astype(BF16), x_hi.astype(BF16)
        half = x_lo.shape[1]
        h1 = _dot(x_lo, wg_b[:half, :]) + _dot(x_hi, wg_b[half:, :])
        h2 = _dot(x_lo, wu_b[:half, :]) + _dot(x_hi, wu_b[half:, :])
        act = (_silu(h1) * h2).astype(BF16)
        _store_row_tiles(o_ref, _pack_halves(_dot(act, wd_b[...])))

    @pl.when(ok_ref[i] == 0)
    def _():
        o_ref[...] = jnp.zeros_like(o_ref)


def _experts(x_sorted, tile_exp, tile_ok, w_gate, w_up, w_down, layer, rt):
    n_slots = x_sorted.shape[0] // rt
    d, f = w_gate.shape[-2:]
    return pl.pallas_call(
        functools.partial(_expert_kernel, rt),
        grid_spec=pltpu.PrefetchScalarGridSpec(
            num_scalar_prefetch=2,
            grid=(n_slots // TM_MOE,),
            in_specs=[pl.BlockSpec((TM_MOE * rt, LANES), lambda i, te, ok: (i, 0)),
                      pl.BlockSpec((1, 1, d, f), lambda i, te, ok: (layer, te[i], 0, 0)),
                      pl.BlockSpec((1, 1, d, f), lambda i, te, ok: (layer, te[i], 0, 0)),
                      pl.BlockSpec((1, 1, f, d), lambda i, te, ok: (layer, te[i], 0, 0))],
            out_specs=pl.BlockSpec((TM_MOE * rt, LANES), lambda i, te, ok: (i, 0)),
            scratch_shapes=[pltpu.VMEM((d, f), BF16), pltpu.VMEM((d, f), BF16), pltpu.VMEM((f, d), BF16)]),
        out_shape=jax.ShapeDtypeStruct((n_slots * rt, LANES), U32),
        compiler_params=_cp(("arbitrary",)),
        name="moe_experts",
    )(tile_exp, tile_ok, x_sorted, w_gate, w_up, w_down)


def _combine_kernel(t_total, rows, rt, dest_ref, x_ref, gate_ref, wt_ref, y_ref, o_ref, y0, y1, sem):
    base = pl.program_id(0) * rows

    def slot(s):
        return y_ref.at[pl.ds(pl.multiple_of(s * rt, rt), rt)]

    def copies(r):
        dst = pl.ds(pl.multiple_of(r * rt, rt), rt)
        return (pltpu.make_async_copy(slot(dest_ref[base + r]), y0.at[dst], sem),
                pltpu.make_async_copy(slot(dest_ref[t_total + base + r]), y1.at[dst], sem))

    def start(gi, c):
        for u in range(DMA_UNROLL):
            for prio, cp in enumerate(copies(gi * DMA_UNROLL + u)):
                cp.start(priority=prio)
        return c

    def wait(gi, c):
        for u in range(DMA_UNROLL):
            for cp in copies(gi * DMA_UNROLL + u):
                cp.wait()
        return c

    lax.fori_loop(0, rows // DMA_UNROLL, start, 0)
    lax.fori_loop(0, rows // DMA_UNROLL, wait, 0)
    w0 = wt_ref[:, 0:1]
    w1 = wt_ref[:, 1:2]
    a_lo, a_hi = _unpack_halves(_load_row_tiles(y0, rt))
    b_lo, b_hi = _unpack_halves(_load_row_tiles(y1, rt))
    half = a_lo.shape[1]
    gate = gate_ref[0]
    o_ref[:, :half] = x_ref[:, :half] + gate[:, :half] * (w0 * a_lo + w1 * b_lo)
    o_ref[:, half:] = x_ref[:, half:] + gate[:, half:] * (w0 * a_hi + w1 * b_hi)


def _combine(dm, x, mods, layer, y_sorted, dest, wts_t):
    t, d = x.shape
    rows = min(dm.tm, 512)
    rt = d // 2 // LANES
    return pl.pallas_call(
        functools.partial(_combine_kernel, t, rows, rt),
        grid_spec=pltpu.PrefetchScalarGridSpec(
            num_scalar_prefetch=1,
            grid=(t // rows,),
            in_specs=[pl.BlockSpec((rows, d), lambda i, dst: (i, 0)),
                      pl.BlockSpec((1, 1, d), lambda i, dst: _mod_idx(dm, layer, 5, rows)(i)),
                      pl.BlockSpec((rows, 2), lambda i, dst: (i, 0)),
                      pl.BlockSpec(memory_space=pl.ANY)],
            out_specs=pl.BlockSpec((rows, d), lambda i, dst: (i, 0)),
            scratch_shapes=[pltpu.VMEM((rows * rt, LANES), U32), pltpu.VMEM((rows * rt, LANES), U32),
                            pltpu.SemaphoreType.DMA(())]),
        out_shape=jax.ShapeDtypeStruct((t, d), F32),
        compiler_params=_cp(("arbitrary",)),
        name="moe_combine",
    )(dest, x, mods, wts_t, y_sorted)


def _moe(dm, x, gain, mods, layer, rw_t, rb, w_gate, w_up, w_down):
    n_exp = rw_t.shape[0]
    hp, idx, rank, wts, counts = _route(dm, x, gain, mods, layer, rw_t, rb)
    n_slots = 2 * dm.t + n_exp * TM_MOE
    dest, tile_exp, tile_ok, pad_lo, pad_n = _slot_plan(idx, rank, counts, n_exp, n_slots)
    rt = dm.d // 2 // LANES
    x_sorted = _scatter_rows(hp, dest, pad_lo, pad_n, n_slots, min(dm.tm, 512), rt)
    y_sorted = _experts(x_sorted, tile_exp, tile_ok, w_gate, w_up, w_down, layer, rt)
    return _combine(dm, x, mods, layer, y_sorted, dest, wts.T)


def _final_norm_kernel(x_ref, g_ref, o_ref):
    x = x_ref[...]
    o_ref[...] = x * lax.rsqrt(jnp.mean(x * x, axis=-1, keepdims=True) + EPS) * g_ref[...]


def _final_norm(dm, x, gain):
    t, d = x.shape
    tm = dm.tm
    return pl.pallas_call(
        _final_norm_kernel,
        grid=(t // tm,),
        in_specs=[pl.BlockSpec((tm, d), lambda i: (i, 0)),
                  pl.BlockSpec((1, d), lambda i: (0, 0))],
        out_specs=pl.BlockSpec((tm, d), lambda i: (i, 0)),
        out_shape=jax.ShapeDtypeStruct((t, d), F32),
        compiler_params=_cp(("parallel",)),
        name="final_norm",
    )(x, gain.reshape(1, d))


def _rope_tables(dm):
    rows = dm.ls // GRID_W
    row = jnp.repeat(jnp.arange(rows, dtype=F32), GRID_W)
    col = jnp.tile(jnp.arange(GRID_W, dtype=F32), rows)
    half = QK_ROPE // 2
    freqs = jnp.power(ROPE_BASE, -jnp.arange(0, half, 2, dtype=F32) / half)
    ar, ac = row[:, None] * freqs, col[:, None] * freqs
    zeros = jnp.zeros((dm.ls, 128 - QK_ROPE), F32)
    ct = jnp.concatenate([jnp.cos(ar), jnp.cos(ar), jnp.cos(ac), jnp.cos(ac), zeros], axis=-1)
    st = jnp.concatenate([-jnp.sin(ar), jnp.sin(ar), -jnp.sin(ac), jnp.sin(ac), zeros], axis=-1)
    ct_c = jnp.concatenate([jnp.ones((dm.tc, QK_ROPE), F32), jnp.zeros((dm.tc, 128 - QK_ROPE), F32)], axis=-1)
    ct = jnp.concatenate([ct_c, jnp.tile(ct, (dm.bs, 1))], axis=0)
    st = jnp.concatenate([jnp.zeros((dm.tc, 128), F32), jnp.tile(st, (dm.bs, 1))], axis=0)
    return ct, st


def _swap_halves_cols(w):
    qt = QK_ROPE // 4
    return jnp.concatenate([w[..., qt:2 * qt], w[..., :qt], w[..., 3 * qt:], w[..., 2 * qt:3 * qt]], axis=-1)


def kernel(x_prompt, x_sample, c, state_ret, state_s5_re, state_s5_im, cache_ckv, cache_krope, c_ctx, ada_w, ada_b, norm_mix, norm_ffn, norm_final, even_w_in, even_w_out, ret_decay, s5_a_re, s5_a_im, s5_log_dt, s5_b_re, s5_b_im, s5_c_re, s5_c_im, s5_d, s5_w_glu, mla_w_in, mla_q_norm, mla_w_uq, mla_kv_norm, mla_w_ukv, mla_w_out, router_w, router_bias, moe_w_gate, moe_w_up, moe_w_down):
    dm = _Dims(x_prompt, x_sample)
    d = dm.d
    depth = ada_w.shape[0]
    n_exp = router_w.shape[1]
    past = cache_ckv.shape[2]
    q_lora = mla_q_norm.shape[1]
    kv_lora = mla_kv_norm.shape[1]

    x = jnp.concatenate([x_prompt.reshape(dm.tc, d), x_sample.reshape(dm.ts, d)], axis=0)
    cond = jnp.zeros((dm.rp, d), F32).at[0].set(c_ctx).at[1:1 + dm.bs].set(c)
    mods = _adaln(cond, ada_w, ada_b, tn=d * N_MOD // 8).reshape(depth * dm.rp * N_MOD, 1, d)
    g_mix = norm_mix.reshape(depth, 1, d)
    g_ffn = norm_ffn.reshape(depth, 1, d)
    rw_t = router_w.T.astype(BF16)
    rb = router_bias.astype(F32).reshape(n_exp, 1)
    ct, st = _rope_tables(dm)

    rets, s5r, s5i, ckvs, krs = [], [], [], [], []
    for layer in range(depth):
        if layer % 2 == 0:
            i = layer // 2
            ret_w = even_w_out.shape[1] - s5_d.shape[1]
            z = _norm_mod_matmul(dm, x, g_mix, mods, layer, 0, even_w_in[i].astype(BF16), BF16,
                                 tn=even_w_in.shape[2] // 5, name="even_in_proj")
            log_gamma = -jnp.exp(ret_decay[i].astype(F32))
            ro_c, sfin = _retention(z, ret_w // RET_HEADS, log_gamma, None, dm.bc, dm.lc, 0, "retention_ctx")
            ro_s, _ = _retention(z, ret_w // RET_HEADS, log_gamma, state_ret[:, i].astype(F32), dm.bs, dm.ls,
                                 dm.tc // dm.ls, "retention_lat")
            ret_out = jnp.concatenate([ro_c, ro_s], axis=0)
            ops = _s5_operators(s5_a_re[i], s5_a_im[i], s5_log_dt[i], s5_b_re[i], s5_b_im[i],
                                s5_c_re[i], s5_c_im[i])
            y, f_re, f_im = _s5(dm, z[:, 4 * ret_w:], ops, s5_d[i], state_s5_re[:, i], state_s5_im[:, i])
            s5_out = _glu(dm, y, s5_w_glu[i].astype(BF16))
            w_out = even_w_out[i].astype(BF16)
            x = _matmul_residual(dm, x, mods, layer, 2, [ret_out, s5_out], [w_out[:ret_w], w_out[ret_w:]],
                                 tn=d // 2, name="even_out_proj")
            rets.append(sfin)
            s5r.append(f_re)
            s5i.append(f_im)
        else:
            j = layer // 2
            w_in = mla_w_in[j]
            w_in_ext = jnp.concatenate([w_in, _swap_halves_cols(w_in[:, q_lora + kv_lora:])], axis=1).astype(BF16)
            a = _norm_mod_matmul(dm, x, g_mix, mods, layer, 0, w_in_ext, F32, tn=w_in_ext.shape[1],
                                 name="mla_in_proj")
            w_uq = mla_w_uq[j].reshape(q_lora, MLA_HEADS, QK_NOPE + QK_ROPE)
            w_uq_ext = jnp.concatenate([w_uq, _swap_halves_cols(w_uq[..., QK_NOPE:])], axis=-1)
            w_uq_ext = w_uq_ext.reshape(q_lora, MLA_HEADS * QK_PAD).astype(BF16)
            q, ckv_n, kr = _mla_q(dm, a, mla_q_norm[j].reshape(1, q_lora), mla_kv_norm[j].reshape(1, kv_lora),
                                  ct, st, w_uq_ext, q_lora, kv_lora)
            ckv_lat = jnp.concatenate([cache_ckv[:, j].astype(F32), ckv_n[dm.tc:].reshape(dm.bs, dm.ls, kv_lora)], axis=1)
            kr_cache = jnp.concatenate([cache_krope[:, j].astype(F32),
                                        jnp.zeros((dm.bs, past, 128 - QK_ROPE), F32)], axis=-1)
            kr_lat = jnp.concatenate([kr_cache, kr[dm.tc:].reshape(dm.bs, dm.ls, 128)], axis=1)
            lk = past + dm.ls
            ckv_keys = jnp.concatenate([ckv_lat.reshape(dm.bs * lk, kv_lora), ckv_n[:dm.tc]], axis=0)
            kr_keys = jnp.concatenate([kr_lat.reshape(dm.bs * lk, 128), kr[:dm.tc]], axis=0)
            w_ukv = mla_w_ukv[j].reshape(kv_lora, MLA_HEADS, QK_NOPE + V_HEAD)
            w_uk = w_ukv[..., :QK_NOPE].reshape(kv_lora, MLA_HEADS * QK_NOPE).astype(BF16)
            w_uv = w_ukv[..., QK_NOPE:].reshape(kv_lora, MLA_HEADS * V_HEAD).astype(BF16)
            k_all, v_all = _kv_expand(ckv_keys, kr_keys, w_uk, w_uv)
            o_c = _attention(q, k_all, v_all, dm.bc, dm.lc, dm.lc, dm.lc, MLA_HEADS, 0, dm.bs * lk, "attn_ctx")
            o_s = _attention(q, k_all, v_all, dm.bs, dm.ls, lk, min(TQ_MAX, dm.ls), LAT_HEADS_PER_STEP, dm.tc, 0,
                             "attn_lat")
            o = jnp.concatenate([o_c, o_s], axis=0)
            x = _matmul_residual(dm, x, mods, layer, 2, [o], [mla_w_out[j].astype(BF16)], tn=d // 2,
                                 name="mla_out_proj")
            ckvs.append(ckv_n[:dm.tc].reshape(dm.bc, dm.lc, kv_lora))
            krs.append(kr[:dm.tc, :QK_ROPE].reshape(dm.bc, dm.lc, QK_ROPE))
        x = _moe(dm, x, g_ffn, mods, layer, rw_t, rb, moe_w_gate, moe_w_up, moe_w_down)

    y = _final_norm(dm, x, norm_final)
    y_prompt = y[:dm.tc].reshape(dm.bc, dm.lc, d)
    y_sample = y[dm.tc:].reshape(dm.bs, dm.ls, d)
    return (y_prompt, y_sample, jnp.stack(rets, axis=1), jnp.stack(s5r, axis=1), jnp.stack(s5i, axis=1),
            jnp.stack(ckvs, axis=1), jnp.stack(krs, axis=1))
```

```python
import functools
import math

import jax
import jax.numpy as jnp
import numpy as np
from jax import lax
from jax.experimental import pallas as pl
from jax.experimental.pallas import tpu as pltpu

F32 = jnp.float32
BF16 = jnp.bfloat16
U32 = jnp.uint32
EPS = 1e-6

RET_HEADS = 4
RET_CHUNK = 128
RET_UNROLL = 4
S5_GROUP = 16
S5_Q = 16
S5_LANES = 128
LANES = 128
MLA_HEADS = 16
QK_NOPE = 128
QK_ROPE = 64
V_HEAD = 128
QK_PAD = 256
V_PAD = 256
GRID_W = 64
ROPE_BASE = 10000.0
N_EXPERT_GROUPS = 4
N_MOD = 6

VMEM_LIMIT = 56 * 1024 * 1024
TM_MAX = 1024
TQ_MAX = 512
KB_MAX = 768
LAT_HEADS_PER_STEP = 2
TM_MOE = 256
DMA_UNROLL = 8
HI_MASK = np.uint32(0xFFFF0000)


def _cp(sem, vmem=VMEM_LIMIT):
    return pltpu.CompilerParams(dimension_semantics=sem, vmem_limit_bytes=vmem)


def _sigmoid(x):
    return 1.0 / (1.0 + jnp.exp(-x))


def _silu(x):
    return x * _sigmoid(x)


def _gelu_tanh(x):
    return 0.5 * x * (1.0 + jnp.tanh(math.sqrt(2.0 / math.pi) * (x + 0.044715 * (x * x * x))))


def _dot(a, b):
    return jnp.dot(a, b, preferred_element_type=F32)


def _dot_nt(a, b):
    return lax.dot_general(a, b, (((1,), (1,)), ((), ())), preferred_element_type=F32)


def _dot_tn(a, b):
    return lax.dot_general(a, b, (((0,), (0,)), ((), ())), preferred_element_type=F32)


def _pack_halves(x):
    half = x.shape[1] // 2
    xb = x.astype(BF16).astype(F32)
    lo = lax.bitcast_convert_type(xb[:, :half], U32) >> 16
    hi = lax.bitcast_convert_type(xb[:, half:], U32) & HI_MASK
    return hi | lo


def _unpack_halves(w):
    return (lax.bitcast_convert_type(w << 16, F32), lax.bitcast_convert_type(w & HI_MASK, F32))


def _store_row_tiles(ref, w):
    r, rt = w.shape[0], w.shape[1] // LANES
    for s in range(rt):
        ref[pl.ds(s, r, stride=rt), :] = w[:, s * LANES:(s + 1) * LANES]


def _load_row_tiles(ref, rt):
    r = ref.shape[0] // rt
    return jnp.concatenate([ref[pl.ds(s, r, stride=rt), :] for s in range(rt)], axis=1)


class _Dims:
    def __init__(self, x_prompt, x_sample):
        self.bc, self.lc, self.d = x_prompt.shape
        self.bs, self.ls, _ = x_sample.shape
        self.tc = self.bc * self.lc
        self.ts = self.bs * self.ls
        self.t = self.tc + self.ts
        self.rp = -(-(1 + self.bs) // 8) * 8
        self.tm = min(TM_MAX, self.ls)
        assert self.tc % self.tm == 0 and self.ls % self.tm == 0

    def mod_row(self, i, tm):
        nct = self.tc // tm
        return jnp.where(i < nct, 0, 1 + (i - nct) // (self.ls // tm))


def _mod_idx(dm, layer, k, tm):
    def idx(i, *_):
        return ((layer * dm.rp + dm.mod_row(i, tm)) * N_MOD + k, 0, 0)
    return idx


def _adaln_kernel(c_ref, w_ref, b_ref, o_ref):
    cs = _silu(c_ref[...]).astype(BF16)
    o_ref[0] = _dot(cs, w_ref[0].astype(BF16)) + b_ref[0]


def _adaln(cond, ada_w, ada_b, tn):
    depth, d, n = ada_w.shape
    rp = cond.shape[0]
    return pl.pallas_call(
        _adaln_kernel,
        grid=(depth, n // tn),
        in_specs=[pl.BlockSpec((rp, d), lambda l, j: (0, 0)),
                  pl.BlockSpec((1, d, tn), lambda l, j: (l, 0, j)),
                  pl.BlockSpec((1, 1, tn), lambda l, j: (l, 0, j))],
        out_specs=pl.BlockSpec((1, rp, tn), lambda l, j: (l, 0, j)),
        out_shape=jax.ShapeDtypeStruct((depth, rp, n), F32),
        compiler_params=_cp(("parallel", "parallel")),
        name="adaln",
    )(cond, ada_w, ada_b.reshape(depth, 1, n))


def _norm_mod(x, g, sc, sh):
    y = x * lax.rsqrt(jnp.mean(x * x, axis=-1, keepdims=True) + EPS) * g
    return y * (1.0 + sc) + sh


def _nmm_kernel(x_ref, g_ref, sh_ref, sc_ref, w_ref, o_ref, hn_ref):
    @pl.when(pl.program_id(1) == 0)
    def _():
        hn_ref[...] = _norm_mod(x_ref[...], g_ref[0], sc_ref[0], sh_ref[0]).astype(BF16)

    o_ref[...] = _dot(hn_ref[...], w_ref[...]).astype(o_ref.dtype)


def _norm_mod_matmul(dm, x, gain, mods, layer, k_shift, w, out_dtype, tn, name):
    t, d = x.shape
    n = w.shape[1]
    tm = dm.tm
    return pl.pallas_call(
        _nmm_kernel,
        grid=(t // tm, n // tn),
        in_specs=[pl.BlockSpec((tm, d), lambda i, j: (i, 0)),
                  pl.BlockSpec((1, 1, d), lambda i, j: (layer, 0, 0)),
                  pl.BlockSpec((1, 1, d), _mod_idx(dm, layer, k_shift, tm)),
                  pl.BlockSpec((1, 1, d), _mod_idx(dm, layer, k_shift + 1, tm)),
                  pl.BlockSpec((d, tn), lambda i, j: (0, j))],
        out_specs=pl.BlockSpec((tm, tn), lambda i, j: (i, j)),
        out_shape=jax.ShapeDtypeStruct((t, n), out_dtype),
        scratch_shapes=[pltpu.VMEM((tm, d), BF16)],
        compiler_params=_cp(("parallel", "arbitrary")),
        name=name,
    )(x, gain, mods, mods, w)


def _nmm_split_kernel(n_main, x_ref, g_ref, sh_ref, sc_ref, w_ref, o_main_ref, o_tail_ref, hn_ref):
    j = pl.program_id(1)

    @pl.when(j == 0)
    def _():
        hn_ref[...] = _norm_mod(x_ref[...], g_ref[0], sc_ref[0], sh_ref[0]).astype(BF16)

    acc = _dot(hn_ref[...], w_ref[...])

    @pl.when(j < n_main)
    def _():
        o_main_ref[...] = acc.astype(o_main_ref.dtype)

    @pl.when(j >= n_main)
    def _():
        o_tail_ref[...] = acc


def _norm_mod_matmul_split(dm, x, gain, mods, layer, k_shift, w, tn, name):
    t, d = x.shape
    n = w.shape[1]
    tm = dm.tm
    n_main = n // tn - 1
    return pl.pallas_call(
        functools.partial(_nmm_split_kernel, n_main),
        grid=(t // tm, n // tn),
        in_specs=[pl.BlockSpec((tm, d), lambda i, j: (i, 0)),
                  pl.BlockSpec((1, 1, d), lambda i, j: (layer, 0, 0)),
                  pl.BlockSpec((1, 1, d), _mod_idx(dm, layer, k_shift, tm)),
                  pl.BlockSpec((1, 1, d), _mod_idx(dm, layer, k_shift + 1, tm)),
                  pl.BlockSpec((d, tn), lambda i, j: (0, j))],
        out_specs=[pl.BlockSpec((tm, tn), lambda i, j: (i, jnp.minimum(j, n_main - 1))),
                   pl.BlockSpec((tm, tn), lambda i, j: (i, 0))],
        out_shape=[jax.ShapeDtypeStruct((t, n_main * tn), BF16),
                   jax.ShapeDtypeStruct((t, tn), F32)],
        scratch_shapes=[pltpu.VMEM((tm, d), BF16)],
        compiler_params=_cp(("parallel", "arbitrary")),
        name=name,
    )(x, gain, mods, mods, w)


def _mmres_kernel(n_in, x_ref, gate_ref, *refs):
    a_refs, w_refs, o_ref = refs[:n_in], refs[n_in:2 * n_in], refs[2 * n_in]
    acc = _dot(a_refs[0][...], w_refs[0][...])
    for a_ref, w_ref in zip(a_refs[1:], w_refs[1:]):
        acc += _dot(a_ref[...], w_ref[...])
    o_ref[...] = x_ref[...] + gate_ref[0] * acc


def _matmul_residual(dm, x, mods, layer, k_gate, acts, ws, tn, name):
    t, d = x.shape
    n_in = len(acts)
    tm = dm.tm

    def gate_idx(i, j):
        return ((layer * dm.rp + dm.mod_row(i, tm)) * N_MOD + k_gate, 0, j)

    in_specs = [pl.BlockSpec((tm, tn), lambda i, j: (i, j)),
                pl.BlockSpec((1, 1, tn), gate_idx)]
    in_specs += [pl.BlockSpec((tm, a.shape[1]), lambda i, j: (i, 0)) for a in acts]
    in_specs += [pl.BlockSpec((w.shape[0], tn), lambda i, j: (0, j)) for w in ws]
    return pl.pallas_call(
        functools.partial(_mmres_kernel, n_in),
        grid=(t // tm, d // tn),
        in_specs=in_specs,
        out_specs=pl.BlockSpec((tm, tn), lambda i, j: (i, j)),
        out_shape=jax.ShapeDtypeStruct((t, d), F32),
        compiler_params=_cp(("parallel", "parallel")),
        name=name,
    )(x, mods, *acts, *ws)


def _ret_kernel(has_s0, nch, scale, lg_ref, q_ref, k_ref, v_ref, g_ref, *refs):
    if has_s0:
        s0_ref, o_ref, sfin_ref, oacc_ref, st_ref = refs
    else:
        o_ref, sfin_ref, oacc_ref, st_ref = refs
    c = RET_CHUNK
    h = pl.program_id(1)
    lgf = lg_ref[0, h]
    lgb = lg_ref[1, h]
    dv = v_ref.shape[1]

    ii = lax.broadcasted_iota(jnp.int32, (c, c), 0).astype(F32)
    jj = lax.broadcasted_iota(jnp.int32, (c, c), 1).astype(F32)
    diff = ii - jj
    dtot = (jnp.where(diff >= 0, jnp.exp(lgf * jnp.maximum(diff, 0.0)), 0.0)
            + jnp.where(diff <= 0, jnp.exp(lgb * jnp.maximum(-diff, 0.0)), 0.0)) * scale
    pos = lax.broadcasted_iota(jnp.int32, (c, 1), 0).astype(F32)
    qdec_f = jnp.exp(lgf * (pos + 1.0)) * scale
    kdec_f = jnp.exp(lgf * (c - 1.0 - pos))
    qdec_b = jnp.exp(lgb * (c - pos)) * scale
    kdec_b = jnp.exp(lgb * pos)
    cdec_f = jnp.exp(jnp.full((1, dv), lgf * c, F32))
    cdec_b = jnp.exp(jnp.full((1, dv), lgb * c, F32))

    if has_s0:
        st_ref[...] = s0_ref[0, 0, 0]
    else:
        st_ref[...] = jnp.zeros_like(st_ref)

    def fwd(n, carry):
        r = pl.multiple_of(n * c, c)
        qn = q_ref[pl.ds(r, c), :]
        kn = k_ref[pl.ds(r, c), :]
        vn = v_ref[pl.ds(r, c), :]
        p = (_dot_nt(qn, kn) * dtot).astype(BF16)
        o = _dot(p, vn)
        o += _dot((qn.astype(F32) * qdec_f).astype(BF16), st_ref[...].astype(BF16))
        oacc_ref[pl.ds(r, c), :] = o
        kd = (kn.astype(F32) * kdec_f).astype(BF16)
        st_ref[...] = st_ref[...] * cdec_f + _dot_tn(kd, vn)
        return carry

    lax.fori_loop(0, nch, fwd, 0, unroll=min(nch, RET_UNROLL))
    sfin_ref[0, 0, 0] = st_ref[...]

    if has_s0:
        st_ref[...] = s0_ref[0, 1, 0]
    else:
        st_ref[...] = jnp.zeros_like(st_ref)

    def bwd(m, carry):
        r = pl.multiple_of((nch - 1 - m) * c, c)
        qn = q_ref[pl.ds(r, c), :]
        kn = k_ref[pl.ds(r, c), :]
        vn = v_ref[pl.ds(r, c), :]
        oacc_ref[pl.ds(r, c), :] += _dot((qn.astype(F32) * qdec_b).astype(BF16), st_ref[...].astype(BF16))
        kd = (kn.astype(F32) * kdec_b).astype(BF16)
        st_ref[...] = st_ref[...] * cdec_b + _dot_tn(kd, vn)
        return carry

    lax.fori_loop(0, nch, bwd, 0, unroll=min(nch, RET_UNROLL))
    sfin_ref[0, 1, 0] = st_ref[...]

    o = oacc_ref[...]
    o = o * lax.rsqrt(jnp.mean(o * o, axis=-1, keepdims=True) + EPS)
    o_ref[...] = (_silu(g_ref[...].astype(F32)) * o).astype(o_ref.dtype)


def _retention(z, dk, log_gamma, s0, nb, seq, row_blk0, name):
    hh = RET_HEADS
    dv = dk
    nch = seq // RET_CHUNK
    has_s0 = s0 is not None
    scale = float(dk) ** -0.5

    def col(off):
        return lambda b, h: (row_blk0 + b, off + h)

    in_specs = [pl.BlockSpec(memory_space=pltpu.SMEM),
                pl.BlockSpec((seq, dk), col(0)),
                pl.BlockSpec((seq, dk), col(hh)),
                pl.BlockSpec((seq, dv), col(2 * hh)),
                pl.BlockSpec((seq, dv), col(3 * hh))]
    args = [log_gamma, z, z, z, z]
    if has_s0:
        in_specs.append(pl.BlockSpec((1, 2, 1, dk, dv), lambda b, h: (b, 0, h, 0, 0)))
        args.append(s0)
    return pl.pallas_call(
        functools.partial(_ret_kernel, has_s0, nch, scale),
        grid=(nb, hh),
        in_specs=in_specs,
        out_specs=[pl.BlockSpec((seq, dv), lambda b, h: (b, h)),
                   pl.BlockSpec((1, 2, 1, dk, dv), lambda b, h: (b, 0, h, 0, 0))],
        out_shape=[jax.ShapeDtypeStruct((nb * seq, hh * dv), BF16),
                   jax.ShapeDtypeStruct((nb, 2, hh, dk, dv), F32)],
        scratch_shapes=[pltpu.VMEM((seq, dv), F32), pltpu.VMEM((dk, dv), F32)],
        compiler_params=_cp(("parallel", "parallel")),
        name=name,
    )(*args)


def _s5_kernel(rc, ncc, bc, ncs, bs, u_ref, w1_ref, v_ref, lam_ref, d_ref, x0_ref,
               y_ref, fin_ref, sre, sim, are, aim, bre, bim):
    u = u_ref[0]
    z = _dot(u, w1_ref[0])
    nq = u.shape[1]
    y0 = z[:, :nq] + d_ref[0] * u.astype(F32)
    sre[...] = z[:, nq:nq + S5_LANES]
    sim[...] = z[:, nq + S5_LANES:nq + 2 * S5_LANES]
    lr = lam_ref[0, 0:1, :]
    li = lam_ref[0, 1:2, :]

    def scan(row0, nc, b, xr0, xi0):
        is_f = lax.broadcasted_iota(jnp.int32, (b, S5_LANES), 1) < S5_LANES // 2

        def body(s, carry):
            xr, xi = carry
            rf = pl.multiple_of(row0 + s * b, b)
            rb = pl.multiple_of(row0 + (nc - 1 - s) * b, b)
            are[pl.ds(rf, b), :] = xr
            aim[pl.ds(rf, b), :] = xi
            bre[pl.ds(rb, b), :] = xr
            bim[pl.ds(rb, b), :] = xi
            sr = jnp.where(is_f, sre[pl.ds(rf, b), :], sre[pl.ds(rb, b), :])
            si = jnp.where(is_f, sim[pl.ds(rf, b), :], sim[pl.ds(rb, b), :])
            return xr * lr - xi * li + sr, xi * lr + xr * li + si

        return lax.fori_loop(0, nc, body, (xr0, xi0))

    zero = jnp.zeros((bc, S5_LANES), F32)
    fr, fi = scan(0, ncc, bc, zero, zero)
    fin_ref[0, 0] = fr
    fin_ref[0, 1] = fi
    scan(rc, ncs, bs, x0_ref[0, 0], x0_ref[0, 1])
    y_ref[0] = (y0 + _dot(are[...].astype(BF16), v_ref[0, 0]) + _dot(aim[...].astype(BF16), v_ref[0, 1])
                + _dot(bre[...].astype(BF16), v_ref[0, 2]) + _dot(bim[...].astype(BF16), v_ref[0, 3])
                ).astype(y_ref.dtype)


def _cmul(ar, ai, br, bi):
    return ar * br - ai * bi, ar * bi + ai * br


def _cexp(re, im):
    e = jnp.exp(re)
    return e * jnp.cos(im), e * jnp.sin(im)


def _s5_operators(a_re, a_im, log_dt, b_re, b_im, c_re, c_im):
    q = S5_Q
    hp = lax.Precision.HIGHEST
    a_re, a_im = a_re.astype(F32), a_im.astype(F32)
    dt = jnp.exp(log_dt.astype(F32))[..., None]
    ldr, ldi = a_re * dt, a_im * dt
    lbr, lbi = _cexp(ldr, ldi)
    den = a_re * a_re + a_im * a_im
    fr = ((lbr - 1.0) * a_re + lbi * a_im) / den
    fi = (lbi * a_re - (lbr - 1.0) * a_im) / den
    bbr, bbi = _cmul(fr[..., None], fi[..., None], b_re.astype(F32), b_im.astype(F32))
    c_re, c_im = c_re.astype(F32), c_im.astype(F32)
    g, p = a_re.shape[1], a_re.shape[2]
    cg = bbr.shape[-1]
    steps = jnp.arange(q + 1, dtype=F32)
    pwr, pwi = _cexp(ldr[..., None] * steps, ldi[..., None] * steps)
    cpr, cpi = _cmul(c_re[..., None], c_im[..., None], pwr[:, :, None, :, :q], pwi[:, :, None, :, :q])
    kern = (jnp.einsum('dgcpt,dgpe->dgtce', cpr, bbr, precision=hp)
            - jnp.einsum('dgcpt,dgpe->dgtce', cpi, bbi, precision=hp))
    ti = jnp.arange(q)
    lag = ti[None, :] - ti[:, None]
    m_f = jnp.where((lag >= 0)[None, :, :, None, None], kern[0][:, jnp.clip(lag, 0, q - 1)], 0.0)
    m_b = jnp.where((lag <= 0)[None, :, :, None, None], kern[1][:, jnp.clip(-lag, 0, q - 1)], 0.0)
    m_tot = jnp.transpose(m_f + m_b, (0, 1, 4, 2, 3)).reshape(g, q * cg, q * cg)

    def state_in(d, t_idx):
        wr, wi = _cmul(pwr[d][..., t_idx][..., None], pwi[d][..., t_idx][..., None],
                       bbr[d][:, :, None, :], bbi[d][:, :, None, :])
        tr = lambda w: jnp.transpose(w, (0, 2, 3, 1)).reshape(g, q * cg, p)
        return tr(wr), tr(wi)

    wfr, wfi = state_in(0, q - 1 - ti)
    wbr, wbi = state_in(1, ti)
    w1 = jnp.concatenate([m_tot, wfr, wbr, wfi, wbi], axis=-1)

    def state_out(d, t_idx):
        vr, vi = _cmul(c_re[d][..., None], c_im[d][..., None],
                       pwr[d][:, None, :, :][..., t_idx], pwi[d][:, None, :, :][..., t_idx])
        tr = lambda w: jnp.transpose(w, (0, 2, 3, 1)).reshape(g, p, q * cg)
        return tr(vr), tr(vi)

    vfr, vfi = state_out(0, ti + 1)
    vbr, vbi = state_out(1, q - ti)
    zf = jnp.zeros_like(vfr)
    v = jnp.stack([jnp.concatenate([vfr, zf], axis=1), jnp.concatenate([-vfi, zf], axis=1),
                   jnp.concatenate([zf, vbr], axis=1), jnp.concatenate([zf, -vbi], axis=1)], axis=1)
    lam_pack = jnp.stack([jnp.concatenate([pwr[0][..., q], pwr[1][..., q]], axis=-1),
                          jnp.concatenate([pwi[0][..., q], pwi[1][..., q]], axis=-1)], axis=1)
    return w1.astype(BF16), v.astype(BF16), lam_pack


def _s5_rows(dm):
    rc = dm.lc // S5_Q * dm.bc
    return rc, rc + dm.ls // S5_Q * dm.bs


def _chunk_row_loops(dm, body):
    rc, _ = _s5_rows(dm)
    for tok0, row0, nc, b, seq in ((0, 0, dm.lc // S5_Q, dm.bc, dm.lc), (dm.tc, rc, dm.ls // S5_Q, dm.bs, dm.ls)):
        def step(n, c, tok0=tok0, row0=row0, b=b, seq=seq):
            body(tok0 + n * S5_Q, pl.multiple_of(row0 + n * b, b), b, seq)
            return c
        lax.fori_loop(0, nc, step, 0)


def _s5_pack_kernel(dm, u_ref, o_ref, a_ref, ab_ref):
    q, cg = S5_Q, S5_GROUP

    def gather(tok, row, b, seq):
        for i in range(q):
            a_ref[i, pl.ds(row, b), :] = u_ref[pl.ds(tok + i, b, stride=seq), :]

    _chunk_row_loops(dm, gather)
    ab_ref[...] = a_ref[...].astype(BF16)
    src = lax.broadcasted_iota(jnp.int32, (2 * LANES, q * cg), 0)
    dst = lax.broadcasted_iota(jnp.int32, (2 * LANES, q * cg), 1)
    for gl in range(LANES // cg):
        acc = None
        for i in range(0, q, 2):
            sel = jnp.where((src % LANES == gl * cg + dst % cg) & (dst // cg == i + src // LANES),
                            1.0, 0.0).astype(BF16)
            part = _dot(jnp.concatenate([ab_ref[i], ab_ref[i + 1]], axis=1), sel)
            acc = part if acc is None else acc + part
        o_ref[gl] = acc.astype(o_ref.dtype)


def _s5_unpack_kernel(dm, y_ref, o_ref, yt_ref):
    q, cg = S5_Q, S5_GROUP
    src = lax.broadcasted_iota(jnp.int32, (q * cg, 2 * LANES), 0)
    dst = lax.broadcasted_iota(jnp.int32, (q * cg, 2 * LANES), 1)
    for i in range(0, q, 2):
        acc = None
        for gl in range(LANES // cg):
            sel = jnp.where((src // cg == i + dst // LANES) & (src % cg == dst % cg)
                            & (dst % LANES // cg == gl), 1.0, 0.0).astype(BF16)
            part = _dot(y_ref[gl], sel)
            acc = part if acc is None else acc + part
        yt_ref[...] = acc

        def spread(tok, row, b, seq, i=i):
            o_ref[pl.ds(tok + i, b, stride=seq), :] = yt_ref[pl.ds(row, b), :LANES]
            o_ref[pl.ds(tok + i + 1, b, stride=seq), :] = yt_ref[pl.ds(row, b), LANES:]

        _chunk_row_loops(dm, spread)


def _s5_pack(dm, u_tok):
    t, w = u_tok.shape
    _, r = _s5_rows(dm)
    gps = LANES // S5_GROUP
    nq = S5_Q * S5_GROUP
    return pl.pallas_call(
        functools.partial(_s5_pack_kernel, dm),
        grid=(w // LANES,),
        in_specs=[pl.BlockSpec((t, LANES), lambda s: (0, s))],
        out_specs=pl.BlockSpec((gps, r, nq), lambda s: (s, 0, 0)),
        out_shape=jax.ShapeDtypeStruct((w // S5_GROUP, r, nq), BF16),
        scratch_shapes=[pltpu.VMEM((S5_Q, r, LANES), F32), pltpu.VMEM((S5_Q, r, LANES), BF16)],
        compiler_params=_cp(("parallel",)),
        name="s5_pack",
    )(u_tok)


def _s5_unpack(dm, y_g):
    g, r, nq = y_g.shape
    gps = LANES // S5_GROUP
    return pl.pallas_call(
        functools.partial(_s5_unpack_kernel, dm),
        grid=(g // gps,),
        in_specs=[pl.BlockSpec((gps, r, nq), lambda s: (s, 0, 0))],
        out_specs=pl.BlockSpec((dm.t, LANES), lambda s: (0, s)),
        out_shape=jax.ShapeDtypeStruct((dm.t, g * S5_GROUP), F32),
        scratch_shapes=[pltpu.VMEM((r, 2 * LANES), F32)],
        compiler_params=_cp(("parallel",)),
        name="s5_unpack",
    )(y_g)


def _s5(dm, u_tok, ops, d_skip, x0_re, x0_im):
    w1, v, lam_pack = ops
    g = w1.shape[0]
    p = lam_pack.shape[2] // 2
    assert 2 * p == S5_LANES
    q, cg = S5_Q, S5_GROUP
    ncc, ncs = dm.lc // q, dm.ls // q
    rc, r = _s5_rows(dm)
    u_g = _s5_pack(dm, u_tok)
    d_g =jnp.tile(d_skip.astype(F32).reshape(g, 1, cg), (1, q, 1)).reshape(g, 1, q * cg)
    x0 = jnp.stack([jnp.concatenate([x0_re[:, 0], x0_re[:, 1]], axis=-1),
                    jnp.concatenate([x0_im[:, 0], x0_im[:, 1]], axis=-1)], axis=0)
    x0 = x0.transpose(2, 0, 1, 3).astype(F32)
    nw = w1.shape[2]
    y_g, fin = pl.pallas_call(
        functools.partial(_s5_kernel, rc, ncc, dm.bc, ncs, dm.bs),
        grid=(g,),
        in_specs=[pl.BlockSpec((1, r, q * cg), lambda i: (i, 0, 0)),
                  pl.BlockSpec((1, q * cg, nw), lambda i: (i, 0, 0)),
                  pl.BlockSpec((1, 4, S5_LANES, q * cg), lambda i: (i, 0, 0, 0)),
                  pl.BlockSpec((1, 2, S5_LANES), lambda i: (i, 0, 0)),
                  pl.BlockSpec((1, 1, q * cg), lambda i: (i, 0, 0)),
                  pl.BlockSpec((1, 2, dm.bs, S5_LANES), lambda i: (i, 0, 0, 0))],
        out_specs=[pl.BlockSpec((1, r, q * cg), lambda i: (i, 0, 0)),
                   pl.BlockSpec((1, 2, dm.bc, S5_LANES), lambda i: (i, 0, 0, 0))],
        out_shape=[jax.ShapeDtypeStruct((g, r, q * cg), BF16),
                   jax.ShapeDtypeStruct((g, 2, dm.bc, S5_LANES), F32)],
        scratch_shapes=[pltpu.VMEM((r, S5_LANES), F32)] * 6,
        compiler_params=_cp(("parallel",)),
        name="s5",
    )(u_g, w1, v, lam_pack, d_g, x0)

    y_tok = _s5_unpack(dm, y_g)
    fin = fin.transpose(2, 1, 0, 3)
    fin_re = jnp.stack([fin[:, 0, :, :p], fin[:, 0, :, p:]], axis=1)
    fin_im = jnp.stack([fin[:, 1, :, :p], fin[:, 1, :, p:]], axis=1)
    return y_tok, fin_re, fin_im


def _glu_kernel(y_ref, w_ref, o_ref):
    z = _gelu_tanh(y_ref[...].astype(F32))
    o_ref[...] = (z * _sigmoid(_dot(z.astype(BF16), w_ref[...]))).astype(o_ref.dtype)


def _glu(dm, y, w):
    t, n = y.shape
    tm = dm.tm
    return pl.pallas_call(
        _glu_kernel,
        grid=(t // tm,),
        in_specs=[pl.BlockSpec((tm, n), lambda i: (i, 0)),
                  pl.BlockSpec((n, n), lambda i: (0, 0))],
        out_specs=pl.BlockSpec((tm, n), lambda i: (i, 0)),
        out_shape=jax.ShapeDtypeStruct((t, n), BF16),
        compiler_params=_cp(("parallel",)),
        name="s5_glu",
    )(y, w)


def _rot(x, ct, st):
    return x * ct + pltpu.roll(x, 64, 1) * st


def _mla_q_kernel(qscale, cq_ref, ckv_ref, kr_ref, qn_ref, kvn_ref, ct_ref, st_ref, w_ref,
                  q_ref, ckvn_ref, kro_ref):
    ct = ct_ref[...]
    st = st_ref[...]
    cq = cq_ref[...]
    cqn = (cq * lax.rsqrt(jnp.mean(cq * cq, axis=-1, keepdims=True) + EPS) * qn_ref[...]).astype(BF16)
    for h in range(MLA_HEADS):
        qh = _dot(cqn, w_ref[:, h * QK_PAD:(h + 1) * QK_PAD])
        q_ref[:, h * QK_PAD:h * QK_PAD + QK_NOPE] = (qh[:, :QK_NOPE] * qscale).astype(BF16)
        q_ref[:, h * QK_PAD + QK_NOPE:(h + 1) * QK_PAD] = (_rot(qh[:, QK_NOPE:], ct, st) * qscale).astype(BF16)
    ckv = ckv_ref[...]
    ckvn_ref[...] = ckv * lax.rsqrt(jnp.mean(ckv * ckv, axis=-1, keepdims=True) + EPS) * kvn_ref[...]
    kro_ref[...] = _rot(kr_ref[...], ct, st)


def _mla_q(dm, a, q_norm, kv_norm, ct, st, w_uq_ext, q_lora, kv_lora):
    t = a.shape[0]
    nq = w_uq_ext.shape[1]
    kr_blk = (q_lora + kv_lora) // 128
    tm = min(dm.tm, 512)
    qscale = float(QK_NOPE + QK_ROPE) ** -0.5 * math.log2(math.e)
    return pl.pallas_call(
        functools.partial(_mla_q_kernel, qscale),
        grid=(t // tm,),
        in_specs=[pl.BlockSpec((tm, q_lora), lambda i: (i, 0)),
                  pl.BlockSpec((tm, kv_lora), lambda i: (i, q_lora // kv_lora)),
                  pl.BlockSpec((tm, 128), lambda i: (i, kr_blk)),
                  pl.BlockSpec((1, q_lora), lambda i: (0, 0)),
                  pl.BlockSpec((1, kv_lora), lambda i: (0, 0)),
                  pl.BlockSpec((tm, 128), lambda i: (i, 0)),
                  pl.BlockSpec((tm, 128), lambda i: (i, 0)),
                  pl.BlockSpec((q_lora, nq), lambda i: (0, 0))],
        out_specs=[pl.BlockSpec((tm, nq), lambda i: (i, 0)),
                   pl.BlockSpec((tm, kv_lora), lambda i: (i, 0)),
                   pl.BlockSpec((tm, 128), lambda i: (i, 0))],
        out_shape=[jax.ShapeDtypeStruct((t, nq), BF16),
                   jax.ShapeDtypeStruct((t, kv_lora), F32),
                   jax.ShapeDtypeStruct((t, 128), F32)],
        compiler_params=_cp(("parallel",)),
        name="mla_q",
    )(a, a, a, q_norm, kv_norm, ct, st, w_uq_ext)


def _kv_expand_kernel(c_ref, kr_ref, wk_ref, wv_ref, k_ref, v_ref):
    c = c_ref[...].astype(BF16)
    kn = _dot(c, wk_ref[...]).astype(BF16)
    vv = _dot(c, wv_ref[...]).astype(BF16)
    kr = kr_ref[...].astype(BF16)
    ones = jnp.ones((c.shape[0], V_PAD - V_HEAD), BF16)
    for h in range(MLA_HEADS):
        k_ref[:, h * QK_PAD:h * QK_PAD + QK_NOPE] = kn[:, h * QK_NOPE:(h + 1) * QK_NOPE]
        k_ref[:, h * QK_PAD + QK_NOPE:(h + 1) * QK_PAD] = kr
        v_ref[:, h * V_PAD:h * V_PAD + V_HEAD] = vv[:, h * V_HEAD:(h + 1) * V_HEAD]
        v_ref[:, h * V_PAD + V_HEAD:(h + 1) * V_PAD] = ones


def _kv_expand(ckv_keys, kr_keys, w_uk, w_uv):
    nk, kv_lora = ckv_keys.shape
    tm = 512
    return pl.pallas_call(
        _kv_expand_kernel,
        grid=(nk // tm,),
        in_specs=[pl.BlockSpec((tm, kv_lora), lambda i: (i, 0)),
                  pl.BlockSpec((tm, 128), lambda i: (i, 0)),
                  pl.BlockSpec(w_uk.shape, lambda i: (0, 0)),
                  pl.BlockSpec(w_uv.shape, lambda i: (0, 0))],
        out_specs=[pl.BlockSpec((tm, MLA_HEADS * QK_PAD), lambda i: (i, 0)),
                   pl.BlockSpec((tm, MLA_HEADS * V_PAD), lambda i: (i, 0))],
        out_shape=[jax.ShapeDtypeStruct((nk, MLA_HEADS * QK_PAD), BF16),
                   jax.ShapeDtypeStruct((nk, MLA_HEADS * V_PAD), BF16)],
        compiler_params=_cp(("parallel",)),
        name="mla_kv_expand",
    )(ckv_keys, kr_keys, w_uk, w_uv)


def _attn_kernel(nh, nkb, q_ref, k_ref, v_ref, o_ref):
    kb = k_ref.shape[0] // nkb
    for h in range(nh):
        q = q_ref[:, h * QK_PAD:(h + 1) * QK_PAD]
        m, acc = None, None
        for j in range(nkb):
            s = _dot_nt(q, k_ref[j * kb:(j + 1) * kb, h * QK_PAD:(h + 1) * QK_PAD])
            mj = jnp.max(s, axis=-1, keepdims=True)
            m_new = mj if j == 0 else jnp.maximum(m, mj)
            pv = _dot(jnp.exp2(s - m_new).astype(BF16), v_ref[j * kb:(j + 1) * kb, h * V_PAD:(h + 1) * V_PAD])
            acc = pv if j == 0 else acc * jnp.exp2(m - m_new) + pv
            m = m_new
        o_ref[:, h * V_HEAD:(h + 1) * V_HEAD] = (acc[:, :V_HEAD] / acc[:, V_HEAD:V_HEAD + 1]).astype(o_ref.dtype)


def _attention(q, k, v, nb, lq, lk, tq, nh, q_row0, k_row0, name):
    nqt = lq // tq
    qb0, kb0 = q_row0 // tq, k_row0 // lk
    nkb = -(-lk // KB_MAX)
    assert lk % nkb == 0
    return pl.pallas_call(
        functools.partial(_attn_kernel, nh, nkb),
        grid=(nb, MLA_HEADS // nh, nqt),
        in_specs=[pl.BlockSpec((tq, nh * QK_PAD), lambda b, h, i: (qb0 + b * nqt + i, h)),
                  pl.BlockSpec((lk, nh * QK_PAD), lambda b, h, i: (kb0 + b, h)),
                  pl.BlockSpec((lk, nh * V_PAD), lambda b, h, i: (kb0 + b, h))],
        out_specs=pl.BlockSpec((tq, nh * V_HEAD), lambda b, h, i: (b * nqt + i, h)),
        out_shape=jax.ShapeDtypeStruct((nb * lq, MLA_HEADS * V_HEAD), BF16),
        compiler_params=_cp(("parallel", "parallel", "arbitrary")),
        name=name,
    )(q, k, v)


def _route_kernel(n_exp, x_ref, g_ref, sh_ref, sc_ref, rw_ref, rb_ref, tri_ref,
                  hp_ref, idx_ref, rank_ref, wt_ref, cnt_ref, run_ref):
    @pl.when(pl.program_id(0) == 0)
    def _():
        run_ref[...] = jnp.zeros_like(run_ref)

    h = _norm_mod(x_ref[...], g_ref[0], sc_ref[0], sh_ref[0])
    _store_row_tiles(hp_ref, _pack_halves(h))
    scores =_sigmoid(_dot_nt(rw_ref[...], h.astype(BF16)))
    biased = scores + rb_ref[...]
    per = n_exp // N_EXPERT_GROUPS
    assert per == 4
    rows_b = [biased[e:e + 1, :] for e in range(n_exp)]
    rows_s = [scores[e:e + 1, :] for e in range(n_exp)]
    best_sum, best_g = None, None
    for gi in range(N_EXPERT_GROUPS):
        a, b, c, d = rows_b[per * gi:per * gi + per]
        hi1, lo1 = jnp.maximum(a, b), jnp.minimum(a, b)
        hi2, lo2 = jnp.maximum(c, d), jnp.minimum(c, d)
        top2 = jnp.maximum(hi1, hi2) + jnp.maximum(jnp.minimum(hi1, hi2), jnp.maximum(lo1, lo2))
        if gi == 0:
            best_sum, best_g = top2, jnp.zeros_like(top2, dtype=jnp.int32)
        else:
            upd = top2 > best_sum
            best_sum = jnp.where(upd, top2, best_sum)
            best_g = jnp.where(upd, gi, best_g)
    vb, vs = [], []
    for k in range(per):
        accb, accs = rows_b[k], rows_s[k]
        for gi in range(1, N_EXPERT_GROUPS):
            sel = best_g == gi
            accb = jnp.where(sel, rows_b[per * gi + k], accb)
            accs = jnp.where(sel, rows_s[per * gi + k], accs)
        vb.append(accb)
        vs.append(accs)

    def first_argmax(vals, exclude):
        bv, bi, bs = None, None, None
        for k in range(per):
            v = vals[k] if exclude is None else jnp.where(exclude == k, -jnp.inf, vals[k])
            if k == 0:
                bv, bi, bs = v, jnp.zeros_like(best_g), vs[0]
            else:
                upd = v > bv
                bv = jnp.where(upd, v, bv)
                bi = jnp.where(upd, k, bi)
                bs = jnp.where(upd, vs[k], bs)
        return bi, bs

    i1, s1 = first_argmax(vb, None)
    i2, s2 = first_argmax(vb, i1)
    tot = s1 + s2
    e1 = best_g * per + i1
    e2 = best_g * per + i2
    idx_ref[0:1, :] = e1
    idx_ref[1:2, :] = e2
    wt_ref[0:1, :] = s1 / tot
    wt_ref[1:2, :] = s2 / tot
    eids = lax.broadcasted_iota(jnp.int32, (n_exp, e1.shape[1]), 0)
    hit1 = eids == e1
    hit2 = eids == e2
    oh1 = jnp.where(hit1, 1.0, 0.0)
    oh2 = jnp.where(hit2, 1.0, 0.0)
    p1 = _dot(oh1.astype(BF16), tri_ref[...])
    p2 = _dot(oh2.astype(BF16), tri_ref[...])
    c1 = jnp.sum(oh1, axis=1, keepdims=True)
    c2 = jnp.sum(oh2, axis=1, keepdims=True)
    run = run_ref[...]
    rank_ref[0:1, :] = jnp.sum(jnp.where(hit1, run + p1, 0.0), axis=0, keepdims=True).astype(jnp.int32)
    rank_ref[1:2, :] = jnp.sum(jnp.where(hit2, run + c1 + p2, 0.0), axis=0, keepdims=True).astype(jnp.int32)
    run_ref[...] = run + c1 + c2
    cnt_ref[...] = run + c1 + c2


def _route(dm, x, gain, mods, layer, rw_t, rb):
    t, d = x.shape
    n_exp = rw_t.shape[0]
    tm = dm.tm
    rt = d // 2 // LANES
    tri = (jnp.arange(tm)[:, None] < jnp.arange(tm)[None, :]).astype(BF16)
    return pl.pallas_call(
        functools.partial(_route_kernel, n_exp),
        grid=(t // tm,),
        in_specs=[pl.BlockSpec((tm, d), lambda i: (i, 0)),
                  pl.BlockSpec((1, 1, d), lambda i: (layer, 0, 0)),
                  pl.BlockSpec((1, 1, d), _mod_idx(dm, layer, 3, tm)),
                  pl.BlockSpec((1, 1, d), _mod_idx(dm, layer, 4, tm)),
                  pl.BlockSpec((n_exp, d), lambda i: (0, 0)),
                  pl.BlockSpec((n_exp, 1), lambda i: (0, 0)),
                  pl.BlockSpec((tm, tm), lambda i: (0, 0))],
        out_specs=[pl.BlockSpec((tm * rt, LANES), lambda i: (i, 0)),
                   pl.BlockSpec((2, tm), lambda i: (0, i)),
                   pl.BlockSpec((2, tm), lambda i: (0, i)),
                   pl.BlockSpec((2, tm), lambda i: (0, i)),
                   pl.BlockSpec((n_exp, 1), lambda i: (0, 0))],
        out_shape=[jax.ShapeDtypeStruct((t * rt, LANES), U32),
                   jax.ShapeDtypeStruct((2, t), jnp.int32),
                   jax.ShapeDtypeStruct((2, t), jnp.int32),
                   jax.ShapeDtypeStruct((2, t), F32),
                   jax.ShapeDtypeStruct((n_exp, 1), F32)],
        scratch_shapes=[pltpu.VMEM((n_exp, 1), F32)],
        compiler_params=_cp(("arbitrary",)),
        name="moe_route",
    )(x, gain, mods, mods, rw_t, rb, tri)


def _slot_plan(idx, rank, counts, n_exp, n_slots):
    counts = counts.reshape(n_exp).astype(jnp.int32)
    padded = ((counts + TM_MOE - 1) // TM_MOE) * TM_MOE
    ends = jnp.cumsum(padded)
    starts = ends - padded
    start_of = jnp.sum(jnp.where(idx[..., None] == jnp.arange(n_exp, dtype=jnp.int32), starts, 0), axis=-1)
    dest = (start_of + rank).reshape(-1).astype(jnp.int32)
    tile_start = jnp.arange(n_slots // TM_MOE, dtype=jnp.int32) * TM_MOE
    tile_exp = jnp.minimum(jnp.sum((ends[None, :] <= tile_start[:, None]).astype(jnp.int32), axis=1), n_exp - 1)
    tile_ok = (tile_start < ends[-1]).astype(jnp.int32)
    pad_lo = jnp.concatenate([starts + counts, ends[-1:]]).astype(jnp.int32)
    pad_n = (padded - counts).astype(jnp.int32)
    return dest, tile_exp.astype(jnp.int32), tile_ok, pad_lo, pad_n


def _scatter_kernel(n_exp, t_total, rows, rt, dest_ref, padlo_ref, padn_ref, hp_ref, xs_ref, zrow, sem):
    i = pl.program_id(0)
    base = i * rows
    tile_rows = TM_MOE * rt

    def slot(s):
        return xs_ref.at[pl.ds(pl.multiple_of(s * rt, rt), rt)]

    @pl.when(i == 0)
    def _():
        zrow[...] = jnp.zeros_like(zrow)
        for e in range(n_exp):
            lo = padlo_ref[e]

            def pad_copy(r, lo=lo):
                return pltpu.make_async_copy(zrow.at[pl.ds(0, rt)], slot(lo + r), sem)

            def start(r, c, pad_copy=pad_copy):
                pad_copy(r).start()
                return c

            def wait(r, c, pad_copy=pad_copy):
                pad_copy(r).wait()
                return c

            lax.fori_loop(0, padn_ref[e], start, 0)
            lax.fori_loop(0, padn_ref[e], wait, 0)

        tail0 = padlo_ref[n_exp]

        def tail_copy(k):
            row = pl.multiple_of((tail0 + k * TM_MOE) * rt, tile_rows)
            return pltpu.make_async_copy(zrow, xs_ref.at[pl.ds(row, tile_rows)], sem)

        def tail_start(k, c):
            tail_copy(k).start()
            return c

        def tail_wait(k, c):
            tail_copy(k).wait()
            return c

        n_tail = (xs_ref.shape[0] // rt - tail0) // TM_MOE
        lax.fori_loop(0, n_tail, tail_start, 0)
        lax.fori_loop(0, n_tail, tail_wait, 0)

    def copies(r):
        src = hp_ref.at[pl.ds(pl.multiple_of(r * rt, rt), rt)]
        return (pltpu.make_async_copy(src, slot(dest_ref[base + r]), sem),
                pltpu.make_async_copy(src, slot(dest_ref[t_total + base + r]), sem))

    def start(gi, c):
        for u in range(DMA_UNROLL):
            for prio, cp in enumerate(copies(gi * DMA_UNROLL + u)):
                cp.start(priority=prio)
        return c

    def wait(gi, c):
        for u in range(DMA_UNROLL):
            for cp in copies(gi * DMA_UNROLL + u):
                cp.wait()
        return c

    lax.fori_loop(0, rows // DMA_UNROLL, start, 0)
    lax.fori_loop(0, rows // DMA_UNROLL, wait, 0)


def _scatter_rows(hp, dest, pad_lo, pad_n, n_slots, rows, rt):
    t = hp.shape[0] // rt
    n_exp = pad_n.shape[0]
    return pl.pallas_call(
        functools.partial(_scatter_kernel, n_exp, t, rows, rt),
        grid_spec=pltpu.PrefetchScalarGridSpec(
            num_scalar_prefetch=3,
            grid=(t // rows,),
            in_specs=[pl.BlockSpec((rows * rt, LANES), lambda i, *_: (i, 0))],
            out_specs=pl.BlockSpec(memory_space=pl.ANY),
            scratch_shapes=[pltpu.VMEM((TM_MOE * rt, LANES), U32), pltpu.SemaphoreType.DMA(())]),
        out_shape=jax.ShapeDtypeStruct((n_slots * rt, LANES), U32),
        compiler_params=_cp(("arbitrary",)),
        name="moe_scatter",
    )(dest, pad_lo, pad_n, hp)


def _expert_kernel(rt, te_ref, ok_ref, x_ref, wg_ref, wu_ref, wd_ref, o_ref, wg_b, wu_b, wd_b):
    i = pl.program_id(0)
    new_expert = jnp.logical_or(i == 0, te_ref[i] != te_ref[jnp.maximum(i - 1, 0)])

    @pl.when(new_expert)
    def _():
        wg_b[...] = wg_ref[0, 0].astype(BF16)
        wu_b[...] = wu_ref[0, 0].astype(BF16)
        wd_b[...] = wd_ref[0, 0].astype(BF16)

    @pl.when(ok_ref[i] == 1)
    def _():
        x_lo, x_hi = _unpack_halves(_load_row_tiles(x_ref, rt))
        x_lo, x_hi = x_lo.astype(BF16), x_hi.astype(BF16)
        half = x_lo.shape[1]
        h1 = _dot(x_lo, wg_b[:half, :]) + _dot(x_hi, wg_b[half:, :])
        h2 = _dot(x_lo, wu_b[:half, :]) + _dot(x_hi, wu_b[half:, :])
        act = (_silu(h1) * h2).astype(BF16)
        _store_row_tiles(o_ref, _pack_halves(_dot(act, wd_b[...])))

    @pl.when(ok_ref[i] == 0)
    def _():
        o_ref[...] = jnp.zeros_like(o_ref)


def _experts(x_sorted, tile_exp, tile_ok, w_gate, w_up, w_down, layer, rt):
    n_slots = x_sorted.shape[0] // rt
    d, f = w_gate.shape[-2:]
    return pl.pallas_call(
        functools.partial(_expert_kernel, rt),
        grid_spec=pltpu.PrefetchScalarGridSpec(
            num_scalar_prefetch=2,
            grid=(n_slots // TM_MOE,),
            in_specs=[pl.BlockSpec((TM_MOE * rt, LANES), lambda i, te, ok: (i, 0)),
                      pl.BlockSpec((1, 1, d, f), lambda i, te, ok: (layer, te[i], 0, 0)),
                      pl.BlockSpec((1, 1, d, f), lambda i, te, ok: (layer, te[i], 0, 0)),
                      pl.BlockSpec((1, 1, f, d), lambda i, te, ok: (layer, te[i], 0, 0))],
            out_specs=pl.BlockSpec((TM_MOE * rt, LANES), lambda i, te, ok: (i, 0)),
            scratch_shapes=[pltpu.VMEM((d, f), BF16), pltpu.VMEM((d, f), BF16), pltpu.VMEM((f, d), BF16)]),
        out_shape=jax.ShapeDtypeStruct((n_slots * rt, LANES), U32),
        compiler_params=_cp(("arbitrary",)),
        name="moe_experts",
    )(tile_exp, tile_ok, x_sorted, w_gate, w_up, w_down)


def _combine_kernel(t_total, rows, rt, dest_ref, x_ref, gate_ref, wt_ref, y_ref, o_ref, y0, y1, sem):
    base = pl.program_id(0) * rows

    def slot(s):
        return y_ref.at[pl.ds(pl.multiple_of(s * rt, rt), rt)]

    def copies(r):
        dst = pl.ds(pl.multiple_of(r * rt, rt), rt)
        return (pltpu.make_async_copy(slot(dest_ref[base + r]), y0.at[dst], sem),
                pltpu.make_async_copy(slot(dest_ref[t_total + base + r]), y1.at[dst], sem))

    def start(gi, c):
        for u in range(DMA_UNROLL):
            for prio, cp in enumerate(copies(gi * DMA_UNROLL + u)):
                cp.start(priority=prio)
        return c

    def wait(gi, c):
        for u in range(DMA_UNROLL):
            for cp in copies(gi * DMA_UNROLL + u):
                cp.wait()
        return c

    lax.fori_loop(0, rows // DMA_UNROLL, start, 0)
    lax.fori_loop(0, rows // DMA_UNROLL, wait, 0)
    w0 = wt_ref[:, 0:1]
    w1 = wt_ref[:, 1:2]
    a_lo, a_hi = _unpack_halves(_load_row_tiles(y0, rt))
    b_lo, b_hi = _unpack_halves(_load_row_tiles(y1, rt))
    half = a_lo.shape[1]
    gate = gate_ref[0]
    o_ref[:, :half] = x_ref[:, :half] + gate[:, :half] * (w0 * a_lo + w1 * b_lo)
    o_ref[:, half:] = x_ref[:, half:] + gate[:, half:] * (w0 * a_hi + w1 * b_hi)


def _combine(dm, x, mods, layer, y_sorted, dest, wts_t):
    t, d = x.shape
    rows = min(dm.tm, 512)
    rt = d // 2 // LANES
    return pl.pallas_call(
        functools.partial(_combine_kernel, t, rows, rt),
        grid_spec=pltpu.PrefetchScalarGridSpec(
            num_scalar_prefetch=1,
            grid=(t // rows,),
            in_specs=[pl.BlockSpec((rows, d), lambda i, dst: (i, 0)),
                      pl.BlockSpec((1, 1, d), lambda i, dst: _mod_idx(dm, layer, 5, rows)(i)),
                      pl.BlockSpec((rows, 2), lambda i, dst: (i, 0)),
                      pl.BlockSpec(memory_space=pl.ANY)],
            out_specs=pl.BlockSpec((rows, d), lambda i, dst: (i, 0)),
            scratch_shapes=[pltpu.VMEM((rows * rt, LANES), U32), pltpu.VMEM((rows * rt, LANES), U32),
                            pltpu.SemaphoreType.DMA(())]),
        out_shape=jax.ShapeDtypeStruct((t, d), F32),
        compiler_params=_cp(("arbitrary",)),
        name="moe_combine",
    )(dest, x, mods, wts_t, y_sorted)


def _moe(dm, x, gain, mods, layer, rw_t, rb, w_gate, w_up, w_down):
    n_exp = rw_t.shape[0]
    hp, idx, rank, wts, counts = _route(dm, x, gain, mods, layer, rw_t, rb)
    n_slots = 2 * dm.t + n_exp * TM_MOE
    dest, tile_exp, tile_ok, pad_lo, pad_n = _slot_plan(idx, rank, counts, n_exp, n_slots)
    rt = dm.d // 2 // LANES
    x_sorted = _scatter_rows(hp, dest, pad_lo, pad_n, n_slots, min(dm.tm, 512), rt)
    y_sorted = _experts(x_sorted, tile_exp, tile_ok, w_gate, w_up, w_down, layer, rt)
    return _combine(dm, x, mods, layer, y_sorted, dest, wts.T)


def _final_norm_kernel(x_ref, g_ref, o_ref):
    x = x_ref[...]
    o_ref[...] = x * lax.rsqrt(jnp.mean(x * x, axis=-1, keepdims=True) + EPS) * g_ref[...]


def _final_norm(dm, x, gain):
    t, d = x.shape
    tm = dm.tm
    return pl.pallas_call(
        _final_norm_kernel,
        grid=(t // tm,),
        in_specs=[pl.BlockSpec((tm, d), lambda i: (i, 0)),
                  pl.BlockSpec((1, d), lambda i: (0, 0))],
        out_specs=pl.BlockSpec((tm, d), lambda i: (i, 0)),
        out_shape=jax.ShapeDtypeStruct((t, d), F32),
        compiler_params=_cp(("parallel",)),
        name="final_norm",
    )(x, gain.reshape(1, d))


def _rope_tables(dm):
    rows = dm.ls // GRID_W
    row = jnp.repeat(jnp.arange(rows, dtype=F32), GRID_W)
    col = jnp.tile(jnp.arange(GRID_W, dtype=F32), rows)
    half = QK_ROPE // 2
    freqs = jnp.power(ROPE_BASE, -jnp.arange(0, half, 2, dtype=F32) / half)
    ar, ac = row[:, None] * freqs, col[:, None] * freqs
    zeros = jnp.zeros((dm.ls, 128 - QK_ROPE), F32)
    ct = jnp.concatenate([jnp.cos(ar), jnp.cos(ar), jnp.cos(ac), jnp.cos(ac), zeros], axis=-1)
    st = jnp.concatenate([-jnp.sin(ar), jnp.sin(ar), -jnp.sin(ac), jnp.sin(ac), zeros], axis=-1)
    ct_c = jnp.concatenate([jnp.ones((dm.tc, QK_ROPE), F32), jnp.zeros((dm.tc, 128 - QK_ROPE), F32)], axis=-1)
    ct = jnp.concatenate([ct_c, jnp.tile(ct, (dm.bs, 1))], axis=0)
    st = jnp.concatenate([jnp.zeros((dm.tc, 128), F32), jnp.tile(st, (dm.bs, 1))], axis=0)
    return ct, st


def _swap_halves_cols(w):
    qt = QK_ROPE // 4
    return jnp.concatenate([w[..., qt:2 * qt], w[..., :qt], w[..., 3 * qt:], w[..., 2 * qt:3 * qt]], axis=-1)


def kernel(x_prompt, x_sample, c, state_ret, state_s5_re, state_s5_im, cache_ckv, cache_krope, c_ctx, ada_w, ada_b, norm_mix, norm_ffn, norm_final, even_w_in, even_w_out, ret_decay, s5_a_re, s5_a_im, s5_log_dt, s5_b_re, s5_b_im, s5_c_re, s5_c_im, s5_d, s5_w_glu, mla_w_in, mla_q_norm, mla_w_uq, mla_kv_norm, mla_w_ukv, mla_w_out, router_w, router_bias, moe_w_gate, moe_w_up, moe_w_down):
    dm = _Dims(x_prompt, x_sample)
    d = dm.d
    depth = ada_w.shape[0]
    n_exp = router_w.shape[1]
    past = cache_ckv.shape[2]
    q_lora = mla_q_norm.shape[1]
    kv_lora = mla_kv_norm.shape[1]

    x = jnp.concatenate([x_prompt.reshape(dm.tc, d), x_sample.reshape(dm.ts, d)], axis=0)
    cond = jnp.zeros((dm.rp, d), F32).at[0].set(c_ctx).at[1:1 + dm.bs].set(c)
    mods = _adaln(cond, ada_w, ada_b, tn=d * N_MOD // 8).reshape(depth * dm.rp * N_MOD, 1, d)
    g_mix = norm_mix.reshape(depth, 1, d)
    g_ffn = norm_ffn.reshape(depth, 1, d)
    rw_t = router_w.T.astype(BF16)
    rb = router_bias.astype(F32).reshape(n_exp, 1)
    ct, st = _rope_tables(dm)
    s5_ops = jax.vmap(_s5_operators)(s5_a_re, s5_a_im, s5_log_dt, s5_b_re, s5_b_im, s5_c_re, s5_c_im)

    rets, s5r, s5i, ckvs, krs = [], [], [], [], []
    for layer in range(depth):
        if layer % 2 == 0:
            i = layer // 2
            ret_w = even_w_out.shape[1] - s5_d.shape[1]
            s5_w = s5_d.shape[1]
            z, u = _norm_mod_matmul_split(dm, x, g_mix, mods, layer, 0, even_w_in[i].astype(BF16), tn=s5_w,
                                          name="even_in_proj")
            log_gamma = -jnp.exp(ret_decay[i].astype(F32))
            ro_c, sfin = _retention(z, ret_w // RET_HEADS, log_gamma, None, dm.bc, dm.lc, 0, "retention_ctx")
            ro_s, _ = _retention(z, ret_w // RET_HEADS, log_gamma, state_ret[:, i].astype(F32), dm.bs, dm.ls,
                                 dm.tc // dm.ls, "retention_lat")
            ret_out = jnp.concatenate([ro_c, ro_s], axis=0)
            ops = tuple(op[i] for op in s5_ops)
            y, f_re, f_im = _s5(dm, u, ops, s5_d[i], state_s5_re[:, i], state_s5_im[:, i])
            s5_out = _glu(dm, y, s5_w_glu[i].astype(BF16))
            w_out = even_w_out[i].astype(BF16)
            x = _matmul_residual(dm, x, mods, layer, 2, [ret_out, s5_out], [w_out[:ret_w], w_out[ret_w:]],
                                 tn=d // 2, name="even_out_proj")
            rets.append(sfin)
            s5r.append(f_re)
            s5i.append(f_im)
        else:
            j = layer // 2
            w_in = mla_w_in[j]
            w_in_ext = jnp.concatenate([w_in, _swap_halves_cols(w_in[:, q_lora + kv_lora:])], axis=1).astype(BF16)
            a = _norm_mod_matmul(dm, x, g_mix, mods, layer, 0, w_in_ext, F32, tn=w_in_ext.shape[1],
                                 name="mla_in_proj")
            w_uq = mla_w_uq[j].reshape(q_lora, MLA_HEADS, QK_NOPE + QK_ROPE)
            w_uq_ext = jnp.concatenate([w_uq, _swap_halves_cols(w_uq[..., QK_NOPE:])], axis=-1)
            w_uq_ext = w_uq_ext.reshape(q_lora, MLA_HEADS * QK_PAD).astype(BF16)
            q, ckv_n, kr = _mla_q(dm, a, mla_q_norm[j].reshape(1, q_lora), mla_kv_norm[j].reshape(1, kv_lora),
                                  ct, st, w_uq_ext, q_lora, kv_lora)
            ckv_lat = jnp.concatenate([cache_ckv[:, j].astype(F32), ckv_n[dm.tc:].reshape(dm.bs, dm.ls, kv_lora)], axis=1)
            kr_cache = jnp.concatenate([cache_krope[:, j].astype(F32),
                                        jnp.zeros((dm.bs, past, 128 - QK_ROPE), F32)], axis=-1)
            kr_lat = jnp.concatenate([kr_cache, kr[dm.tc:].reshape(dm.bs, dm.ls, 128)], axis=1)
            lk = past + dm.ls
            ckv_keys = jnp.concatenate([ckv_lat.reshape(dm.bs * lk, kv_lora), ckv_n[:dm.tc]], axis=0)
            kr_keys = jnp.concatenate([kr_lat.reshape(dm.bs * lk, 128), kr[:dm.tc]], axis=0)
            w_ukv = mla_w_ukv[j].reshape(kv_lora, MLA_HEADS, QK_NOPE + V_HEAD)
            w_uk = w_ukv[..., :QK_NOPE].reshape(kv_lora, MLA_HEADS * QK_NOPE).astype(BF16)
            w_uv = w_ukv[..., QK_NOPE:].reshape(kv_lora, MLA_HEADS * V_HEAD).astype(BF16)
            k_all, v_all = _kv_expand(ckv_keys, kr_keys, w_uk, w_uv)
            o_c = _attention(q, k_all, v_all, dm.bc, dm.lc, dm.lc, dm.lc, MLA_HEADS, 0, dm.bs * lk, "attn_ctx")
            o_s = _attention(q, k_all, v_all, dm.bs, dm.ls, lk, min(TQ_MAX, dm.ls), LAT_HEADS_PER_STEP, dm.tc, 0,
                             "attn_lat")
            o = jnp.concatenate([o_c, o_s], axis=0)
            x = _matmul_residual(dm, x, mods, layer, 2, [o], [mla_w_out[j].astype(BF16)], tn=d // 2,
                                 name="mla_out_proj")
            ckvs.append(ckv_n[:dm.tc].reshape(dm.bc, dm.lc, kv_lora))
            krs.append(kr[:dm.tc, :QK_ROPE].reshape(dm.bc, dm.lc, QK_ROPE))
        x = _moe(dm, x, g_ffn, mods, layer, rw_t, rb, moe_w_gate, moe_w_up, moe_w_down)

    y = _final_norm(dm, x, norm_final)
    y_prompt = y[:dm.tc].reshape(dm.bc, dm.lc, d)
    y_sample = y[dm.tc:].reshape(dm.bs, dm.ls, d)
    return (y_prompt, y_sample, jnp.stack(rets, axis=1), jnp.stack(s5r, axis=1), jnp.stack(s5i, axis=1),
            jnp.stack(ckvs, axis=1), jnp.stack(krs, axis=1))
```

```python
import functools
import math

import jax
import jax.numpy as jnp
import numpy as np
from jax import lax
from jax.experimental import pallas as pl
from jax.experimental.pallas import tpu as pltpu

F32 = jnp.float32
BF16 = jnp.bfloat16
U32 = jnp.uint32
EPS = 1e-6

RET_HEADS = 4
RET_CHUNK = 128
RET_UNROLL = 4
S5_GROUP = 16
S5_Q = 16
S5_LANES = 128
LANES = 128
MLA_HEADS = 16
QK_NOPE = 128
QK_ROPE = 64
V_HEAD = 128
QK_PAD = 256
V_PAD = 256
GRID_W = 64
ROPE_BASE = 10000.0
N_EXPERT_GROUPS = 4
N_MOD = 6

VMEM_LIMIT = 56 * 1024 * 1024
TM_MAX = 1024
TQ_MAX = 512
KB_MAX = 768
LAT_HEADS_PER_STEP = 2
TM_MOE = 256
DMA_UNROLL = 8
HI_MASK = np.uint32(0xFFFF0000)


def _cp(sem, vmem=VMEM_LIMIT):
    return pltpu.CompilerParams(dimension_semantics=sem, vmem_limit_bytes=vmem)


def _sigmoid(x):
    return 1.0 / (1.0 + jnp.exp(-x))


def _silu(x):
    return x * _sigmoid(x)


def _gelu_tanh(x):
    return 0.5 * x * (1.0 + jnp.tanh(math.sqrt(2.0 / math.pi) * (x + 0.044715 * (x * x * x))))


def _dot(a, b):
    return jnp.dot(a, b, preferred_element_type=F32)


def _dot_nt(a, b):
    return lax.dot_general(a, b, (((1,), (1,)), ((), ())), preferred_element_type=F32)


def _dot_tn(a, b):
    return lax.dot_general(a, b, (((0,), (0,)), ((), ())), preferred_element_type=F32)


def _pack_halves(x):
    half = x.shape[1] // 2
    xb = x.astype(BF16).astype(F32)
    lo = lax.bitcast_convert_type(xb[:, :half], U32) >> 16
    hi = lax.bitcast_convert_type(xb[:, half:], U32) & HI_MASK
    return hi | lo


def _unpack_halves(w):
    return (lax.bitcast_convert_type(w << 16, F32), lax.bitcast_convert_type(w & HI_MASK, F32))


def _store_row_tiles(ref, w):
    r, rt = w.shape[0], w.shape[1] // LANES
    for s in range(rt):
        ref[pl.ds(s, r, stride=rt), :] = w[:, s * LANES:(s + 1) * LANES]


def _load_row_tiles(ref, rt):
    r = ref.shape[0] // rt
    return jnp.concatenate([ref[pl.ds(s, r, stride=rt), :] for s in range(rt)], axis=1)


class _Dims:
    def __init__(self, x_prompt, x_sample):
        self.bc, self.lc, self.d = x_prompt.shape
        self.bs, self.ls, _ = x_sample.shape
        self.tc = self.bc * self.lc
        self.ts = self.bs * self.ls
        self.t = self.tc + self.ts
        self.rp = -(-(1 + self.bs) // 8) * 8
        self.tm = min(TM_MAX, self.ls)
        assert self.tc % self.tm == 0 and self.ls % self.tm == 0

    def mod_row(self, i, tm):
        nct = self.tc // tm
        return jnp.where(i < nct, 0, 1 + (i - nct) // (self.ls // tm))


def _mod_idx(dm, layer, k, tm):
    def idx(i, *_):
        return ((layer * dm.rp + dm.mod_row(i, tm)) * N_MOD + k, 0, 0)
    return idx


def _adaln_kernel(c_ref, w_ref, b_ref, o_ref):
    cs = _silu(c_ref[...]).astype(BF16)
    o_ref[0] = _dot(cs, w_ref[0].astype(BF16)) + b_ref[0]


def _adaln(cond, ada_w, ada_b, tn):
    depth, d, n = ada_w.shape
    rp = cond.shape[0]
    return pl.pallas_call(
        _adaln_kernel,
        grid=(depth, n // tn),
        in_specs=[pl.BlockSpec((rp, d), lambda l, j: (0, 0)),
                  pl.BlockSpec((1, d, tn), lambda l, j: (l, 0, j)),
                  pl.BlockSpec((1, 1, tn), lambda l, j: (l, 0, j))],
        out_specs=pl.BlockSpec((1, rp, tn), lambda l, j: (l, 0, j)),
        out_shape=jax.ShapeDtypeStruct((depth, rp, n), F32),
        compiler_params=_cp(("parallel", "parallel")),
        name="adaln",
    )(cond, ada_w, ada_b.reshape(depth, 1, n))


def _norm_mod(x, g, sc, sh):
    y = x * lax.rsqrt(jnp.mean(x * x, axis=-1, keepdims=True) + EPS) * g
    return y * (1.0 + sc) + sh


def _nmm_kernel(x_ref, g_ref, sh_ref, sc_ref, w_ref, o_ref, hn_ref):
    @pl.when(pl.program_id(1) == 0)
    def _():
        hn_ref[...] = _norm_mod(x_ref[...], g_ref[0], sc_ref[0], sh_ref[0]).astype(BF16)

    o_ref[...] = _dot(hn_ref[...], w_ref[...]).astype(o_ref.dtype)


def _norm_mod_matmul(dm, x, gain, mods, layer, k_shift, w, out_dtype, tn, name):
    t, d = x.shape
    n = w.shape[1]
    tm = dm.tm
    return pl.pallas_call(
        _nmm_kernel,
        grid=(t // tm, n // tn),
        in_specs=[pl.BlockSpec((tm, d), lambda i, j: (i, 0)),
                  pl.BlockSpec((1, 1, d), lambda i, j: (layer, 0, 0)),
                  pl.BlockSpec((1, 1, d), _mod_idx(dm, layer, k_shift, tm)),
                  pl.BlockSpec((1, 1, d), _mod_idx(dm, layer, k_shift + 1, tm)),
                  pl.BlockSpec((d, tn), lambda i, j: (0, j))],
        out_specs=pl.BlockSpec((tm, tn), lambda i, j: (i, j)),
        out_shape=jax.ShapeDtypeStruct((t, n), out_dtype),
        scratch_shapes=[pltpu.VMEM((tm, d), BF16)],
        compiler_params=_cp(("parallel", "arbitrary")),
        name=name,
    )(x, gain, mods, mods, w)


def _nmm_split_kernel(n_main, x_ref, g_ref, sh_ref, sc_ref, w_ref, o_main_ref, o_tail_ref, hn_ref):
    j = pl.program_id(1)

    @pl.when(j == 0)
    def _():
        hn_ref[...] = _norm_mod(x_ref[...], g_ref[0], sc_ref[0], sh_ref[0]).astype(BF16)

    acc = _dot(hn_ref[...], w_ref[...])

    @pl.when(j < n_main)
    def _():
        o_main_ref[...] = acc.astype(o_main_ref.dtype)

    @pl.when(j >= n_main)
    def _():
        o_tail_ref[...] = acc


def _norm_mod_matmul_split(dm, x, gain, mods, layer, k_shift, w, tn, name):
    t, d = x.shape
    n = w.shape[1]
    tm = dm.tm
    n_main = n // tn - 1
    return pl.pallas_call(
        functools.partial(_nmm_split_kernel, n_main),
        grid=(t // tm, n // tn),
        in_specs=[pl.BlockSpec((tm, d), lambda i, j: (i, 0)),
                  pl.BlockSpec((1, 1, d), lambda i, j: (layer, 0, 0)),
                  pl.BlockSpec((1, 1, d), _mod_idx(dm, layer, k_shift, tm)),
                  pl.BlockSpec((1, 1, d), _mod_idx(dm, layer, k_shift + 1, tm)),
                  pl.BlockSpec((d, tn), lambda i, j: (0, j))],
        out_specs=[pl.BlockSpec((tm, tn), lambda i, j: (i, jnp.minimum(j, n_main - 1))),
                   pl.BlockSpec((tm, tn), lambda i, j: (i, 0))],
        out_shape=[jax.ShapeDtypeStruct((t, n_main * tn), BF16),
                   jax.ShapeDtypeStruct((t, tn), F32)],
        scratch_shapes=[pltpu.VMEM((tm, d), BF16)],
        compiler_params=_cp(("parallel", "arbitrary")),
        name=name,
    )(x, gain, mods, mods, w)


def _mmres_kernel(split, nct, x_ref, gate_ref, *refs):
    n_a = sum(2 if sp else 1 for sp in split)
    a_refs, w_refs, o_ref = refs[:n_a], refs[n_a:n_a + len(split)], refs[n_a + len(split)]

    def run(use_ctx):
        acc, pos = None, 0
        for sp, w_ref in zip(split, w_refs):
            a_ref = a_refs[pos if (use_ctx or not sp) else pos + 1]
            pos += 2 if sp else 1
            part = _dot(a_ref[...], w_ref[...])
            acc = part if acc is None else acc + part
        o_ref[...] = x_ref[...] + gate_ref[0] * acc

    if any(split):
        is_ctx = pl.program_id(0) < nct
        pl.when(is_ctx)(lambda: run(True))
        pl.when(jnp.logical_not(is_ctx))(lambda: run(False))
    else:
        run(True)


def _matmul_residual(dm, x, mods, layer, k_gate, acts, ws, tn, name):
    t, d = x.shape
    tm = dm.tm
    nct = dm.tc // tm
    split = tuple(isinstance(a, tuple) for a in acts)

    def gate_idx(i, j):
        return ((layer * dm.rp + dm.mod_row(i, tm)) * N_MOD + k_gate, 0, j)

    in_specs = [pl.BlockSpec((tm, tn), lambda i, j: (i, j)),
                pl.BlockSpec((1, 1, tn), gate_idx)]
    flat = []
    for a in acts:
        if isinstance(a, tuple):
            in_specs.append(pl.BlockSpec((tm, a[0].shape[1]), lambda i, j: (jnp.minimum(i, nct - 1), 0)))
            in_specs.append(pl.BlockSpec((tm, a[1].shape[1]), lambda i, j: (jnp.maximum(i - nct, 0), 0)))
            flat += list(a)
        else:
            in_specs.append(pl.BlockSpec((tm, a.shape[1]), lambda i, j: (i, 0)))
            flat.append(a)
    in_specs += [pl.BlockSpec((w.shape[0], tn), lambda i, j: (0, j)) for w in ws]
    return pl.pallas_call(
        functools.partial(_mmres_kernel, split, nct),
        grid=(t // tm, d // tn),
        in_specs=in_specs,
        out_specs=pl.BlockSpec((tm, tn), lambda i, j: (i, j)),
        out_shape=jax.ShapeDtypeStruct((t, d), F32),
        compiler_params=_cp(("parallel", "parallel")),
        name=name,
    )(x, mods, *flat, *ws)


def _ret_kernel(has_s0, nch, scale, lg_ref, q_ref, k_ref, v_ref, g_ref, *refs):
    if has_s0:
        s0_ref, o_ref, sfin_ref, oacc_ref, st_ref = refs
    else:
        o_ref, sfin_ref, oacc_ref, st_ref = refs
    c = RET_CHUNK
    h = pl.program_id(1)
    lgf = lg_ref[0, h]
    lgb = lg_ref[1, h]
    dv = v_ref.shape[1]

    ii = lax.broadcasted_iota(jnp.int32, (c, c), 0).astype(F32)
    jj = lax.broadcasted_iota(jnp.int32, (c, c), 1).astype(F32)
    diff = ii - jj
    dtot = (jnp.where(diff >= 0, jnp.exp(lgf * jnp.maximum(diff, 0.0)), 0.0)
            + jnp.where(diff <= 0, jnp.exp(lgb * jnp.maximum(-diff, 0.0)), 0.0)) * scale
    pos = lax.broadcasted_iota(jnp.int32, (c, 1), 0).astype(F32)
    qdec_f = jnp.exp(lgf * (pos + 1.0)) * scale
    kdec_f = jnp.exp(lgf * (c - 1.0 - pos))
    qdec_b = jnp.exp(lgb * (c - pos)) * scale
    kdec_b = jnp.exp(lgb * pos)
    cdec_f = jnp.exp(jnp.full((1, dv), lgf * c, F32))
    cdec_b = jnp.exp(jnp.full((1, dv), lgb * c, F32))

    if has_s0:
        st_ref[...] = s0_ref[0, 0, 0, 0].astype(F32)
    else:
        st_ref[...] = jnp.zeros_like(st_ref)

    def fwd(n, carry):
        r = pl.multiple_of(n * c, c)
        qn = q_ref[pl.ds(r, c), :]
        kn = k_ref[pl.ds(r, c), :]
        vn = v_ref[pl.ds(r, c), :]
        p = (_dot_nt(qn, kn) * dtot).astype(BF16)
        o = _dot(p, vn)
        o += _dot((qn.astype(F32) * qdec_f).astype(BF16), st_ref[...].astype(BF16))
        oacc_ref[pl.ds(r, c), :] = o
        kd = (kn.astype(F32) * kdec_f).astype(BF16)
        st_ref[...] = st_ref[...] * cdec_f + _dot_tn(kd, vn)
        return carry

    lax.fori_loop(0, nch, fwd, 0, unroll=min(nch, RET_UNROLL))
    sfin_ref[0, 0, 0] = st_ref[...]

    if has_s0:
        st_ref[...] = s0_ref[0, 0, 1, 0].astype(F32)
    else:
        st_ref[...] = jnp.zeros_like(st_ref)

    def bwd(m, carry):
        r = pl.multiple_of((nch - 1 - m) * c, c)
        qn = q_ref[pl.ds(r, c), :]
        kn = k_ref[pl.ds(r, c), :]
        vn = v_ref[pl.ds(r, c), :]
        oacc_ref[pl.ds(r, c), :] += _dot((qn.astype(F32) * qdec_b).astype(BF16), st_ref[...].astype(BF16))
        kd = (kn.astype(F32) * kdec_b).astype(BF16)
        st_ref[...] = st_ref[...] * cdec_b + _dot_tn(kd, vn)
        return carry

    lax.fori_loop(0, nch, bwd, 0, unroll=min(nch, RET_UNROLL))
    sfin_ref[0, 1, 0] = st_ref[...]

    o = oacc_ref[...]
    o = o * lax.rsqrt(jnp.mean(o * o, axis=-1, keepdims=True) + EPS)
    o_ref[...] = (_silu(g_ref[...].astype(F32)) * o).astype(o_ref.dtype)


def _retention(z, dk, log_gamma, s0, nb, seq, row_blk0, name):
    hh = RET_HEADS
    dv = dk
    nch = seq // RET_CHUNK
    has_s0 = s0 is not None
    scale = float(dk) ** -0.5

    def col(off):
        return lambda b, h: (row_blk0 + b, off + h)

    in_specs = [pl.BlockSpec(memory_space=pltpu.SMEM),
                pl.BlockSpec((seq, dk), col(0)),
                pl.BlockSpec((seq, dk), col(hh)),
                pl.BlockSpec((seq, dv), col(2 * hh)),
                pl.BlockSpec((seq, dv), col(3 * hh))]
    args = [log_gamma, z, z, z, z]
    if has_s0:
        s0_all, s0_layer = s0
        in_specs.append(pl.BlockSpec((1, 1, 2, 1, dk, dv), lambda b, h: (b, s0_layer, 0, h, 0, 0)))
        args.append(s0_all)
    return pl.pallas_call(
        functools.partial(_ret_kernel, has_s0, nch, scale),
        grid=(nb, hh),
        in_specs=in_specs,
        out_specs=[pl.BlockSpec((seq, dv), lambda b, h: (b, h)),
                   pl.BlockSpec((1, 2, 1, dk, dv), lambda b, h: (b, 0, h, 0, 0))],
        out_shape=[jax.ShapeDtypeStruct((nb * seq, hh * dv), BF16),
                   jax.ShapeDtypeStruct((nb, 2, hh, dk, dv), F32)],
        scratch_shapes=[pltpu.VMEM((seq, dv), F32), pltpu.VMEM((dk, dv), F32)],
        compiler_params=_cp(("parallel", "parallel")),
        name=name,
    )(*args)


def _s5_kernel(rc, ncc, bc, ncs, bs, u_ref, w1_ref, v_ref, lam_ref, d_ref, x0_ref,
               y_ref, fin_ref, sre, sim, are, aim, bre, bim):
    u = u_ref[0]
    z = _dot(u, w1_ref[0])
    nq = u.shape[1]
    y0 = z[:, :nq] + d_ref[0] * u.astype(F32)
    sre[...] = z[:, nq:nq + S5_LANES]
    sim[...] = z[:, nq + S5_LANES:nq + 2 * S5_LANES]
    lr = lam_ref[0, 0:1, :]
    li = lam_ref[0, 1:2, :]

    def scan(row0, nc, b, xr0, xi0):
        is_f = lax.broadcasted_iota(jnp.int32, (b, S5_LANES), 1) < S5_LANES // 2

        def body(s, carry):
            xr, xi = carry
            rf = pl.multiple_of(row0 + s * b, b)
            rb = pl.multiple_of(row0 + (nc - 1 - s) * b, b)
            are[pl.ds(rf, b), :] = xr
            aim[pl.ds(rf, b), :] = xi
            bre[pl.ds(rb, b), :] = xr
            bim[pl.ds(rb, b), :] = xi
            sr = jnp.where(is_f, sre[pl.ds(rf, b), :], sre[pl.ds(rb, b), :])
            si = jnp.where(is_f, sim[pl.ds(rf, b), :], sim[pl.ds(rb, b), :])
            return xr * lr - xi * li + sr, xi * lr + xr * li + si

        return lax.fori_loop(0, nc, body, (xr0, xi0))

    zero = jnp.zeros((bc, S5_LANES), F32)
    fr, fi = scan(0, ncc, bc, zero, zero)
    fin_ref[0, 0] = fr
    fin_ref[0, 1] = fi
    scan(rc, ncs, bs, x0_ref[0, 0], x0_ref[0, 1])
    y_ref[0] = (y0 + _dot(are[...].astype(BF16), v_ref[0, 0]) + _dot(aim[...].astype(BF16), v_ref[0, 1])
                + _dot(bre[...].astype(BF16), v_ref[0, 2]) + _dot(bim[...].astype(BF16), v_ref[0, 3])
                ).astype(y_ref.dtype)


def _cmul(ar, ai, br, bi):
    return ar * br - ai * bi, ar * bi + ai * br


def _cexp(re, im):
    e = jnp.exp(re)
    return e * jnp.cos(im), e * jnp.sin(im)


def _s5_operators(a_re, a_im, log_dt, b_re, b_im, c_re, c_im):
    q = S5_Q
    hp = lax.Precision.HIGHEST
    a_re, a_im = a_re.astype(F32), a_im.astype(F32)
    dt = jnp.exp(log_dt.astype(F32))[..., None]
    ldr, ldi = a_re * dt, a_im * dt
    lbr, lbi = _cexp(ldr, ldi)
    den = a_re * a_re + a_im * a_im
    fr = ((lbr - 1.0) * a_re + lbi * a_im) / den
    fi = (lbi * a_re - (lbr - 1.0) * a_im) / den
    bbr, bbi = _cmul(fr[..., None], fi[..., None], b_re.astype(F32), b_im.astype(F32))
    c_re, c_im = c_re.astype(F32), c_im.astype(F32)
    g, p = a_re.shape[1], a_re.shape[2]
    cg = bbr.shape[-1]
    steps = jnp.arange(q + 1, dtype=F32)
    pwr, pwi = _cexp(ldr[..., None] * steps, ldi[..., None] * steps)
    cpr, cpi = _cmul(c_re[..., None], c_im[..., None], pwr[:, :, None, :, :q], pwi[:, :, None, :, :q])
    kern = (jnp.einsum('dgcpt,dgpe->dgtce', cpr, bbr, precision=hp)
            - jnp.einsum('dgcpt,dgpe->dgtce', cpi, bbi, precision=hp))
    ti = jnp.arange(q)
    lag = ti[None, :] - ti[:, None]
    m_f = jnp.where((lag >= 0)[None, :, :, None, None], kern[0][:, jnp.clip(lag, 0, q - 1)], 0.0)
    m_b = jnp.where((lag <= 0)[None, :, :, None, None], kern[1][:, jnp.clip(-lag, 0, q - 1)], 0.0)
    m_tot = jnp.transpose(m_f + m_b, (0, 1, 4, 2, 3)).reshape(g, q * cg, q * cg)

    def state_in(d, t_idx):
        wr, wi = _cmul(pwr[d][..., t_idx][..., None], pwi[d][..., t_idx][..., None],
                       bbr[d][:, :, None, :], bbi[d][:, :, None, :])
        tr = lambda w: jnp.transpose(w, (0, 2, 3, 1)).reshape(g, q * cg, p)
        return tr(wr), tr(wi)

    wfr, wfi = state_in(0, q - 1 - ti)
    wbr, wbi = state_in(1, ti)
    w1 = jnp.concatenate([m_tot, wfr, wbr, wfi, wbi], axis=-1)

    def state_out(d, t_idx):
        vr, vi = _cmul(c_re[d][..., None], c_im[d][..., None],
                       pwr[d][:, None, :, :][..., t_idx], pwi[d][:, None, :, :][..., t_idx])
        tr = lambda w: jnp.transpose(w, (0, 2, 3, 1)).reshape(g, p, q * cg)
        return tr(vr), tr(vi)

    vfr, vfi = state_out(0, ti + 1)
    vbr, vbi = state_out(1, q - ti)
    zf = jnp.zeros_like(vfr)
    v = jnp.stack([jnp.concatenate([vfr, zf], axis=1), jnp.concatenate([-vfi, zf], axis=1),
                   jnp.concatenate([zf, vbr], axis=1), jnp.concatenate([zf, -vbi], axis=1)], axis=1)
    lam_pack = jnp.stack([jnp.concatenate([pwr[0][..., q], pwr[1][..., q]], axis=-1),
                          jnp.concatenate([pwi[0][..., q], pwi[1][..., q]], axis=-1)], axis=1)
    return w1.astype(BF16), v.astype(BF16), lam_pack


def _s5_rows(dm):
    rc = dm.lc // S5_Q * dm.bc
    return rc, rc + dm.ls // S5_Q * dm.bs


def _chunk_row_loops(dm, body):
    rc, _ = _s5_rows(dm)
    for tok0, row0, nc, b, seq in ((0, 0, dm.lc // S5_Q, dm.bc, dm.lc), (dm.tc, rc, dm.ls // S5_Q, dm.bs, dm.ls)):
        def step(n, c, tok0=tok0, row0=row0, b=b, seq=seq):
            body(tok0 + n * S5_Q, pl.multiple_of(row0 + n * b, b), b, seq)
            return c
        lax.fori_loop(0, nc, step, 0)


def _s5_pack_kernel(dm, u_ref, o_ref, a_ref, ab_ref):
    q, cg = S5_Q, S5_GROUP

    def gather(tok, row, b, seq):
        for i in range(q):
            a_ref[i, pl.ds(row, b), :] = u_ref[pl.ds(tok + i, b, stride=seq), :]

    _chunk_row_loops(dm, gather)
    ab_ref[...] = a_ref[...].astype(BF16)
    src = lax.broadcasted_iota(jnp.int32, (2 * LANES, q * cg), 0)
    dst = lax.broadcasted_iota(jnp.int32, (2 * LANES, q * cg), 1)
    for gl in range(LANES // cg):
        acc = None
        for i in range(0, q, 2):
            sel = jnp.where((src % LANES == gl * cg + dst % cg) & (dst // cg == i + src // LANES),
                            1.0, 0.0).astype(BF16)
            part = _dot(jnp.concatenate([ab_ref[i], ab_ref[i + 1]], axis=1), sel)
            acc = part if acc is None else acc + part
        o_ref[gl] = acc.astype(o_ref.dtype)


def _s5_unpack_kernel(dm, y_ref, o_ref, yt_ref):
    q, cg = S5_Q, S5_GROUP
    src = lax.broadcasted_iota(jnp.int32, (q * cg, 2 * LANES), 0)
    dst = lax.broadcasted_iota(jnp.int32, (q * cg, 2 * LANES), 1)
    for i in range(0, q, 2):
        acc = None
        for gl in range(LANES // cg):
            sel = jnp.where((src // cg == i + dst // LANES) & (src % cg == dst % cg)
                            & (dst % LANES // cg == gl), 1.0, 0.0).astype(BF16)
            part = _dot(y_ref[gl], sel)
            acc = part if acc is None else acc + part
        yt_ref[...] = acc

        def spread(tok, row, b, seq, i=i):
            o_ref[pl.ds(tok + i, b, stride=seq), :] = yt_ref[pl.ds(row, b), :LANES]
            o_ref[pl.ds(tok + i + 1, b, stride=seq), :] = yt_ref[pl.ds(row, b), LANES:]

        _chunk_row_loops(dm, spread)


def _s5_pack(dm, u_tok):
    t, w = u_tok.shape
    _, r = _s5_rows(dm)
    gps = LANES // S5_GROUP
    nq = S5_Q * S5_GROUP
    return pl.pallas_call(
        functools.partial(_s5_pack_kernel, dm),
        grid=(w // LANES,),
        in_specs=[pl.BlockSpec((t, LANES), lambda s: (0, s))],
        out_specs=pl.BlockSpec((gps, r, nq), lambda s: (s, 0, 0)),
        out_shape=jax.ShapeDtypeStruct((w // S5_GROUP, r, nq), BF16),
        scratch_shapes=[pltpu.VMEM((S5_Q, r, LANES), F32), pltpu.VMEM((S5_Q, r, LANES), BF16)],
        compiler_params=_cp(("parallel",)),
        name="s5_pack",
    )(u_tok)


def _s5_unpack(dm, y_g):
    g, r, nq = y_g.shape
    gps = LANES // S5_GROUP
    return pl.pallas_call(
        functools.partial(_s5_unpack_kernel, dm),
        grid=(g // gps,),
        in_specs=[pl.BlockSpec((gps, r, nq), lambda s: (s, 0, 0))],
        out_specs=pl.BlockSpec((dm.t, LANES), lambda s: (0, s)),
        out_shape=jax.ShapeDtypeStruct((dm.t, g * S5_GROUP), F32),
        scratch_shapes=[pltpu.VMEM((r, 2 * LANES), F32)],
        compiler_params=_cp(("parallel",)),
        name="s5_unpack",
    )(y_g)


def _s5(dm, u_tok, ops, d_skip, x0_re, x0_im):
    w1, v, lam_pack = ops
    g = w1.shape[0]
    p = lam_pack.shape[2] // 2
    assert 2 * p == S5_LANES
    q, cg = S5_Q, S5_GROUP
    ncc, ncs = dm.lc // q, dm.ls // q
    rc, r = _s5_rows(dm)
    u_g = _s5_pack(dm, u_tok)
    d_g =jnp.tile(d_skip.astype(F32).reshape(g, 1, cg), (1, q, 1)).reshape(g, 1, q * cg)
    x0 = jnp.stack([jnp.concatenate([x0_re[:, 0], x0_re[:, 1]], axis=-1),
                    jnp.concatenate([x0_im[:, 0], x0_im[:, 1]], axis=-1)], axis=0)
    x0 = x0.transpose(2, 0, 1, 3).astype(F32)
    nw = w1.shape[2]
    y_g, fin = pl.pallas_call(
        functools.partial(_s5_kernel, rc, ncc, dm.bc, ncs, dm.bs),
        grid=(g,),
        in_specs=[pl.BlockSpec((1, r, q * cg), lambda i: (i, 0, 0)),
                  pl.BlockSpec((1, q * cg, nw), lambda i: (i, 0, 0)),
                  pl.BlockSpec((1, 4, S5_LANES, q * cg), lambda i: (i, 0, 0, 0)),
                  pl.BlockSpec((1, 2, S5_LANES), lambda i: (i, 0, 0)),
                  pl.BlockSpec((1, 1, q * cg), lambda i: (i, 0, 0)),
                  pl.BlockSpec((1, 2, dm.bs, S5_LANES), lambda i: (i, 0, 0, 0))],
        out_specs=[pl.BlockSpec((1, r, q * cg), lambda i: (i, 0, 0)),
                   pl.BlockSpec((1, 2, dm.bc, S5_LANES), lambda i: (i, 0, 0, 0))],
        out_shape=[jax.ShapeDtypeStruct((g, r, q * cg), BF16),
                   jax.ShapeDtypeStruct((g, 2, dm.bc, S5_LANES), F32)],
        scratch_shapes=[pltpu.VMEM((r, S5_LANES), F32)] * 6,
        compiler_params=_cp(("parallel",)),
        name="s5",
    )(u_g, w1, v, lam_pack, d_g, x0)

    y_tok = _s5_unpack(dm, y_g)
    fin = fin.transpose(2, 1, 0, 3)
    fin_re = jnp.stack([fin[:, 0, :, :p], fin[:, 0, :, p:]], axis=1)
    fin_im = jnp.stack([fin[:, 1, :, :p], fin[:, 1, :, p:]], axis=1)
    return y_tok, fin_re, fin_im


def _glu_kernel(y_ref, w_ref, o_ref):
    z = _gelu_tanh(y_ref[...].astype(F32))
    o_ref[...] = (z * _sigmoid(_dot(z.astype(BF16), w_ref[...]))).astype(o_ref.dtype)


def _glu(dm, y, w):
    t, n = y.shape
    tm = dm.tm
    return pl.pallas_call(
        _glu_kernel,
        grid=(t // tm,),
        in_specs=[pl.BlockSpec((tm, n), lambda i: (i, 0)),
                  pl.BlockSpec((n, n), lambda i: (0, 0))],
        out_specs=pl.BlockSpec((tm, n), lambda i: (i, 0)),
        out_shape=jax.ShapeDtypeStruct((t, n), BF16),
        compiler_params=_cp(("parallel",)),
        name="s5_glu",
    )(y, w)


def _rot(x, ct, st):
    return x * ct + pltpu.roll(x, 64, 1) * st


def _mla_q_kernel(qscale, cq_ref, ckv_ref, kr_ref, qn_ref, kvn_ref, ct_ref, st_ref, w_ref,
                  q_ref, ckvn_ref, kro_ref):
    ct = ct_ref[...]
    st = st_ref[...]
    cq = cq_ref[...]
    cqn = (cq * lax.rsqrt(jnp.mean(cq * cq, axis=-1, keepdims=True) + EPS) * qn_ref[...]).astype(BF16)
    for h in range(MLA_HEADS):
        qh = _dot(cqn, w_ref[:, h * QK_PAD:(h + 1) * QK_PAD])
        q_ref[:, h * QK_PAD:h * QK_PAD + QK_NOPE] = (qh[:, :QK_NOPE] * qscale).astype(BF16)
        q_ref[:, h * QK_PAD + QK_NOPE:(h + 1) * QK_PAD] = (_rot(qh[:, QK_NOPE:], ct, st) * qscale).astype(BF16)
    ckv = ckv_ref[...]
    ckvn_ref[...] = ckv * lax.rsqrt(jnp.mean(ckv * ckv, axis=-1, keepdims=True) + EPS) * kvn_ref[...]
    kro_ref[...] = _rot(kr_ref[...], ct, st)


def _mla_q(dm, a, q_norm, kv_norm, ct, st, w_uq_ext, q_lora, kv_lora):
    t = a.shape[0]
    nq = w_uq_ext.shape[1]
    kr_blk = (q_lora + kv_lora) // 128
    tm = min(dm.tm, 512)
    qscale = float(QK_NOPE + QK_ROPE) ** -0.5 * math.log2(math.e)
    return pl.pallas_call(
        functools.partial(_mla_q_kernel, qscale),
        grid=(t // tm,),
        in_specs=[pl.BlockSpec((tm, q_lora), lambda i: (i, 0)),
                  pl.BlockSpec((tm, kv_lora), lambda i: (i, q_lora // kv_lora)),
                  pl.BlockSpec((tm, 128), lambda i: (i, kr_blk)),
                  pl.BlockSpec((1, q_lora), lambda i: (0, 0)),
                  pl.BlockSpec((1, kv_lora), lambda i: (0, 0)),
                  pl.BlockSpec((tm, 128), lambda i: (i, 0)),
                  pl.BlockSpec((tm, 128), lambda i: (i, 0)),
                  pl.BlockSpec((q_lora, nq), lambda i: (0, 0))],
        out_specs=[pl.BlockSpec((tm, nq), lambda i: (i, 0)),
                   pl.BlockSpec((tm, kv_lora), lambda i: (i, 0)),
                   pl.BlockSpec((tm, 128), lambda i: (i, 0))],
        out_shape=[jax.ShapeDtypeStruct((t, nq), BF16),
                   jax.ShapeDtypeStruct((t, kv_lora), F32),
                   jax.ShapeDtypeStruct((t, 128), F32)],
        compiler_params=_cp(("parallel",)),
        name="mla_q",
    )(a, a, a, q_norm, kv_norm, ct, st, w_uq_ext)


def _kv_expand_kernel(n_lat_tiles, per, cc_ref, ckr_ref, c_ref, kr_ref, wk_ref, wv_ref, k_ref, v_ref):
    t = pl.program_id(0)
    is_cache = jnp.logical_and(t < n_lat_tiles, t % per == 0)
    ckr = ckr_ref[0, 0].astype(F32)
    ckr = jnp.concatenate([ckr, jnp.zeros((ckr.shape[0], LANES - ckr.shape[1]), F32)], axis=1)
    c = jnp.where(is_cache, cc_ref[0, 0].astype(F32), c_ref[...]).astype(BF16)
    kr = jnp.where(is_cache, ckr, kr_ref[...]).astype(BF16)
    kn = _dot(c, wk_ref[...]).astype(BF16)
    vv = _dot(c, wv_ref[...]).astype(BF16)
    ones = jnp.ones((c.shape[0], V_PAD - V_HEAD), BF16)
    for h in range(MLA_HEADS):
        k_ref[:, h * QK_PAD:h * QK_PAD + QK_NOPE] = kn[:, h * QK_NOPE:(h + 1) * QK_NOPE]
        k_ref[:, h * QK_PAD + QK_NOPE:(h + 1) * QK_PAD] = kr
        v_ref[:, h * V_PAD:h * V_PAD + V_HEAD] = vv[:, h * V_HEAD:(h + 1) * V_HEAD]
        v_ref[:, h * V_PAD + V_HEAD:(h + 1) * V_PAD] = ones


def _kv_expand(dm, cache_ckv, cache_krope, j, ckv_n, kr, w_uk, w_uv):
    past, kv_lora = cache_ckv.shape[2:]
    tm = past
    assert dm.ls % tm == 0 and dm.tc % tm == 0
    per = 1 + dm.ls // tm
    n_lat = dm.bs * per
    nk = (n_lat + dm.tc // tm) * tm

    def own_blk(t):
        lat = dm.tc // tm + (t // per) * (dm.ls // tm) + jnp.maximum(t % per - 1, 0)
        return jnp.where(t < n_lat, lat, t - n_lat)

    def cache_idx(t):
        return (jnp.minimum(t // per, dm.bs - 1), j, 0, 0)

    return pl.pallas_call(
        functools.partial(_kv_expand_kernel, n_lat, per),
        grid=(nk // tm,),
        in_specs=[pl.BlockSpec((1, 1, tm, kv_lora), cache_idx),
                  pl.BlockSpec((1, 1, tm, cache_krope.shape[3]), cache_idx),
                  pl.BlockSpec((tm, kv_lora), lambda t: (own_blk(t), 0)),
                  pl.BlockSpec((tm, LANES), lambda t: (own_blk(t), 0)),
                  pl.BlockSpec(w_uk.shape, lambda t: (0, 0)),
                  pl.BlockSpec(w_uv.shape, lambda t: (0, 0))],
        out_specs=[pl.BlockSpec((tm, MLA_HEADS * QK_PAD), lambda t: (t, 0)),
                   pl.BlockSpec((tm, MLA_HEADS * V_PAD), lambda t: (t, 0))],
        out_shape=[jax.ShapeDtypeStruct((nk, MLA_HEADS * QK_PAD), BF16),
                   jax.ShapeDtypeStruct((nk, MLA_HEADS * V_PAD), BF16)],
        compiler_params=_cp(("parallel",)),
        name="mla_kv_expand",
    )(cache_ckv, cache_krope, ckv_n, kr, w_uk, w_uv)


def _attn_kernel(nh, nkb, q_ref, k_ref, v_ref, o_ref):
    kb = k_ref.shape[0] // nkb
    for h in range(nh):
        q = q_ref[:, h * QK_PAD:(h + 1) * QK_PAD]
        m, acc = None, None
        for j in range(nkb):
            s = _dot_nt(q, k_ref[j * kb:(j + 1) * kb, h * QK_PAD:(h + 1) * QK_PAD])
            mj = jnp.max(s, axis=-1, keepdims=True)
            m_new = mj if j == 0 else jnp.maximum(m, mj)
            pv = _dot(jnp.exp2(s - m_new).astype(BF16), v_ref[j * kb:(j + 1) * kb, h * V_PAD:(h + 1) * V_PAD])
            acc = pv if j == 0 else acc * jnp.exp2(m - m_new) + pv
            m = m_new
        o_ref[:, h * V_HEAD:(h + 1) * V_HEAD] = (acc[:, :V_HEAD] / acc[:, V_HEAD:V_HEAD + 1]).astype(o_ref.dtype)


def _attention(q, k, v, nb, lq, lk, tq, nh, q_row0, k_row0, name):
    nqt = lq // tq
    qb0, kb0 = q_row0 // tq, k_row0 // lk
    nkb = -(-lk // KB_MAX)
    assert lk % nkb == 0
    return pl.pallas_call(
        functools.partial(_attn_kernel, nh, nkb),
        grid=(nb, MLA_HEADS // nh, nqt),
        in_specs=[pl.BlockSpec((tq, nh * QK_PAD), lambda b, h, i: (qb0 + b * nqt + i, h)),
                  pl.BlockSpec((lk, nh * QK_PAD), lambda b, h, i: (kb0 + b, h)),
                  pl.BlockSpec((lk, nh * V_PAD), lambda b, h, i: (kb0 + b, h))],
        out_specs=pl.BlockSpec((tq, nh * V_HEAD), lambda b, h, i: (b * nqt + i, h)),
        out_shape=jax.ShapeDtypeStruct((nb * lq, MLA_HEADS * V_HEAD), BF16),
        compiler_params=_cp(("parallel", "parallel", "arbitrary")),
        name=name,
    )(q, k, v)


def _route_kernel(n_exp, x_ref, g_ref, sh_ref, sc_ref, rw_ref, rb_ref, tri_ref,
                  hp_ref, idx_ref, rank_ref, wt_ref, cnt_ref, run_ref):
    @pl.when(pl.program_id(0) == 0)
    def _():
        run_ref[...] = jnp.zeros_like(run_ref)

    h = _norm_mod(x_ref[...], g_ref[0], sc_ref[0], sh_ref[0])
    _store_row_tiles(hp_ref, _pack_halves(h))
    scores =_sigmoid(_dot_nt(rw_ref[...], h.astype(BF16)))
    biased = scores + rb_ref[...]
    per = n_exp // N_EXPERT_GROUPS
    assert per == 4
    rows_b = [biased[e:e + 1, :] for e in range(n_exp)]
    rows_s = [scores[e:e + 1, :] for e in range(n_exp)]
    best_sum, best_g = None, None
    for gi in range(N_EXPERT_GROUPS):
        a, b, c, d = rows_b[per * gi:per * gi + per]
        hi1, lo1 = jnp.maximum(a, b), jnp.minimum(a, b)
        hi2, lo2 = jnp.maximum(c, d), jnp.minimum(c, d)
        top2 = jnp.maximum(hi1, hi2) + jnp.maximum(jnp.minimum(hi1, hi2), jnp.maximum(lo1, lo2))
        if gi == 0:
            best_sum, best_g = top2, jnp.zeros_like(top2, dtype=jnp.int32)
        else:
            upd = top2 > best_sum
            best_sum = jnp.where(upd, top2, best_sum)
            best_g = jnp.where(upd, gi, best_g)
    vb, vs = [], []
    for k in range(per):
        accb, accs = rows_b[k], rows_s[k]
        for gi in range(1, N_EXPERT_GROUPS):
            sel = best_g == gi
            accb = jnp.where(sel, rows_b[per * gi + k], accb)
            accs = jnp.where(sel, rows_s[per * gi + k], accs)
        vb.append(accb)
        vs.append(accs)

    def first_argmax(vals, exclude):
        bv, bi, bs = None, None, None
        for k in range(per):
            v = vals[k] if exclude is None else jnp.where(exclude == k, -jnp.inf, vals[k])
            if k == 0:
                bv, bi, bs = v, jnp.zeros_like(best_g), vs[0]
            else:
                upd = v > bv
                bv = jnp.where(upd, v, bv)
                bi = jnp.where(upd, k, bi)
                bs = jnp.where(upd, vs[k], bs)
        return bi, bs

    i1, s1 = first_argmax(vb, None)
    i2, s2 = first_argmax(vb, i1)
    tot = s1 + s2
    e1 = best_g * per + i1
    e2 = best_g * per + i2
    idx_ref[0:1, :] = e1
    idx_ref[1:2, :] = e2
    wt_ref[0:1, :] = s1 / tot
    wt_ref[1:2, :] = s2 / tot
    eids = lax.broadcasted_iota(jnp.int32, (n_exp, e1.shape[1]), 0)
    hit1 = eids == e1
    hit2 = eids == e2
    oh1 = jnp.where(hit1, 1.0, 0.0)
    oh2 = jnp.where(hit2, 1.0, 0.0)
    p1 = _dot(oh1.astype(BF16), tri_ref[...])
    p2 = _dot(oh2.astype(BF16), tri_ref[...])
    c1 = jnp.sum(oh1, axis=1, keepdims=True)
    c2 = jnp.sum(oh2, axis=1, keepdims=True)
    run = run_ref[...]
    rank_ref[0:1, :] = jnp.sum(jnp.where(hit1, run + p1, 0.0), axis=0, keepdims=True).astype(jnp.int32)
    rank_ref[1:2, :] = jnp.sum(jnp.where(hit2, run + c1 + p2, 0.0), axis=0, keepdims=True).astype(jnp.int32)
    run_ref[...] = run + c1 + c2
    cnt_ref[...] = run + c1 + c2


def _route(dm, x, gain, mods, layer, rw_t, rb):
    t, d = x.shape
    n_exp = rw_t.shape[0]
    tm = dm.tm
    rt = d // 2 // LANES
    tri = (jnp.arange(tm)[:, None] < jnp.arange(tm)[None, :]).astype(BF16)
    return pl.pallas_call(
        functools.partial(_route_kernel, n_exp),
        grid=(t // tm,),
        in_specs=[pl.BlockSpec((tm, d), lambda i: (i, 0)),
                  pl.BlockSpec((1, 1, d), lambda i: (layer, 0, 0)),
                  pl.BlockSpec((1, 1, d), _mod_idx(dm, layer, 3, tm)),
                  pl.BlockSpec((1, 1, d), _mod_idx(dm, layer, 4, tm)),
                  pl.BlockSpec((n_exp, d), lambda i: (0, 0)),
                  pl.BlockSpec((n_exp, 1), lambda i: (0, 0)),
                  pl.BlockSpec((tm, tm), lambda i: (0, 0))],
        out_specs=[pl.BlockSpec((tm * rt, LANES), lambda i: (i, 0)),
                   pl.BlockSpec((2, tm), lambda i: (0, i)),
                   pl.BlockSpec((2, tm), lambda i: (0, i)),
                   pl.BlockSpec((2, tm), lambda i: (0, i)),
                   pl.BlockSpec((n_exp, 1), lambda i: (0, 0))],
        out_shape=[jax.ShapeDtypeStruct((t * rt, LANES), U32),
                   jax.ShapeDtypeStruct((2, t), jnp.int32),
                   jax.ShapeDtypeStruct((2, t), jnp.int32),
                   jax.ShapeDtypeStruct((2, t), F32),
                   jax.ShapeDtypeStruct((n_exp, 1), F32)],
        scratch_shapes=[pltpu.VMEM((n_exp, 1), F32)],
        compiler_params=_cp(("arbitrary",)),
        name="moe_route",
    )(x, gain, mods, mods, rw_t, rb, tri)


def _slot_plan(idx, rank, counts, n_exp, n_slots):
    counts = counts.reshape(n_exp).astype(jnp.int32)
    padded = ((counts + TM_MOE - 1) // TM_MOE) * TM_MOE
    ends = jnp.cumsum(padded)
    starts = ends - padded
    start_of = jnp.sum(jnp.where(idx[..., None] == jnp.arange(n_exp, dtype=jnp.int32), starts, 0), axis=-1)
    dest = (start_of + rank).reshape(-1).astype(jnp.int32)
    tile_start = jnp.arange(n_slots // TM_MOE, dtype=jnp.int32) * TM_MOE
    tile_exp = jnp.minimum(jnp.sum((ends[None, :] <= tile_start[:, None]).astype(jnp.int32), axis=1), n_exp - 1)
    tile_ok = (tile_start < ends[-1]).astype(jnp.int32)
    pad_lo = jnp.concatenate([starts + counts, ends[-1:]]).astype(jnp.int32)
    pad_n = (padded - counts).astype(jnp.int32)
    return dest, tile_exp.astype(jnp.int32), tile_ok, pad_lo, pad_n


def _scatter_kernel(n_exp, t_total, rows, rt, dest_ref, padlo_ref, padn_ref, hp_ref, xs_ref, zrow, sem):
    i = pl.program_id(0)
    base = i * rows
    tile_rows = TM_MOE * rt

    def slot(s):
        return xs_ref.at[pl.ds(pl.multiple_of(s * rt, rt), rt)]

    @pl.when(i == 0)
    def _():
        zrow[...] = jnp.zeros_like(zrow)
        for e in range(n_exp):
            lo = padlo_ref[e]

            def pad_copy(r, lo=lo):
                return pltpu.make_async_copy(zrow.at[pl.ds(0, rt)], slot(lo + r), sem)

            def start(r, c, pad_copy=pad_copy):
                pad_copy(r).start()
                return c

            def wait(r, c, pad_copy=pad_copy):
                pad_copy(r).wait()
                return c

            lax.fori_loop(0, padn_ref[e], start, 0)
            lax.fori_loop(0, padn_ref[e], wait, 0)

        tail0 = padlo_ref[n_exp]

        def tail_copy(k):
            row = pl.multiple_of((tail0 + k * TM_MOE) * rt, tile_rows)
            return pltpu.make_async_copy(zrow, xs_ref.at[pl.ds(row, tile_rows)], sem)

        def tail_start(k, c):
            tail_copy(k).start()
            return c

        def tail_wait(k, c):
            tail_copy(k).wait()
            return c

        n_tail = (xs_ref.shape[0] // rt - tail0) // TM_MOE
        lax.fori_loop(0, n_tail, tail_start, 0)
        lax.fori_loop(0, n_tail, tail_wait, 0)

    def copies(r):
        src = hp_ref.at[pl.ds(pl.multiple_of(r * rt, rt), rt)]
        return (pltpu.make_async_copy(src, slot(dest_ref[base + r]), sem),
                pltpu.make_async_copy(src, slot(dest_ref[t_total + base + r]), sem))

    def start(gi, c):
        for u in range(DMA_UNROLL):
            for prio, cp in enumerate(copies(gi * DMA_UNROLL + u)):
                cp.start(priority=prio)
        return c

    def wait(gi, c):
        for u in range(DMA_UNROLL):
            for cp in copies(gi * DMA_UNROLL + u):
                cp.wait()
        return c

    lax.fori_loop(0, rows // DMA_UNROLL, start, 0)
    lax.fori_loop(0, rows // DMA_UNROLL, wait, 0)


def _scatter_rows(hp, dest, pad_lo, pad_n, n_slots, rows, rt):
    t = hp.shape[0] // rt
    n_exp = pad_n.shape[0]
    return pl.pallas_call(
        functools.partial(_scatter_kernel, n_exp, t, rows, rt),
        grid_spec=pltpu.PrefetchScalarGridSpec(
            num_scalar_prefetch=3,
            grid=(t // rows,),
            in_specs=[pl.BlockSpec((rows * rt, LANES), lambda i, *_: (i, 0))],
            out_specs=pl.BlockSpec(memory_space=pl.ANY),
            scratch_shapes=[pltpu.VMEM((TM_MOE * rt, LANES), U32), pltpu.SemaphoreType.DMA(())]),
        out_shape=jax.ShapeDtypeStruct((n_slots * rt, LANES), U32),
        compiler_params=_cp(("arbitrary",)),
        name="moe_scatter",
    )(dest, pad_lo, pad_n, hp)


def _expert_kernel(rt, te_ref, ok_ref, x_ref, wg_ref, wu_ref, wd_ref, o_ref, wg_b, wu_b, wd_b):
    i = pl.program_id(0)
    new_expert = jnp.logical_or(i == 0, te_ref[i] != te_ref[jnp.maximum(i - 1, 0)])

    @pl.when(new_expert)
    def _():
        wg_b[...] = wg_ref[0, 0].astype(BF16)
        wu_b[...] = wu_ref[0, 0].astype(BF16)
        wd_b[...] = wd_ref[0, 0].astype(BF16)

    @pl.when(ok_ref[i] == 1)
    def _():
        x_lo, x_hi = _unpack_halves(_load_row_tiles(x_ref, rt))
        x_lo, x_hi = x_lo.astype(BF16), x_hi.astype(BF16)
        half = x_lo.shape[1]
        h1 = _dot(x_lo, wg_b[:half, :]) + _dot(x_hi, wg_b[half:, :])
        h2 = _dot(x_lo, wu_b[:half, :]) + _dot(x_hi, wu_b[half:, :])
        act = (_silu(h1) * h2).astype(BF16)
        _store_row_tiles(o_ref, _pack_halves(_dot(act, wd_b[...])))

    @pl.when(ok_ref[i] == 0)
    def _():
        o_ref[...] = jnp.zeros_like(o_ref)


def _experts(x_sorted, tile_exp, tile_ok, w_gate, w_up, w_down, layer, rt):
    n_slots = x_sorted.shape[0] // rt
    d, f = w_gate.shape[-2:]
    return pl.pallas_call(
        functools.partial(_expert_kernel, rt),
        grid_spec=pltpu.PrefetchScalarGridSpec(
            num_scalar_prefetch=2,
            grid=(n_slots // TM_MOE,),
            in_specs=[pl.BlockSpec((TM_MOE * rt, LANES), lambda i, te, ok: (i, 0)),
                      pl.BlockSpec((1, 1, d, f), lambda i, te, ok: (layer, te[i], 0, 0)),
                      pl.BlockSpec((1, 1, d, f), lambda i, te, ok: (layer, te[i], 0, 0)),
                      pl.BlockSpec((1, 1, f, d), lambda i, te, ok: (layer, te[i], 0, 0))],
            out_specs=pl.BlockSpec((TM_MOE * rt, LANES), lambda i, te, ok: (i, 0)),
            scratch_shapes=[pltpu.VMEM((d, f), BF16), pltpu.VMEM((d, f), BF16), pltpu.VMEM((f, d), BF16)]),
        out_shape=jax.ShapeDtypeStruct((n_slots * rt, LANES), U32),
        compiler_params=_cp(("arbitrary",)),
        name="moe_experts",
    )(tile_exp, tile_ok, x_sorted, w_gate, w_up, w_down)


def _combine_kernel(t_total, rows, rt, dest_ref, x_ref, gate_ref, wt_ref, y_ref, o_ref, y0, y1, sem):
    base = pl.program_id(0) * rows

    def slot(s):
        return y_ref.at[pl.ds(pl.multiple_of(s * rt, rt), rt)]

    def copies(r):
        dst = pl.ds(pl.multiple_of(r * rt, rt), rt)
        return (pltpu.make_async_copy(slot(dest_ref[base + r]), y0.at[dst], sem),
                pltpu.make_async_copy(slot(dest_ref[t_total + base + r]), y1.at[dst], sem))

    def start(gi, c):
        for u in range(DMA_UNROLL):
            for prio, cp in enumerate(copies(gi * DMA_UNROLL + u)):
                cp.start(priority=prio)
        return c

    def wait(gi, c):
        for u in range(DMA_UNROLL):
            for cp in copies(gi * DMA_UNROLL + u):
                cp.wait()
        return c

    lax.fori_loop(0, rows // DMA_UNROLL, start, 0)
    lax.fori_loop(0, rows // DMA_UNROLL, wait, 0)
    w0 = wt_ref[:, 0:1]
    w1 = wt_ref[:, 1:2]
    a_lo, a_hi = _unpack_halves(_load_row_tiles(y0, rt))
    b_lo, b_hi = _unpack_halves(_load_row_tiles(y1, rt))
    half = a_lo.shape[1]
    gate = gate_ref[0]
    o_ref[:, :half] = x_ref[:, :half] + gate[:, :half] * (w0 * a_lo + w1 * b_lo)
    o_ref[:, half:] = x_ref[:, half:] + gate[:, half:] * (w0 * a_hi + w1 * b_hi)


def _combine(dm, x, mods, layer, y_sorted, dest, wts_t):
    t, d = x.shape
    rows = min(dm.tm, 512)
    rt = d // 2 // LANES
    return pl.pallas_call(
        functools.partial(_combine_kernel, t, rows, rt),
        grid_spec=pltpu.PrefetchScalarGridSpec(
            num_scalar_prefetch=1,
            grid=(t // rows,),
            in_specs=[pl.BlockSpec((rows, d), lambda i, dst: (i, 0)),
                      pl.BlockSpec((1, 1, d), lambda i, dst: _mod_idx(dm, layer, 5, rows)(i)),
                      pl.BlockSpec((rows, 2), lambda i, dst: (i, 0)),
                      pl.BlockSpec(memory_space=pl.ANY)],
            out_specs=pl.BlockSpec((rows, d), lambda i, dst: (i, 0)),
            scratch_shapes=[pltpu.VMEM((rows * rt, LANES), U32), pltpu.VMEM((rows * rt, LANES), U32),
                            pltpu.SemaphoreType.DMA(())]),
        out_shape=jax.ShapeDtypeStruct((t, d), F32),
        compiler_params=_cp(("arbitrary",)),
        name="moe_combine",
    )(dest, x, mods, wts_t, y_sorted)


def _moe(dm, x, gain, mods, layer, rw_t, rb, w_gate, w_up, w_down):
    n_exp = rw_t.shape[0]
    hp, idx, rank, wts, counts = _route(dm, x, gain, mods, layer, rw_t, rb)
    n_slots = 2 * dm.t + n_exp * TM_MOE
    dest, tile_exp, tile_ok, pad_lo, pad_n = _slot_plan(idx, rank, counts, n_exp, n_slots)
    rt = dm.d // 2 // LANES
    x_sorted = _scatter_rows(hp, dest, pad_lo, pad_n, n_slots, min(dm.tm, 512), rt)
    y_sorted = _experts(x_sorted, tile_exp, tile_ok, w_gate, w_up, w_down, layer, rt)
    return _combine(dm, x, mods, layer, y_sorted, dest, wts.T)


def _final_norm_kernel(x_ref, g_ref, o_ref):
    x = x_ref[...]
    o_ref[...] = x * lax.rsqrt(jnp.mean(x * x, axis=-1, keepdims=True) + EPS) * g_ref[...]


def _final_norm(dm, x, gain, row0, rows):
    d = x.shape[1]
    tm = dm.tm
    blk0 = row0 // tm
    return pl.pallas_call(
        _final_norm_kernel,
        grid=(rows // tm,),
        in_specs=[pl.BlockSpec((tm, d), lambda i: (blk0 + i, 0)),
                  pl.BlockSpec((1, d), lambda i: (0, 0))],
        out_specs=pl.BlockSpec((tm, d), lambda i: (i, 0)),
        out_shape=jax.ShapeDtypeStruct((rows, d), F32),
        compiler_params=_cp(("parallel",)),
        name="final_norm",
    )(x, gain.reshape(1, d))


def _rope_tables(dm):
    rows = dm.ls // GRID_W
    row = jnp.repeat(jnp.arange(rows, dtype=F32), GRID_W)
    col = jnp.tile(jnp.arange(GRID_W, dtype=F32), rows)
    half = QK_ROPE // 2
    freqs = jnp.power(ROPE_BASE, -jnp.arange(0, half, 2, dtype=F32) / half)
    ar, ac = row[:, None] * freqs, col[:, None] * freqs
    zeros = jnp.zeros((dm.ls, 128 - QK_ROPE), F32)
    ct = jnp.concatenate([jnp.cos(ar), jnp.cos(ar), jnp.cos(ac), jnp.cos(ac), zeros], axis=-1)
    st = jnp.concatenate([-jnp.sin(ar), jnp.sin(ar), -jnp.sin(ac), jnp.sin(ac), zeros], axis=-1)
    ct_c = jnp.concatenate([jnp.ones((dm.tc, QK_ROPE), F32), jnp.zeros((dm.tc, 128 - QK_ROPE), F32)], axis=-1)
    ct = jnp.concatenate([ct_c, jnp.tile(ct, (dm.bs, 1))], axis=0)
    st = jnp.concatenate([jnp.zeros((dm.tc, 128), F32), jnp.tile(st, (dm.bs, 1))], axis=0)
    return ct, st


def _swap_halves_cols(w):
    qt = QK_ROPE // 4
    return jnp.concatenate([w[..., qt:2 * qt], w[..., :qt], w[..., 3 * qt:], w[..., 2 * qt:3 * qt]], axis=-1)


def kernel(x_prompt, x_sample, c, state_ret, state_s5_re, state_s5_im, cache_ckv, cache_krope, c_ctx, ada_w, ada_b, norm_mix, norm_ffn, norm_final, even_w_in, even_w_out, ret_decay, s5_a_re, s5_a_im, s5_log_dt, s5_b_re, s5_b_im, s5_c_re, s5_c_im, s5_d, s5_w_glu, mla_w_in, mla_q_norm, mla_w_uq, mla_kv_norm, mla_w_ukv, mla_w_out, router_w, router_bias, moe_w_gate, moe_w_up, moe_w_down):
    dm = _Dims(x_prompt, x_sample)
    d = dm.d
    depth = ada_w.shape[0]
    n_exp = router_w.shape[1]
    past = cache_ckv.shape[2]
    q_lora = mla_q_norm.shape[1]
    kv_lora = mla_kv_norm.shape[1]

    x = jnp.concatenate([x_prompt.reshape(dm.tc, d), x_sample.reshape(dm.ts, d)], axis=0)
    cond = jnp.zeros((dm.rp, d), F32).at[0].set(c_ctx).at[1:1 + dm.bs].set(c)
    mods = _adaln(cond, ada_w, ada_b, tn=d * N_MOD // 8).reshape(depth * dm.rp * N_MOD, 1, d)
    g_mix = norm_mix.reshape(depth, 1, d)
    g_ffn = norm_ffn.reshape(depth, 1, d)
    rw_t = router_w.T.astype(BF16)
    rb = router_bias.astype(F32).reshape(n_exp, 1)
    ct, st = _rope_tables(dm)
    s5_ops = jax.vmap(_s5_operators)(s5_a_re, s5_a_im, s5_log_dt, s5_b_re, s5_b_im, s5_c_re, s5_c_im)

    rets, s5r, s5i, ckvs, krs = [], [], [], [], []
    for layer in range(depth):
        if layer % 2 == 0:
            i = layer // 2
            ret_w = even_w_out.shape[1] - s5_d.shape[1]
            s5_w = s5_d.shape[1]
            z, u = _norm_mod_matmul_split(dm, x, g_mix, mods, layer, 0, even_w_in[i].astype(BF16), tn=s5_w,
                                          name="even_in_proj")
            log_gamma = -jnp.exp(ret_decay[i].astype(F32))
            ro_c, sfin = _retention(z, ret_w // RET_HEADS, log_gamma, None, dm.bc, dm.lc, 0, "retention_ctx")
            ro_s, _ = _retention(z, ret_w // RET_HEADS, log_gamma, (state_ret, i), dm.bs, dm.ls,
                                 dm.tc // dm.ls, "retention_lat")
            ops = tuple(op[i] for op in s5_ops)
            y, f_re, f_im = _s5(dm, u, ops, s5_d[i], state_s5_re[:, i], state_s5_im[:, i])
            s5_out = _glu(dm, y, s5_w_glu[i].astype(BF16))
            w_out = even_w_out[i].astype(BF16)
            x = _matmul_residual(dm, x, mods, layer, 2, [(ro_c, ro_s), s5_out], [w_out[:ret_w], w_out[ret_w:]],
                                 tn=d // 2, name="even_out_proj")
            rets.append(sfin)
            s5r.append(f_re)
            s5i.append(f_im)
        else:
            j = layer // 2
            w_in = mla_w_in[j]
            w_in_ext = jnp.concatenate([w_in, _swap_halves_cols(w_in[:, q_lora + kv_lora:])], axis=1).astype(BF16)
            a = _norm_mod_matmul(dm, x, g_mix, mods, layer, 0, w_in_ext, F32, tn=w_in_ext.shape[1],
                                 name="mla_in_proj")
            w_uq = mla_w_uq[j].reshape(q_lora, MLA_HEADS, QK_NOPE + QK_ROPE)
            w_uq_ext = jnp.concatenate([w_uq, _swap_halves_cols(w_uq[..., QK_NOPE:])], axis=-1)
            w_uq_ext = w_uq_ext.reshape(q_lora, MLA_HEADS * QK_PAD).astype(BF16)
            q, ckv_n, kr = _mla_q(dm, a, mla_q_norm[j].reshape(1, q_lora), mla_kv_norm[j].reshape(1, kv_lora),
                                  ct, st, w_uq_ext, q_lora, kv_lora)
            lk = past + dm.ls
            w_ukv = mla_w_ukv[j].reshape(kv_lora, MLA_HEADS, QK_NOPE + V_HEAD)
            w_uk = w_ukv[..., :QK_NOPE].reshape(kv_lora, MLA_HEADS * QK_NOPE).astype(BF16)
            w_uv = w_ukv[..., QK_NOPE:].reshape(kv_lora, MLA_HEADS * V_HEAD).astype(BF16)
            k_all, v_all = _kv_expand(dm, cache_ckv, cache_krope, j, ckv_n, kr, w_uk, w_uv)
            o_c = _attention(q, k_all, v_all, dm.bc, dm.lc, dm.lc, dm.lc, MLA_HEADS, 0, dm.bs * lk, "attn_ctx")
            o_s = _attention(q, k_all, v_all, dm.bs, dm.ls, lk, min(TQ_MAX, dm.ls), LAT_HEADS_PER_STEP, dm.tc, 0,
                             "attn_lat")
            x = _matmul_residual(dm, x, mods, layer, 2, [(o_c, o_s)], [mla_w_out[j].astype(BF16)], tn=d // 2,
                                 name="mla_out_proj")
            ckvs.append(ckv_n[:dm.tc].reshape(dm.bc, dm.lc, kv_lora))
            krs.append(kr[:dm.tc, :QK_ROPE].reshape(dm.bc, dm.lc, QK_ROPE))
        x = _moe(dm, x, g_ffn, mods, layer, rw_t, rb, moe_w_gate, moe_w_up, moe_w_down)

    y_prompt = _final_norm(dm, x, norm_final, 0, dm.tc).reshape(dm.bc, dm.lc, d)
    y_sample = _final_norm(dm, x, norm_final, dm.tc, dm.ts).reshape(dm.bs, dm.ls, d)
    return (y_prompt, y_sample, jnp.stack(rets, axis=1), jnp.stack(s5r, axis=1), jnp.stack(s5i, axis=1),
            jnp.stack(ckvs, axis=1), jnp.stack(krs, axis=1))
```

```python
import functools
import math

import jax
import jax.numpy as jnp
import numpy as np
from jax import lax
from jax.experimental import pallas as pl
from jax.experimental.pallas import tpu as pltpu

F32 = jnp.float32
BF16 = jnp.bfloat16
U32 = jnp.uint32
EPS = 1e-6

RET_HEADS = 4
RET_CHUNK = 128
RET_UNROLL = 4
S5_GROUP = 16
S5_Q = 16
S5_LANES = 128
LANES = 128
MLA_HEADS = 16
QK_NOPE = 128
QK_ROPE = 64
V_HEAD = 128
QK_PAD = 256
V_PAD = 256
GRID_W = 64
ROPE_BASE = 10000.0
N_EXPERT_GROUPS = 4
N_MOD = 6

VMEM_LIMIT = 56 * 1024 * 1024
TM_MAX = 1024
TQ_MAX = 512
KB_MAX = 768
LAT_HEADS_PER_STEP = 2
TM_MOE = 256
DMA_UNROLL = 8
HI_MASK = np.uint32(0xFFFF0000)


def _cp(sem, vmem=VMEM_LIMIT):
    return pltpu.CompilerParams(dimension_semantics=sem, vmem_limit_bytes=vmem)


def _sigmoid(x):
    return 1.0 / (1.0 + jnp.exp(-x))


def _silu(x):
    return x * _sigmoid(x)


def _gelu_tanh(x):
    return 0.5 * x * (1.0 + jnp.tanh(math.sqrt(2.0 / math.pi) * (x + 0.044715 * (x * x * x))))


def _dot(a, b):
    return jnp.dot(a, b, preferred_element_type=F32)


def _dot_nt(a, b):
    return lax.dot_general(a, b, (((1,), (1,)), ((), ())), preferred_element_type=F32)


def _dot_tn(a, b):
    return lax.dot_general(a, b, (((0,), (0,)), ((), ())), preferred_element_type=F32)


def _pack_halves(x):
    half = x.shape[1] // 2
    xb = x.astype(BF16).astype(F32)
    lo = lax.bitcast_convert_type(xb[:, :half], U32) >> 16
    hi = lax.bitcast_convert_type(xb[:, half:], U32) & HI_MASK
    return hi | lo


def _unpack_halves(w):
    return (lax.bitcast_convert_type(w << 16, F32), lax.bitcast_convert_type(w & HI_MASK, F32))


def _store_row_tiles(ref, w):
    r, rt = w.shape[0], w.shape[1] // LANES
    for s in range(rt):
        ref[pl.ds(s, r, stride=rt), :] = w[:, s * LANES:(s + 1) * LANES]


def _load_row_tiles(ref, rt):
    r = ref.shape[0] // rt
    return jnp.concatenate([ref[pl.ds(s, r, stride=rt), :] for s in range(rt)], axis=1)


class _Dims:
    def __init__(self, x_prompt, x_sample):
        self.bc, self.lc, self.d = x_prompt.shape
        self.bs, self.ls, _ = x_sample.shape
        self.tc = self.bc * self.lc
        self.ts = self.bs * self.ls
        self.t = self.tc + self.ts
        self.rp = -(-(1 + self.bs) // 8) * 8
        self.tm = min(TM_MAX, self.ls)
        assert self.tc % self.tm == 0 and self.ls % self.tm == 0

    def mod_row(self, i, tm):
        nct = self.tc // tm
        return jnp.where(i < nct, 0, 1 + (i - nct) // (self.ls // tm))


def _mod_idx(dm, layer, k, tm):
    def idx(i, *_):
        return ((layer * dm.rp + dm.mod_row(i, tm)) * N_MOD + k, 0, 0)
    return idx


def _adaln_kernel(c_ref, w_ref, b_ref, o_ref):
    cs = _silu(c_ref[...]).astype(BF16)
    o_ref[0] = _dot(cs, w_ref[0].astype(BF16)) + b_ref[0]


def _adaln(cond, ada_w, ada_b, tn):
    depth, d, n = ada_w.shape
    rp = cond.shape[0]
    return pl.pallas_call(
        _adaln_kernel,
        grid=(depth, n // tn),
        in_specs=[pl.BlockSpec((rp, d), lambda l, j: (0, 0)),
                  pl.BlockSpec((1, d, tn), lambda l, j: (l, 0, j)),
                  pl.BlockSpec((1, 1, tn), lambda l, j: (l, 0, j))],
        out_specs=pl.BlockSpec((1, rp, tn), lambda l, j: (l, 0, j)),
        out_shape=jax.ShapeDtypeStruct((depth, rp, n), F32),
        compiler_params=_cp(("parallel", "parallel")),
        name="adaln",
    )(cond, ada_w, ada_b.reshape(depth, 1, n))


def _norm_mod(x, g, sc, sh):
    y = x * lax.rsqrt(jnp.mean(x * x, axis=-1, keepdims=True) + EPS) * g
    return y * (1.0 + sc) + sh


def _nmm_kernel(x_ref, g_ref, sh_ref, sc_ref, w_ref, o_ref, hn_ref):
    @pl.when(pl.program_id(1) == 0)
    def _():
        hn_ref[...] = _norm_mod(x_ref[...], g_ref[0], sc_ref[0], sh_ref[0]).astype(BF16)

    o_ref[...] = _dot(hn_ref[...], w_ref[...]).astype(o_ref.dtype)


def _norm_mod_matmul(dm, x, gain, mods, layer, k_shift, w, out_dtype, tn, name):
    t, d = x.shape
    n = w.shape[1]
    tm = dm.tm
    return pl.pallas_call(
        _nmm_kernel,
        grid=(t // tm, n // tn),
        in_specs=[pl.BlockSpec((tm, d), lambda i, j: (i, 0)),
                  pl.BlockSpec((1, 1, d), lambda i, j: (layer, 0, 0)),
                  pl.BlockSpec((1, 1, d), _mod_idx(dm, layer, k_shift, tm)),
                  pl.BlockSpec((1, 1, d), _mod_idx(dm, layer, k_shift + 1, tm)),
                  pl.BlockSpec((d, tn), lambda i, j: (0, j))],
        out_specs=pl.BlockSpec((tm, tn), lambda i, j: (i, j)),
        out_shape=jax.ShapeDtypeStruct((t, n), out_dtype),
        scratch_shapes=[pltpu.VMEM((tm, d), BF16)],
        compiler_params=_cp(("parallel", "arbitrary")),
        name=name,
    )(x, gain, mods, mods, w)


def _nmm_split_kernel(n_main, x_ref, g_ref, sh_ref, sc_ref, w_ref, o_main_ref, o_tail_ref, hn_ref):
    j = pl.program_id(1)

    @pl.when(j == 0)
    def _():
        hn_ref[...] = _norm_mod(x_ref[...], g_ref[0], sc_ref[0], sh_ref[0]).astype(BF16)

    acc = _dot(hn_ref[...], w_ref[...])

    @pl.when(j < n_main)
    def _():
        o_main_ref[...] = acc.astype(o_main_ref.dtype)

    @pl.when(j >= n_main)
    def _():
        o_tail_ref[...] = acc


def _norm_mod_matmul_split(dm, x, gain, mods, layer, k_shift, w, tn, name):
    t, d = x.shape
    n = w.shape[1]
    tm = dm.tm
    n_main = n // tn - 1
    return pl.pallas_call(
        functools.partial(_nmm_split_kernel, n_main),
        grid=(t // tm, n // tn),
        in_specs=[pl.BlockSpec((tm, d), lambda i, j: (i, 0)),
                  pl.BlockSpec((1, 1, d), lambda i, j: (layer, 0, 0)),
                  pl.BlockSpec((1, 1, d), _mod_idx(dm, layer, k_shift, tm)),
                  pl.BlockSpec((1, 1, d), _mod_idx(dm, layer, k_shift + 1, tm)),
                  pl.BlockSpec((d, tn), lambda i, j: (0, j))],
        out_specs=[pl.BlockSpec((tm, tn), lambda i, j: (i, jnp.minimum(j, n_main - 1))),
                   pl.BlockSpec((tm, tn), lambda i, j: (i, 0))],
        out_shape=[jax.ShapeDtypeStruct((t, n_main * tn), BF16),
                   jax.ShapeDtypeStruct((t, tn), F32)],
        scratch_shapes=[pltpu.VMEM((tm, d), BF16)],
        compiler_params=_cp(("parallel", "arbitrary")),
        name=name,
    )(x, gain, mods, mods, w)


def _mmres_kernel(split, nct, x_ref, gate_ref, *refs):
    n_a = sum(2 if sp else 1 for sp in split)
    a_refs, w_refs, o_ref = refs[:n_a], refs[n_a:n_a + len(split)], refs[n_a + len(split)]

    def run(use_ctx):
        acc, pos = None, 0
        for sp, w_ref in zip(split, w_refs):
            a_ref = a_refs[pos if (use_ctx or not sp) else pos + 1]
            pos += 2 if sp else 1
            part = _dot(a_ref[...], w_ref[...])
            acc = part if acc is None else acc + part
        o_ref[...] = x_ref[...] + gate_ref[0] * acc

    if any(split):
        is_ctx = pl.program_id(0) < nct
        pl.when(is_ctx)(lambda: run(True))
        pl.when(jnp.logical_not(is_ctx))(lambda: run(False))
    else:
        run(True)


def _matmul_residual(dm, x, mods, layer, k_gate, acts, ws, tn, name):
    t, d = x.shape
    tm = dm.tm
    nct = dm.tc // tm
    split = tuple(isinstance(a, tuple) for a in acts)

    def gate_idx(i, j):
        return ((layer * dm.rp + dm.mod_row(i, tm)) * N_MOD + k_gate, 0, j)

    in_specs = [pl.BlockSpec((tm, tn), lambda i, j: (i, j)),
                pl.BlockSpec((1, 1, tn), gate_idx)]
    flat = []
    for a in acts:
        if isinstance(a, tuple):
            in_specs.append(pl.BlockSpec((tm, a[0].shape[1]), lambda i, j: (jnp.minimum(i, nct - 1), 0)))
            in_specs.append(pl.BlockSpec((tm, a[1].shape[1]), lambda i, j: (jnp.maximum(i - nct, 0), 0)))
            flat += list(a)
        else:
            in_specs.append(pl.BlockSpec((tm, a.shape[1]), lambda i, j: (i, 0)))
            flat.append(a)
    in_specs += [pl.BlockSpec((w.shape[0], tn), lambda i, j: (0, j)) for w in ws]
    return pl.pallas_call(
        functools.partial(_mmres_kernel, split, nct),
        grid=(t // tm, d // tn),
        in_specs=in_specs,
        out_specs=pl.BlockSpec((tm, tn), lambda i, j: (i, j)),
        out_shape=jax.ShapeDtypeStruct((t, d), F32),
        compiler_params=_cp(("parallel", "parallel")),
        name=name,
    )(x, mods, *flat, *ws)


def _ret_kernel(has_s0, nch, scale, lg_ref, q_ref, k_ref, v_ref, g_ref, *refs):
    if has_s0:
        s0_ref, o_ref, sfin_ref, oacc_ref, st_ref = refs
    else:
        o_ref, sfin_ref, oacc_ref, st_ref = refs
    c = RET_CHUNK
    h = pl.program_id(1)
    lgf = lg_ref[0, h]
    lgb = lg_ref[1, h]
    dv = v_ref.shape[1]

    ii = lax.broadcasted_iota(jnp.int32, (c, c), 0).astype(F32)
    jj = lax.broadcasted_iota(jnp.int32, (c, c), 1).astype(F32)
    diff = ii - jj
    dtot = (jnp.where(diff >= 0, jnp.exp(lgf * jnp.maximum(diff, 0.0)), 0.0)
            + jnp.where(diff <= 0, jnp.exp(lgb * jnp.maximum(-diff, 0.0)), 0.0)) * scale
    pos = lax.broadcasted_iota(jnp.int32, (c, 1), 0).astype(F32)
    qdec_f = jnp.exp(lgf * (pos + 1.0)) * scale
    kdec_f = jnp.exp(lgf * (c - 1.0 - pos))
    qdec_b = jnp.exp(lgb * (c - pos)) * scale
    kdec_b = jnp.exp(lgb * pos)
    cdec_f = jnp.exp(jnp.full((1, dv), lgf * c, F32))
    cdec_b = jnp.exp(jnp.full((1, dv), lgb * c, F32))

    if has_s0:
        st_ref[...] = s0_ref[0, 0, 0, 0].astype(F32)
    else:
        st_ref[...] = jnp.zeros_like(st_ref)

    def fwd(n, carry):
        r = pl.multiple_of(n * c, c)
        qn = q_ref[pl.ds(r, c), :]
        kn = k_ref[pl.ds(r, c), :]
        vn = v_ref[pl.ds(r, c), :]
        p = (_dot_nt(qn, kn) * dtot).astype(BF16)
        o = _dot(p, vn)
        o += _dot((qn.astype(F32) * qdec_f).astype(BF16), st_ref[...].astype(BF16))
        oacc_ref[pl.ds(r, c), :] = o
        kd = (kn.astype(F32) * kdec_f).astype(BF16)
        st_ref[...] = st_ref[...] * cdec_f + _dot_tn(kd, vn)
        return carry

    lax.fori_loop(0, nch, fwd, 0, unroll=min(nch, RET_UNROLL))
    sfin_ref[0, 0, 0] = st_ref[...]

    if has_s0:
        st_ref[...] = s0_ref[0, 0, 1, 0].astype(F32)
    else:
        st_ref[...] = jnp.zeros_like(st_ref)

    def bwd(m, carry):
        r = pl.multiple_of((nch - 1 - m) * c, c)
        qn = q_ref[pl.ds(r, c), :]
        kn = k_ref[pl.ds(r, c), :]
        vn = v_ref[pl.ds(r, c), :]
        oacc_ref[pl.ds(r, c), :] += _dot((qn.astype(F32) * qdec_b).astype(BF16), st_ref[...].astype(BF16))
        kd = (kn.astype(F32) * kdec_b).astype(BF16)
        st_ref[...] = st_ref[...] * cdec_b + _dot_tn(kd, vn)
        return carry

    lax.fori_loop(0, nch, bwd, 0, unroll=min(nch, RET_UNROLL))
    sfin_ref[0, 1, 0] = st_ref[...]

    o = oacc_ref[...]
    o = o * lax.rsqrt(jnp.mean(o * o, axis=-1, keepdims=True) + EPS)
    o_ref[...] = (_silu(g_ref[...].astype(F32)) * o).astype(o_ref.dtype)


def _retention(z, dk, log_gamma, s0, nb, seq, row_blk0, name):
    hh = RET_HEADS
    dv = dk
    nch = seq // RET_CHUNK
    has_s0 = s0 is not None
    scale = float(dk) ** -0.5

    def col(off):
        return lambda b, h: (row_blk0 + b, off + h)

    in_specs = [pl.BlockSpec(memory_space=pltpu.SMEM),
                pl.BlockSpec((seq, dk), col(0)),
                pl.BlockSpec((seq, dk), col(hh)),
                pl.BlockSpec((seq, dv), col(2 * hh)),
                pl.BlockSpec((seq, dv), col(3 * hh))]
    args = [log_gamma, z, z, z, z]
    if has_s0:
        s0_all, s0_layer = s0
        in_specs.append(pl.BlockSpec((1, 1, 2, 1, dk, dv), lambda b, h: (b, s0_layer, 0, h, 0, 0)))
        args.append(s0_all)
    return pl.pallas_call(
        functools.partial(_ret_kernel, has_s0, nch, scale),
        grid=(nb, hh),
        in_specs=in_specs,
        out_specs=[pl.BlockSpec((seq, dv), lambda b, h: (b, h)),
                   pl.BlockSpec((1, 2, 1, dk, dv), lambda b, h: (b, 0, h, 0, 0))],
        out_shape=[jax.ShapeDtypeStruct((nb * seq, hh * dv), BF16),
                   jax.ShapeDtypeStruct((nb, 2, hh, dk, dv), F32)],
        scratch_shapes=[pltpu.VMEM((seq, dv), F32), pltpu.VMEM((dk, dv), F32)],
        compiler_params=_cp(("parallel", "parallel")),
        name=name,
    )(*args)


def _s5_kernel(rc, ncc, bc, ncs, bs, u_ref, strip_ref, wst_ref, v_ref, lam_ref, d_ref, x0_ref,
               y_ref, fin_ref, mt_ref, sre, sim, are, aim, bre, bim):
    u = u_ref[0]
    q, cg = S5_Q, S5_GROUP
    strip = strip_ref[0]
    for j in range(q):
        off = (q - 1 - j) * cg
        mt_ref[j * cg:(j + 1) * cg, :] = strip[:, off:off + q * cg].astype(BF16)
    z = _dot(u, wst_ref[0])
    y0 = _dot(u, mt_ref[...]) + d_ref[0] * u.astype(F32)
    sre[...] = z[:, :S5_LANES]
    sim[...] = z[:, S5_LANES:2 * S5_LANES]
    lr = lam_ref[0, 0:1, :]
    li = lam_ref[0, 1:2, :]

    def scan(row0, nc, b, xr0, xi0):
        is_f = lax.broadcasted_iota(jnp.int32, (b, S5_LANES), 1) < S5_LANES // 2

        def body(s, carry):
            xr, xi = carry
            rf = pl.multiple_of(row0 + s * b, b)
            rb = pl.multiple_of(row0 + (nc - 1 - s) * b, b)
            are[pl.ds(rf, b), :] = xr
            aim[pl.ds(rf, b), :] = xi
            bre[pl.ds(rb, b), :] = xr
            bim[pl.ds(rb, b), :] = xi
            sr = jnp.where(is_f, sre[pl.ds(rf, b), :], sre[pl.ds(rb, b), :])
            si = jnp.where(is_f, sim[pl.ds(rf, b), :], sim[pl.ds(rb, b), :])
            return xr * lr - xi * li + sr, xi * lr + xr * li + si

        return lax.fori_loop(0, nc, body, (xr0, xi0))

    zero = jnp.zeros((bc, S5_LANES), F32)
    fr, fi = scan(0, ncc, bc, zero, zero)
    fin_ref[0, 0] = fr
    fin_ref[0, 1] = fi
    scan(rc, ncs, bs, x0_ref[0, 0], x0_ref[0, 1])
    y_ref[0] = (y0 + _dot(are[...].astype(BF16), v_ref[0, 0]) + _dot(aim[...].astype(BF16), v_ref[0, 1])
                + _dot(bre[...].astype(BF16), v_ref[0, 2]) + _dot(bim[...].astype(BF16), v_ref[0, 3])
                ).astype(y_ref.dtype)


def _cmul(ar, ai, br, bi):
    return ar * br - ai * bi, ar * bi + ai * br


def _cexp(re, im):
    e = jnp.exp(re)
    return e * jnp.cos(im), e * jnp.sin(im)


def _s5_operators(a_re, a_im, log_dt, b_re, b_im, c_re, c_im):
    q = S5_Q
    hp = lax.Precision.HIGHEST
    a_re, a_im = a_re.astype(F32), a_im.astype(F32)
    dt = jnp.exp(log_dt.astype(F32))[..., None]
    ldr, ldi = a_re * dt, a_im * dt
    lbr, lbi = _cexp(ldr, ldi)
    den = a_re * a_re + a_im * a_im
    fr = ((lbr - 1.0) * a_re + lbi * a_im) / den
    fi = (lbi * a_re - (lbr - 1.0) * a_im) / den
    bbr, bbi = _cmul(fr[..., None], fi[..., None], b_re.astype(F32), b_im.astype(F32))
    c_re, c_im = c_re.astype(F32), c_im.astype(F32)
    g, p = a_re.shape[1], a_re.shape[2]
    cg = bbr.shape[-1]
    steps = jnp.arange(q + 1, dtype=F32)
    pwr, pwi = _cexp(ldr[..., None] * steps, ldi[..., None] * steps)
    c_pr, c_pi = jnp.swapaxes(c_re, 2, 3), jnp.swapaxes(c_im, 2, 3)
    cpr, cpi = _cmul(c_pr[:, :, :, None, :], c_pi[:, :, :, None, :],
                     pwr[..., :q, None], pwi[..., :q, None])
    cpr, cpi = cpr.reshape(2, g, p, q * cg), cpi.reshape(2, g, p, q * cg)
    kern = (jnp.einsum('dgpe,dgpx->dgex', bbr, cpr, precision=hp)
            - jnp.einsum('dgpe,dgpx->dgex', bbi, cpi, precision=hp)).reshape(2, g, cg, q, cg)
    strip = jnp.concatenate([kern[1][:, :, :0:-1], kern[0][:, :, :1] + kern[1][:, :, :1], kern[0][:, :, 1:],
                             jnp.zeros((g, cg, 1, cg), F32)], axis=2).reshape(g, cg, 2 * q * cg)
    ti = jnp.arange(q)

    def state_in(d, t_idx):
        wr, wi = _cmul(pwr[d][..., t_idx][..., None], pwi[d][..., t_idx][..., None],
                       bbr[d][:, :, None, :], bbi[d][:, :, None, :])
        tr = lambda w: jnp.transpose(w, (0, 2, 3, 1)).reshape(g, q * cg, p)
        return tr(wr), tr(wi)

    wfr, wfi = state_in(0, q - 1 - ti)
    wbr, wbi = state_in(1, ti)
    wst = jnp.concatenate([wfr, wbr, wfi, wbi], axis=-1)

    def state_out(d, t_idx):
        vr, vi = _cmul(c_re[d][..., None], c_im[d][..., None],
                       pwr[d][:, None, :, :][..., t_idx], pwi[d][:, None, :, :][..., t_idx])
        tr = lambda w: jnp.transpose(w, (0, 2, 3, 1)).reshape(g, p, q * cg)
        return tr(vr), tr(vi)

    vfr, vfi = state_out(0, ti + 1)
    vbr, vbi = state_out(1, q - ti)
    zf = jnp.zeros_like(vfr)
    v = jnp.stack([jnp.concatenate([vfr, zf], axis=1), jnp.concatenate([-vfi, zf], axis=1),
                   jnp.concatenate([zf, vbr], axis=1), jnp.concatenate([zf, -vbi], axis=1)], axis=1)
    lam_pack = jnp.stack([jnp.concatenate([pwr[0][..., q], pwr[1][..., q]], axis=-1),
                          jnp.concatenate([pwi[0][..., q], pwi[1][..., q]], axis=-1)], axis=1)
    return strip, wst.astype(BF16), v.astype(BF16), lam_pack


def _s5_rows(dm):
    rc = dm.lc // S5_Q * dm.bc
    return rc, rc + dm.ls // S5_Q * dm.bs


def _chunk_row_loops(dm, body):
    rc, _ = _s5_rows(dm)
    for tok0, row0, nc, b, seq in ((0, 0, dm.lc // S5_Q, dm.bc, dm.lc), (dm.tc, rc, dm.ls // S5_Q, dm.bs, dm.ls)):
        def step(n, c, tok0=tok0, row0=row0, b=b, seq=seq):
            body(tok0 + n * S5_Q, pl.multiple_of(row0 + n * b, b), b, seq)
            return c
        lax.fori_loop(0, nc, step, 0)


def _s5_pack_kernel(dm, u_ref, o_ref, a_ref, ab_ref):
    q, cg = S5_Q, S5_GROUP

    def gather(tok, row, b, seq):
        for i in range(q):
            a_ref[i, pl.ds(row, b), :] = u_ref[pl.ds(tok + i, b, stride=seq), :]

    _chunk_row_loops(dm, gather)
    ab_ref[...] = a_ref[...].astype(BF16)
    src = lax.broadcasted_iota(jnp.int32, (2 * LANES, q * cg), 0)
    dst = lax.broadcasted_iota(jnp.int32, (2 * LANES, q * cg), 1)
    for gl in range(LANES // cg):
        acc = None
        for i in range(0, q, 2):
            sel = jnp.where((src % LANES == gl * cg + dst % cg) & (dst // cg == i + src // LANES),
                            1.0, 0.0).astype(BF16)
            part = _dot(jnp.concatenate([ab_ref[i], ab_ref[i + 1]], axis=1), sel)
            acc = part if acc is None else acc + part
        o_ref[gl] = acc.astype(o_ref.dtype)


def _s5_unpack_kernel(dm, y_ref, o_ref, yt_ref):
    q, cg = S5_Q, S5_GROUP
    src = lax.broadcasted_iota(jnp.int32, (q * cg, 2 * LANES), 0)
    dst = lax.broadcasted_iota(jnp.int32, (q * cg, 2 * LANES), 1)
    for i in range(0, q, 2):
        acc = None
        for gl in range(LANES // cg):
            sel = jnp.where((src // cg == i + dst // LANES) & (src % cg == dst % cg)
                            & (dst % LANES // cg == gl), 1.0, 0.0).astype(BF16)
            part = _dot(y_ref[gl], sel)
            acc = part if acc is None else acc + part
        yt_ref[...] = acc

        def spread(tok, row, b, seq, i=i):
            o_ref[pl.ds(tok + i, b, stride=seq), :] = yt_ref[pl.ds(row, b), :LANES]
            o_ref[pl.ds(tok + i + 1, b, stride=seq), :] = yt_ref[pl.ds(row, b), LANES:]

        _chunk_row_loops(dm, spread)


def _s5_pack(dm, u_tok):
    t, w = u_tok.shape
    _, r = _s5_rows(dm)
    gps = LANES // S5_GROUP
    nq = S5_Q * S5_GROUP
    return pl.pallas_call(
        functools.partial(_s5_pack_kernel, dm),
        grid=(w // LANES,),
        in_specs=[pl.BlockSpec((t, LANES), lambda s: (0, s))],
        out_specs=pl.BlockSpec((gps, r, nq), lambda s: (s, 0, 0)),
        out_shape=jax.ShapeDtypeStruct((w // S5_GROUP, r, nq), BF16),
        scratch_shapes=[pltpu.VMEM((S5_Q, r, LANES), F32), pltpu.VMEM((S5_Q, r, LANES), BF16)],
        compiler_params=_cp(("parallel",)),
        name="s5_pack",
    )(u_tok)


def _s5_unpack(dm, y_g):
    g, r, nq = y_g.shape
    gps = LANES // S5_GROUP
    return pl.pallas_call(
        functools.partial(_s5_unpack_kernel, dm),
        grid=(g // gps,),
        in_specs=[pl.BlockSpec((gps, r, nq), lambda s: (s, 0, 0))],
        out_specs=pl.BlockSpec((dm.t, LANES), lambda s: (0, s)),
        out_shape=jax.ShapeDtypeStruct((dm.t, g * S5_GROUP), F32),
        scratch_shapes=[pltpu.VMEM((r, 2 * LANES), F32)],
        compiler_params=_cp(("parallel",)),
        name="s5_unpack",
    )(y_g)


def _s5(dm, u_tok, ops, d_skip, x0_re, x0_im):
    strip, wst, v, lam_pack = ops
    g = wst.shape[0]
    p = lam_pack.shape[2] // 2
    assert 2 * p == S5_LANES
    q, cg = S5_Q, S5_GROUP
    ncc, ncs = dm.lc // q, dm.ls // q
    rc, r = _s5_rows(dm)
    u_g = _s5_pack(dm, u_tok)
    d_g = jnp.tile(d_skip.astype(F32).reshape(g, 1, cg), (1, q, 1)).reshape(g, 1, q * cg)
    x0 = jnp.stack([jnp.concatenate([x0_re[:, 0], x0_re[:, 1]], axis=-1),
                    jnp.concatenate([x0_im[:, 0], x0_im[:, 1]], axis=-1)], axis=0)
    x0 = x0.transpose(2, 0, 1, 3).astype(F32)
    y_g, fin = pl.pallas_call(
        functools.partial(_s5_kernel, rc, ncc, dm.bc, ncs, dm.bs),
        grid=(g,),
        in_specs=[pl.BlockSpec((1, r, q * cg), lambda i: (i, 0, 0)),
                  pl.BlockSpec((1, cg, 2 * q * cg), lambda i: (i, 0, 0)),
                  pl.BlockSpec((1, q * cg, 2 * S5_LANES), lambda i: (i, 0, 0)),
                  pl.BlockSpec((1, 4, S5_LANES, q * cg), lambda i: (i, 0, 0, 0)),
                  pl.BlockSpec((1, 2, S5_LANES), lambda i: (i, 0, 0)),
                  pl.BlockSpec((1, 1, q * cg), lambda i: (i, 0, 0)),
                  pl.BlockSpec((1, 2, dm.bs, S5_LANES), lambda i: (i, 0, 0, 0))],
        out_specs=[pl.BlockSpec((1, r, q * cg), lambda i: (i, 0, 0)),
                   pl.BlockSpec((1, 2, dm.bc, S5_LANES), lambda i: (i, 0, 0, 0))],
        out_shape=[jax.ShapeDtypeStruct((g, r, q * cg), BF16),
                   jax.ShapeDtypeStruct((g, 2, dm.bc, S5_LANES), F32)],
        scratch_shapes=[pltpu.VMEM((q * cg, q * cg), BF16)] + [pltpu.VMEM((r, S5_LANES), F32)] * 6,
        compiler_params=_cp(("parallel",)),
        name="s5",
    )(u_g, strip, wst, v, lam_pack, d_g, x0)

    y_tok = _s5_unpack(dm, y_g)
    fin = fin.transpose(2, 1, 0, 3)
    fin_re = jnp.stack([fin[:, 0, :, :p], fin[:, 0, :, p:]], axis=1)
    fin_im = jnp.stack([fin[:, 1, :, :p], fin[:, 1, :, p:]], axis=1)
    return y_tok, fin_re, fin_im


def _glu_kernel(y_ref, w_ref, o_ref):
    z = _gelu_tanh(y_ref[...].astype(F32))
    o_ref[...] = (z * _sigmoid(_dot(z.astype(BF16), w_ref[...]))).astype(o_ref.dtype)


def _glu(dm, y, w):
    t, n = y.shape
    tm = dm.tm
    return pl.pallas_call(
        _glu_kernel,
        grid=(t // tm,),
        in_specs=[pl.BlockSpec((tm, n), lambda i: (i, 0)),
                  pl.BlockSpec((n, n), lambda i: (0, 0))],
        out_specs=pl.BlockSpec((tm, n), lambda i: (i, 0)),
        out_shape=jax.ShapeDtypeStruct((t, n), BF16),
        compiler_params=_cp(("parallel",)),
        name="s5_glu",
    )(y, w)


def _rot(x, ct, st):
    return x * ct + pltpu.roll(x, 64, 1) * st


def _mla_q_kernel(qscale, cq_ref, ckv_ref, kr_ref, qn_ref, kvn_ref, ct_ref, st_ref, w_ref,
                  q_ref, ckvn_ref, kro_ref):
    ct = ct_ref[...]
    st = st_ref[...]
    cq = cq_ref[...]
    cqn = (cq * lax.rsqrt(jnp.mean(cq * cq, axis=-1, keepdims=True) + EPS) * qn_ref[...]).astype(BF16)
    for h in range(MLA_HEADS):
        qh = _dot(cqn, w_ref[:, h * QK_PAD:(h + 1) * QK_PAD])
        q_ref[:, h * QK_PAD:h * QK_PAD + QK_NOPE] = (qh[:, :QK_NOPE] * qscale).astype(BF16)
        q_ref[:, h * QK_PAD + QK_NOPE:(h + 1) * QK_PAD] = (_rot(qh[:, QK_NOPE:], ct, st) * qscale).astype(BF16)
    ckv = ckv_ref[...]
    ckvn_ref[...] = ckv * lax.rsqrt(jnp.mean(ckv * ckv, axis=-1, keepdims=True) + EPS) * kvn_ref[...]
    kro_ref[...] = _rot(kr_ref[...], ct, st)


def _mla_q(dm, a, q_norm, kv_norm, ct, st, w_uq_ext, q_lora, kv_lora):
    t = a.shape[0]
    nq = w_uq_ext.shape[1]
    kr_blk = (q_lora + kv_lora) // 128
    tm = min(dm.tm, 512)
    qscale = float(QK_NOPE + QK_ROPE) ** -0.5 * math.log2(math.e)
    return pl.pallas_call(
        functools.partial(_mla_q_kernel, qscale),
        grid=(t // tm,),
        in_specs=[pl.BlockSpec((tm, q_lora), lambda i: (i, 0)),
                  pl.BlockSpec((tm, kv_lora), lambda i: (i, q_lora // kv_lora)),
                  pl.BlockSpec((tm, 128), lambda i: (i, kr_blk)),
                  pl.BlockSpec((1, q_lora), lambda i: (0, 0)),
                  pl.BlockSpec((1, kv_lora), lambda i: (0, 0)),
                  pl.BlockSpec((tm, 128), lambda i: (i, 0)),
                  pl.BlockSpec((tm, 128), lambda i: (i, 0)),
                  pl.BlockSpec((q_lora, nq), lambda i: (0, 0))],
        out_specs=[pl.BlockSpec((tm, nq), lambda i: (i, 0)),
                   pl.BlockSpec((tm, kv_lora), lambda i: (i, 0)),
                   pl.BlockSpec((tm, 128), lambda i: (i, 0))],
        out_shape=[jax.ShapeDtypeStruct((t, nq), BF16),
                   jax.ShapeDtypeStruct((t, kv_lora), F32),
                   jax.ShapeDtypeStruct((t, 128), F32)],
        compiler_params=_cp(("parallel",)),
        name="mla_q",
    )(a, a, a, q_norm, kv_norm, ct, st, w_uq_ext)


def _kv_expand_kernel(n_lat_tiles, per, cc_ref, ckr_ref, c_ref, kr_ref, wk_ref, wv_ref, k_ref, v_ref):
    t = pl.program_id(0)
    is_cache = jnp.logical_and(t < n_lat_tiles, t % per == 0)
    ckr = ckr_ref[0, 0].astype(F32)
    ckr = jnp.concatenate([ckr, jnp.zeros((ckr.shape[0], LANES - ckr.shape[1]), F32)], axis=1)
    c = jnp.where(is_cache, cc_ref[0, 0].astype(F32), c_ref[...]).astype(BF16)
    kr = jnp.where(is_cache, ckr, kr_ref[...]).astype(BF16)
    kn = _dot(c, wk_ref[...]).astype(BF16)
    vv = _dot(c, wv_ref[...]).astype(BF16)
    ones = jnp.ones((c.shape[0], V_PAD - V_HEAD), BF16)
    for h in range(MLA_HEADS):
        k_ref[:, h * QK_PAD:h * QK_PAD + QK_NOPE] = kn[:, h * QK_NOPE:(h + 1) * QK_NOPE]
        k_ref[:, h * QK_PAD + QK_NOPE:(h + 1) * QK_PAD] = kr
        v_ref[:, h * V_PAD:h * V_PAD + V_HEAD] = vv[:, h * V_HEAD:(h + 1) * V_HEAD]
        v_ref[:, h * V_PAD + V_HEAD:(h + 1) * V_PAD] = ones


def _kv_expand(dm, cache_ckv, cache_krope, j, ckv_n, kr, w_uk, w_uv):
    past, kv_lora = cache_ckv.shape[2:]
    tm = past
    assert dm.ls % tm == 0 and dm.tc % tm == 0
    per = 1 + dm.ls // tm
    n_lat = dm.bs * per
    nk = (n_lat + dm.tc // tm) * tm

    def own_blk(t):
        lat = dm.tc // tm + (t // per) * (dm.ls // tm) + jnp.maximum(t % per - 1, 0)
        return jnp.where(t < n_lat, lat, t - n_lat)

    def cache_idx(t):
        return (jnp.minimum(t // per, dm.bs - 1), j, 0, 0)

    return pl.pallas_call(
        functools.partial(_kv_expand_kernel, n_lat, per),
        grid=(nk // tm,),
        in_specs=[pl.BlockSpec((1, 1, tm, kv_lora), cache_idx),
                  pl.BlockSpec((1, 1, tm, cache_krope.shape[3]), cache_idx),
                  pl.BlockSpec((tm, kv_lora), lambda t: (own_blk(t), 0)),
                  pl.BlockSpec((tm, LANES), lambda t: (own_blk(t), 0)),
                  pl.BlockSpec(w_uk.shape, lambda t: (0, 0)),
                  pl.BlockSpec(w_uv.shape, lambda t: (0, 0))],
        out_specs=[pl.BlockSpec((tm, MLA_HEADS * QK_PAD), lambda t: (t, 0)),
                   pl.BlockSpec((tm, MLA_HEADS * V_PAD), lambda t: (t, 0))],
        out_shape=[jax.ShapeDtypeStruct((nk, MLA_HEADS * QK_PAD), BF16),
                   jax.ShapeDtypeStruct((nk, MLA_HEADS * V_PAD), BF16)],
        compiler_params=_cp(("parallel",)),
        name="mla_kv_expand",
    )(cache_ckv, cache_krope, ckv_n, kr, w_uk, w_uv)


def _attn_kernel(nh, nkb, q_ref, k_ref, v_ref, o_ref):
    kb = k_ref.shape[0] // nkb
    for h in range(nh):
        q = q_ref[:, h * QK_PAD:(h + 1) * QK_PAD]
        m, acc = None, None
        for j in range(nkb):
            s = _dot_nt(q, k_ref[j * kb:(j + 1) * kb, h * QK_PAD:(h + 1) * QK_PAD])
            mj = jnp.max(s, axis=-1, keepdims=True)
            m_new = mj if j == 0 else jnp.maximum(m, mj)
            pv = _dot(jnp.exp2(s - m_new).astype(BF16), v_ref[j * kb:(j + 1) * kb, h * V_PAD:(h + 1) * V_PAD])
            acc = pv if j == 0 else acc * jnp.exp2(m - m_new) + pv
            m = m_new
        o_ref[:, h * V_HEAD:(h + 1) * V_HEAD] = (acc[:, :V_HEAD] / acc[:, V_HEAD:V_HEAD + 1]).astype(o_ref.dtype)


def _attention(q, k, v, nb, lq, lk, tq, nh, q_row0, k_row0, name):
    nqt = lq // tq
    qb0, kb0 = q_row0 // tq, k_row0 // lk
    nkb = -(-lk // KB_MAX)
    assert lk % nkb == 0
    return pl.pallas_call(
        functools.partial(_attn_kernel, nh, nkb),
        grid=(nb, MLA_HEADS // nh, nqt),
        in_specs=[pl.BlockSpec((tq, nh * QK_PAD), lambda b, h, i: (qb0 + b * nqt + i, h)),
                  pl.BlockSpec((lk, nh * QK_PAD), lambda b, h, i: (kb0 + b, h)),
                  pl.BlockSpec((lk, nh * V_PAD), lambda b, h, i: (kb0 + b, h))],
        out_specs=pl.BlockSpec((tq, nh * V_HEAD), lambda b, h, i: (b * nqt + i, h)),
        out_shape=jax.ShapeDtypeStruct((nb * lq, MLA_HEADS * V_HEAD), BF16),
        compiler_params=_cp(("parallel", "parallel", "arbitrary")),
        name=name,
    )(q, k, v)


def _route_kernel(n_exp, x_ref, g_ref, sh_ref, sc_ref, rw_ref, rb_ref, tri_ref,
                  hp_ref, idx_ref, rank_ref, wt_ref, cnt_ref, run_ref):
    @pl.when(pl.program_id(0) == 0)
    def _():
        run_ref[...] = jnp.zeros_like(run_ref)

    h = _norm_mod(x_ref[...], g_ref[0], sc_ref[0], sh_ref[0])
    _store_row_tiles(hp_ref, _pack_halves(h))
    scores =_sigmoid(_dot_nt(rw_ref[...], h.astype(BF16)))
    biased = scores + rb_ref[...]
    per = n_exp // N_EXPERT_GROUPS
    assert per == 4
    rows_b = [biased[e:e + 1, :] for e in range(n_exp)]
    rows_s = [scores[e:e + 1, :] for e in range(n_exp)]
    best_sum, best_g = None, None
    for gi in range(N_EXPERT_GROUPS):
        a, b, c, d = rows_b[per * gi:per * gi + per]
        hi1, lo1 = jnp.maximum(a, b), jnp.minimum(a, b)
        hi2, lo2 = jnp.maximum(c, d), jnp.minimum(c, d)
        top2 = jnp.maximum(hi1, hi2) + jnp.maximum(jnp.minimum(hi1, hi2), jnp.maximum(lo1, lo2))
        if gi == 0:
            best_sum, best_g = top2, jnp.zeros_like(top2, dtype=jnp.int32)
        else:
            upd = top2 > best_sum
            best_sum = jnp.where(upd, top2, best_sum)
            best_g = jnp.where(upd, gi, best_g)
    vb, vs = [], []
    for k in range(per):
        accb, accs = rows_b[k], rows_s[k]
        for gi in range(1, N_EXPERT_GROUPS):
            sel = best_g == gi
            accb = jnp.where(sel, rows_b[per * gi + k], accb)
            accs = jnp.where(sel, rows_s[per * gi + k], accs)
        vb.append(accb)
        vs.append(accs)

    def first_argmax(vals, exclude):
        bv, bi, bs = None, None, None
        for k in range(per):
            v = vals[k] if exclude is None else jnp.where(exclude == k, -jnp.inf, vals[k])
            if k == 0:
                bv, bi, bs = v, jnp.zeros_like(best_g), vs[0]
            else:
                upd = v > bv
                bv = jnp.where(upd, v, bv)
                bi = jnp.where(upd, k, bi)
                bs = jnp.where(upd, vs[k], bs)
        return bi, bs

    i1, s1 = first_argmax(vb, None)
    i2, s2 = first_argmax(vb, i1)
    tot = s1 + s2
    e1 = best_g * per + i1
    e2 = best_g * per + i2
    idx_ref[0:1, :] = e1
    idx_ref[1:2, :] = e2
    wt_ref[0:1, :] = s1 / tot
    wt_ref[1:2, :] = s2 / tot
    eids = lax.broadcasted_iota(jnp.int32, (n_exp, e1.shape[1]), 0)
    hit1 = eids == e1
    hit2 = eids == e2
    oh1 = jnp.where(hit1, 1.0, 0.0)
    oh2 = jnp.where(hit2, 1.0, 0.0)
    p1 = _dot(oh1.astype(BF16), tri_ref[...])
    p2 = _dot(oh2.astype(BF16), tri_ref[...])
    c1 = jnp.sum(oh1, axis=1, keepdims=True)
    c2 = jnp.sum(oh2, axis=1, keepdims=True)
    run = run_ref[...]
    rank_ref[0:1, :] = jnp.sum(jnp.where(hit1, run + p1, 0.0), axis=0, keepdims=True).astype(jnp.int32)
    rank_ref[1:2, :] = jnp.sum(jnp.where(hit2, run + c1 + p2, 0.0), axis=0, keepdims=True).astype(jnp.int32)
    run_ref[...] = run + c1 + c2
    cnt_ref[...] = run + c1 + c2


def _route(dm, x, gain, mods, layer, rw_t, rb):
    t, d = x.shape
    n_exp = rw_t.shape[0]
    tm = dm.tm
    rt = d // 2 // LANES
    tri = (jnp.arange(tm)[:, None] < jnp.arange(tm)[None, :]).astype(BF16)
    return pl.pallas_call(
        functools.partial(_route_kernel, n_exp),
        grid=(t // tm,),
        in_specs=[pl.BlockSpec((tm, d), lambda i: (i, 0)),
                  pl.BlockSpec((1, 1, d), lambda i: (layer, 0, 0)),
                  pl.BlockSpec((1, 1, d), _mod_idx(dm, layer, 3, tm)),
                  pl.BlockSpec((1, 1, d), _mod_idx(dm, layer, 4, tm)),
                  pl.BlockSpec((n_exp, d), lambda i: (0, 0)),
                  pl.BlockSpec((n_exp, 1), lambda i: (0, 0)),
                  pl.BlockSpec((tm, tm), lambda i: (0, 0))],
        out_specs=[pl.BlockSpec((tm * rt, LANES), lambda i: (i, 0)),
                   pl.BlockSpec((2, tm), lambda i: (0, i)),
                   pl.BlockSpec((2, tm), lambda i: (0, i)),
                   pl.BlockSpec((2, tm), lambda i: (0, i)),
                   pl.BlockSpec((n_exp, 1), lambda i: (0, 0))],
        out_shape=[jax.ShapeDtypeStruct((t * rt, LANES), U32),
                   jax.ShapeDtypeStruct((2, t), jnp.int32),
                   jax.ShapeDtypeStruct((2, t), jnp.int32),
                   jax.ShapeDtypeStruct((2, t), F32),
                   jax.ShapeDtypeStruct((n_exp, 1), F32)],
        scratch_shapes=[pltpu.VMEM((n_exp, 1), F32)],
        compiler_params=_cp(("arbitrary",)),
        name="moe_route",
    )(x, gain, mods, mods, rw_t, rb, tri)


def _slot_plan(idx, rank, counts, n_exp, n_slots):
    counts = counts.reshape(n_exp).astype(jnp.int32)
    padded = ((counts + TM_MOE - 1) // TM_MOE) * TM_MOE
    ends = jnp.cumsum(padded)
    starts = ends - padded
    start_of = jnp.sum(jnp.where(idx[..., None] == jnp.arange(n_exp, dtype=jnp.int32), starts, 0), axis=-1)
    dest = (start_of + rank).reshape(-1).astype(jnp.int32)
    tile_start = jnp.arange(n_slots // TM_MOE, dtype=jnp.int32) * TM_MOE
    tile_exp = jnp.minimum(jnp.sum((ends[None, :] <= tile_start[:, None]).astype(jnp.int32), axis=1), n_exp - 1)
    tile_ok = (tile_start < ends[-1]).astype(jnp.int32)
    pad_lo = jnp.concatenate([starts + counts, ends[-1:]]).astype(jnp.int32)
    pad_n = (padded - counts).astype(jnp.int32)
    return dest, tile_exp.astype(jnp.int32), tile_ok, pad_lo, pad_n


def _scatter_kernel(n_exp, t_total, rows, rt, dest_ref, padlo_ref, padn_ref, hp_ref, xs_ref, zrow, sem):
    i = pl.program_id(0)
    base = i * rows
    tile_rows = TM_MOE * rt

    def slot(s):
        return xs_ref.at[pl.ds(pl.multiple_of(s * rt, rt), rt)]

    @pl.when(i == 0)
    def _():
        zrow[...] = jnp.zeros_like(zrow)
        for e in range(n_exp):
            lo = padlo_ref[e]

            def pad_copy(r, lo=lo):
                return pltpu.make_async_copy(zrow.at[pl.ds(0, rt)], slot(lo + r), sem)

            def start(r, c, pad_copy=pad_copy):
                pad_copy(r).start()
                return c

            def wait(r, c, pad_copy=pad_copy):
                pad_copy(r).wait()
                return c

            lax.fori_loop(0, padn_ref[e], start, 0)
            lax.fori_loop(0, padn_ref[e], wait, 0)

        tail0 = padlo_ref[n_exp]

        def tail_copy(k):
            row = pl.multiple_of((tail0 + k * TM_MOE) * rt, tile_rows)
            return pltpu.make_async_copy(zrow, xs_ref.at[pl.ds(row, tile_rows)], sem)

        def tail_start(k, c):
            tail_copy(k).start()
            return c

        def tail_wait(k, c):
            tail_copy(k).wait()
            return c

        n_tail = (xs_ref.shape[0] // rt - tail0) // TM_MOE
        lax.fori_loop(0, n_tail, tail_start, 0)
        lax.fori_loop(0, n_tail, tail_wait, 0)

    def copies(r):
        src = hp_ref.at[pl.ds(pl.multiple_of(r * rt, rt), rt)]
        return (pltpu.make_async_copy(src, slot(dest_ref[base + r]), sem),
                pltpu.make_async_copy(src, slot(dest_ref[t_total + base + r]), sem))

    def start(gi, c):
        for u in range(DMA_UNROLL):
            for prio, cp in enumerate(copies(gi * DMA_UNROLL + u)):
                cp.start(priority=prio)
        return c

    def wait(gi, c):
        for u in range(DMA_UNROLL):
            for cp in copies(gi * DMA_UNROLL + u):
                cp.wait()
        return c

    lax.fori_loop(0, rows // DMA_UNROLL, start, 0)
    lax.fori_loop(0, rows // DMA_UNROLL, wait, 0)


def _scatter_rows(hp, dest, pad_lo, pad_n, n_slots, rows, rt):
    t = hp.shape[0] // rt
    n_exp = pad_n.shape[0]
    return pl.pallas_call(
        functools.partial(_scatter_kernel, n_exp, t, rows, rt),
        grid_spec=pltpu.PrefetchScalarGridSpec(
            num_scalar_prefetch=3,
            grid=(t // rows,),
            in_specs=[pl.BlockSpec((rows * rt, LANES), lambda i, *_: (i, 0))],
            out_specs=pl.BlockSpec(memory_space=pl.ANY),
            scratch_shapes=[pltpu.VMEM((TM_MOE * rt, LANES), U32), pltpu.SemaphoreType.DMA(())]),
        out_shape=jax.ShapeDtypeStruct((n_slots * rt, LANES), U32),
        compiler_params=_cp(("arbitrary",)),
        name="moe_scatter",
    )(dest, pad_lo, pad_n, hp)


def _expert_kernel(rt, te_ref, ok_ref, x_ref, wg_ref, wu_ref, wd_ref, o_ref, wg_b, wu_b, wd_b):
    i = pl.program_id(0)
    new_expert = jnp.logical_or(i == 0, te_ref[i] != te_ref[jnp.maximum(i - 1, 0)])

    @pl.when(new_expert)
    def _():
        wg_b[...] = wg_ref[0, 0].astype(BF16)
        wu_b[...] = wu_ref[0, 0].astype(BF16)
        wd_b[...] = wd_ref[0, 0].astype(BF16)

    @pl.when(ok_ref[i] == 1)
    def _():
        x_lo, x_hi = _unpack_halves(_load_row_tiles(x_ref, rt))
        x_lo, x_hi = x_lo.astype(BF16), x_hi.astype(BF16)
        half = x_lo.shape[1]
        h1 = _dot(x_lo, wg_b[:half, :]) + _dot(x_hi, wg_b[half:, :])
        h2 = _dot(x_lo, wu_b[:half, :]) + _dot(x_hi, wu_b[half:, :])
        act = (_silu(h1) * h2).astype(BF16)
        _store_row_tiles(o_ref, _pack_halves(_dot(act, wd_b[...])))

    @pl.when(ok_ref[i] == 0)
    def _():
        o_ref[...] = jnp.zeros_like(o_ref)


def _experts(x_sorted, tile_exp, tile_ok, w_gate, w_up, w_down, layer, rt):
    n_slots = x_sorted.shape[0] // rt
    d, f = w_gate.shape[-2:]
    return pl.pallas_call(
        functools.partial(_expert_kernel, rt),
        grid_spec=pltpu.PrefetchScalarGridSpec(
            num_scalar_prefetch=2,
            grid=(n_slots // TM_MOE,),
            in_specs=[pl.BlockSpec((TM_MOE * rt, LANES), lambda i, te, ok: (i, 0)),
                      pl.BlockSpec((1, 1, d, f), lambda i, te, ok: (layer, te[i], 0, 0)),
                      pl.BlockSpec((1, 1, d, f), lambda i, te, ok: (layer, te[i], 0, 0)),
                      pl.BlockSpec((1, 1, f, d), lambda i, te, ok: (layer, te[i], 0, 0))],
            out_specs=pl.BlockSpec((TM_MOE * rt, LANES), lambda i, te, ok: (i, 0)),
            scratch_shapes=[pltpu.VMEM((d, f), BF16), pltpu.VMEM((d, f), BF16), pltpu.VMEM((f, d), BF16)]),
        out_shape=jax.ShapeDtypeStruct((n_slots * rt, LANES), U32),
        compiler_params=_cp(("arbitrary",)),
        name="moe_experts",
    )(tile_exp, tile_ok, x_sorted, w_gate, w_up, w_down)


def _combine_kernel(t_total, rows, rt, dest_ref, x_ref, gate_ref, wt_ref, y_ref, o_ref, y0, y1, sem):
    base = pl.program_id(0) * rows

    def slot(s):
        return y_ref.at[pl.ds(pl.multiple_of(s * rt, rt), rt)]

    def copies(r):
        dst = pl.ds(pl.multiple_of(r * rt, rt), rt)
        return (pltpu.make_async_copy(slot(dest_ref[base + r]), y0.at[dst], sem),
                pltpu.make_async_copy(slot(dest_ref[t_total + base + r]), y1.at[dst], sem))

    def start(gi, c):
        for u in range(DMA_UNROLL):
            for prio, cp in enumerate(copies(gi * DMA_UNROLL + u)):
                cp.start(priority=prio)
        return c

    def wait(gi, c):
        for u in range(DMA_UNROLL):
            for cp in copies(gi * DMA_UNROLL + u):
                cp.wait()
        return c

    lax.fori_loop(0, rows // DMA_UNROLL, start, 0)
    lax.fori_loop(0, rows // DMA_UNROLL, wait, 0)
    w0 = wt_ref[:, 0:1]
    w1 = wt_ref[:, 1:2]
    a_lo, a_hi = _unpack_halves(_load_row_tiles(y0, rt))
    b_lo, b_hi = _unpack_halves(_load_row_tiles(y1, rt))
    half = a_lo.shape[1]
    gate = gate_ref[0]
    o_ref[:, :half] = x_ref[:, :half] + gate[:, :half] * (w0 * a_lo + w1 * b_lo)
    o_ref[:, half:] = x_ref[:, half:] + gate[:, half:] * (w0 * a_hi + w1 * b_hi)


def _combine(dm, x, mods, layer, y_sorted, dest, wts_t):
    t, d = x.shape
    rows = min(dm.tm, 512)
    rt = d // 2 // LANES
    return pl.pallas_call(
        functools.partial(_combine_kernel, t, rows, rt),
        grid_spec=pltpu.PrefetchScalarGridSpec(
            num_scalar_prefetch=1,
            grid=(t // rows,),
            in_specs=[pl.BlockSpec((rows, d), lambda i, dst: (i, 0)),
                      pl.BlockSpec((1, 1, d), lambda i, dst: _mod_idx(dm, layer, 5, rows)(i)),
                      pl.BlockSpec((rows, 2), lambda i, dst: (i, 0)),
                      pl.BlockSpec(memory_space=pl.ANY)],
            out_specs=pl.BlockSpec((rows, d), lambda i, dst: (i, 0)),
            scratch_shapes=[pltpu.VMEM((rows * rt, LANES), U32), pltpu.VMEM((rows * rt, LANES), U32),
                            pltpu.SemaphoreType.DMA(())]),
        out_shape=jax.ShapeDtypeStruct((t, d), F32),
        compiler_params=_cp(("arbitrary",)),
        name="moe_combine",
    )(dest, x, mods, wts_t, y_sorted)


def _moe(dm, x, gain, mods, layer, rw_t, rb, w_gate, w_up, w_down):
    n_exp = rw_t.shape[0]
    hp, idx, rank, wts, counts = _route(dm, x, gain, mods, layer, rw_t, rb)
    n_slots = 2 * dm.t + n_exp * TM_MOE
    dest, tile_exp, tile_ok, pad_lo, pad_n = _slot_plan(idx, rank, counts, n_exp, n_slots)
    rt = dm.d // 2 // LANES
    x_sorted = _scatter_rows(hp, dest, pad_lo, pad_n, n_slots, min(dm.tm, 512), rt)
    y_sorted = _experts(x_sorted, tile_exp, tile_ok, w_gate, w_up, w_down, layer, rt)
    return _combine(dm, x, mods, layer, y_sorted, dest, wts.T)


def _final_norm_kernel(x_ref, g_ref, o_ref):
    x = x_ref[...]
    o_ref[...] = x * lax.rsqrt(jnp.mean(x * x, axis=-1, keepdims=True) + EPS) * g_ref[...]


def _final_norm(dm, x, gain, row0, rows):
    d = x.shape[1]
    tm = dm.tm
    blk0 = row0 // tm
    return pl.pallas_call(
        _final_norm_kernel,
        grid=(rows // tm,),
        in_specs=[pl.BlockSpec((tm, d), lambda i: (blk0 + i, 0)),
                  pl.BlockSpec((1, d), lambda i: (0, 0))],
        out_specs=pl.BlockSpec((tm, d), lambda i: (i, 0)),
        out_shape=jax.ShapeDtypeStruct((rows, d), F32),
        compiler_params=_cp(("parallel",)),
        name="final_norm",
    )(x, gain.reshape(1, d))


def _rope_tables(dm):
    rows = dm.ls // GRID_W
    row = jnp.repeat(jnp.arange(rows, dtype=F32), GRID_W)
    col = jnp.tile(jnp.arange(GRID_W, dtype=F32), rows)
    half = QK_ROPE // 2
    freqs = jnp.power(ROPE_BASE, -jnp.arange(0, half, 2, dtype=F32) / half)
    ar, ac = row[:, None] * freqs, col[:, None] * freqs
    zeros = jnp.zeros((dm.ls, 128 - QK_ROPE), F32)
    ct = jnp.concatenate([jnp.cos(ar), jnp.cos(ar), jnp.cos(ac), jnp.cos(ac), zeros], axis=-1)
    st = jnp.concatenate([-jnp.sin(ar), jnp.sin(ar), -jnp.sin(ac), jnp.sin(ac), zeros], axis=-1)
    ct_c = jnp.concatenate([jnp.ones((dm.tc, QK_ROPE), F32), jnp.zeros((dm.tc, 128 - QK_ROPE), F32)], axis=-1)
    ct = jnp.concatenate([ct_c, jnp.tile(ct, (dm.bs, 1))], axis=0)
    st = jnp.concatenate([jnp.zeros((dm.tc, 128), F32), jnp.tile(st, (dm.bs, 1))], axis=0)
    return ct, st


def _swap_halves_cols(w):
    qt = QK_ROPE // 4
    return jnp.concatenate([w[..., qt:2 * qt], w[..., :qt], w[..., 3 * qt:], w[..., 2 * qt:3 * qt]], axis=-1)


def kernel(x_prompt, x_sample, c, state_ret, state_s5_re, state_s5_im, cache_ckv, cache_krope, c_ctx, ada_w, ada_b, norm_mix, norm_ffn, norm_final, even_w_in, even_w_out, ret_decay, s5_a_re, s5_a_im, s5_log_dt, s5_b_re, s5_b_im, s5_c_re, s5_c_im, s5_d, s5_w_glu, mla_w_in, mla_q_norm, mla_w_uq, mla_kv_norm, mla_w_ukv, mla_w_out, router_w, router_bias, moe_w_gate, moe_w_up, moe_w_down):
    dm = _Dims(x_prompt, x_sample)
    d = dm.d
    depth = ada_w.shape[0]
    n_exp = router_w.shape[1]
    past = cache_ckv.shape[2]
    q_lora = mla_q_norm.shape[1]
    kv_lora = mla_kv_norm.shape[1]

    x = jnp.concatenate([x_prompt.reshape(dm.tc, d), x_sample.reshape(dm.ts, d)], axis=0)
    cond = jnp.zeros((dm.rp, d), F32).at[0].set(c_ctx).at[1:1 + dm.bs].set(c)
    mods = _adaln(cond, ada_w, ada_b, tn=d * N_MOD // 8).reshape(depth * dm.rp * N_MOD, 1, d)
    g_mix = norm_mix.reshape(depth, 1, d)
    g_ffn = norm_ffn.reshape(depth, 1, d)
    rw_t = router_w.T.astype(BF16)
    rb = router_bias.astype(F32).reshape(n_exp, 1)
    ct, st = _rope_tables(dm)
    s5_ops = jax.vmap(_s5_operators)(s5_a_re, s5_a_im, s5_log_dt, s5_b_re, s5_b_im, s5_c_re, s5_c_im)

    rets, s5r, s5i, ckvs, krs = [], [], [], [], []
    for layer in range(depth):
        if layer % 2 == 0:
            i = layer // 2
            ret_w = even_w_out.shape[1] - s5_d.shape[1]
            s5_w = s5_d.shape[1]
            z, u = _norm_mod_matmul_split(dm, x, g_mix, mods, layer, 0, even_w_in[i].astype(BF16), tn=s5_w,
                                          name="even_in_proj")
            log_gamma = -jnp.exp(ret_decay[i].astype(F32))
            ro_c, sfin = _retention(z, ret_w // RET_HEADS, log_gamma, None, dm.bc, dm.lc, 0, "retention_ctx")
            ro_s, _ = _retention(z, ret_w // RET_HEADS, log_gamma, (state_ret, i), dm.bs, dm.ls,
                                 dm.tc // dm.ls, "retention_lat")
            ops = tuple(op[i] for op in s5_ops)
            y, f_re, f_im = _s5(dm, u, ops, s5_d[i], state_s5_re[:, i], state_s5_im[:, i])
            s5_out = _glu(dm, y, s5_w_glu[i].astype(BF16))
            w_out = even_w_out[i].astype(BF16)
            x = _matmul_residual(dm, x, mods, layer, 2, [(ro_c, ro_s), s5_out], [w_out[:ret_w], w_out[ret_w:]],
                                 tn=d // 2, name="even_out_proj")
            rets.append(sfin)
            s5r.append(f_re)
            s5i.append(f_im)
        else:
            j = layer // 2
            w_in = mla_w_in[j]
            w_in_ext = jnp.concatenate([w_in, _swap_halves_cols(w_in[:, q_lora + kv_lora:])], axis=1).astype(BF16)
            a = _norm_mod_matmul(dm, x, g_mix, mods, layer, 0, w_in_ext, F32, tn=w_in_ext.shape[1],
                                 name="mla_in_proj")
            w_uq = mla_w_uq[j].reshape(q_lora, MLA_HEADS, QK_NOPE + QK_ROPE)
            w_uq_ext = jnp.concatenate([w_uq, _swap_halves_cols(w_uq[..., QK_NOPE:])], axis=-1)
            w_uq_ext = w_uq_ext.reshape(q_lora, MLA_HEADS * QK_PAD).astype(BF16)
            q, ckv_n, kr = _mla_q(dm, a, mla_q_norm[j].reshape(1, q_lora), mla_kv_norm[j].reshape(1, kv_lora),
                                  ct, st, w_uq_ext, q_lora, kv_lora)
            lk = past + dm.ls
            w_ukv = mla_w_ukv[j].reshape(kv_lora, MLA_HEADS, QK_NOPE + V_HEAD)
            w_uk = w_ukv[..., :QK_NOPE].reshape(kv_lora, MLA_HEADS * QK_NOPE).astype(BF16)
            w_uv = w_ukv[..., QK_NOPE:].reshape(kv_lora, MLA_HEADS * V_HEAD).astype(BF16)
            k_all, v_all = _kv_expand(dm, cache_ckv, cache_krope, j, ckv_n, kr, w_uk, w_uv)
            o_c = _attention(q, k_all, v_all, dm.bc, dm.lc, dm.lc, dm.lc, MLA_HEADS, 0, dm.bs * lk, "attn_ctx")
            o_s = _attention(q, k_all, v_all, dm.bs, dm.ls, lk, min(TQ_MAX, dm.ls), LAT_HEADS_PER_STEP, dm.tc, 0,
                             "attn_lat")
            x = _matmul_residual(dm, x, mods, layer, 2, [(o_c, o_s)], [mla_w_out[j].astype(BF16)], tn=d // 2,
                                 name="mla_out_proj")
            ckvs.append(ckv_n[:dm.tc].reshape(dm.bc, dm.lc, kv_lora))
            krs.append(kr[:dm.tc, :QK_ROPE].reshape(dm.bc, dm.lc, QK_ROPE))
        x = _moe(dm, x, g_ffn, mods, layer, rw_t, rb, moe_w_gate, moe_w_up, moe_w_down)

    y_prompt = _final_norm(dm, x, norm_final, 0, dm.tc).reshape(dm.bc, dm.lc, d)
    y_sample = _final_norm(dm, x, norm_final, dm.tc, dm.ts).reshape(dm.bs, dm.ls, d)
    return (y_prompt, y_sample, jnp.stack(rets, axis=1), jnp.stack(s5r, axis=1), jnp.stack(s5i, axis=1),
            jnp.stack(ckvs, axis=1), jnp.stack(krs, axis=1))
```

```python
import functools
import math

import jax
import jax.numpy as jnp
import numpy as np
from jax import lax
from jax.experimental import pallas as pl
from jax.experimental.pallas import tpu as pltpu

F32 = jnp.float32
BF16 = jnp.bfloat16
U32 = jnp.uint32
EPS = 1e-6

RET_HEADS = 4
RET_CHUNK = 128
RET_UNROLL = 4
S5_GROUP = 16
S5_Q = 16
S5_LANES = 128
LANES = 128
MLA_HEADS = 16
QK_NOPE = 128
QK_ROPE = 64
V_HEAD = 128
QK_PAD = 256
V_PAD = 256
GRID_W = 64
ROPE_BASE = 10000.0
N_EXPERT_GROUPS = 4
N_MOD = 6

VMEM_LIMIT = 56 * 1024 * 1024
TM_MAX = 1024
TQ_MAX = 512
KB_MAX = 768
LAT_HEADS_PER_STEP = 4
TM_MOE = 256
DMA_UNROLL = 8
HI_MASK = np.uint32(0xFFFF0000)


def _cp(sem, vmem=VMEM_LIMIT):
    return pltpu.CompilerParams(dimension_semantics=sem, vmem_limit_bytes=vmem)


def _sigmoid(x):
    return 1.0 / (1.0 + jnp.exp(-x))


def _silu(x):
    return x * _sigmoid(x)


def _gelu_tanh(x):
    return 0.5 * x * (1.0 + jnp.tanh(math.sqrt(2.0 / math.pi) * (x + 0.044715 * (x * x * x))))


def _dot(a, b):
    return jnp.dot(a, b, preferred_element_type=F32)


def _dot_nt(a, b):
    return lax.dot_general(a, b, (((1,), (1,)), ((), ())), preferred_element_type=F32)


def _dot_tn(a, b):
    return lax.dot_general(a, b, (((0,), (0,)), ((), ())), preferred_element_type=F32)


def _pack_halves(x):
    half = x.shape[1] // 2
    xb = x.astype(BF16).astype(F32)
    lo = lax.bitcast_convert_type(xb[:, :half], U32) >> 16
    hi = lax.bitcast_convert_type(xb[:, half:], U32) & HI_MASK
    return hi | lo


def _unpack_halves(w):
    return (lax.bitcast_convert_type(w << 16, F32), lax.bitcast_convert_type(w & HI_MASK, F32))


def _store_row_tiles(ref, w):
    r, rt = w.shape[0], w.shape[1] // LANES
    for s in range(rt):
        ref[pl.ds(s, r, stride=rt), :] = w[:, s * LANES:(s + 1) * LANES]


def _load_row_tiles(ref, rt):
    r = ref.shape[0] // rt
    return jnp.concatenate([ref[pl.ds(s, r, stride=rt), :] for s in range(rt)], axis=1)


class _Dims:
    def __init__(self, x_prompt, x_sample):
        self.bc, self.lc, self.d = x_prompt.shape
        self.bs, self.ls, _ = x_sample.shape
        self.tc = self.bc * self.lc
        self.ts = self.bs * self.ls
        self.t = self.tc + self.ts
        self.rp = -(-(1 + self.bs) // 8) * 8
        self.tm = min(TM_MAX, self.ls)
        assert self.tc % self.tm == 0 and self.ls % self.tm == 0

    def mod_row(self, i, tm):
        nct = self.tc // tm
        return jnp.where(i < nct, 0, 1 + (i - nct) // (self.ls // tm))


def _mod_idx(dm, layer, k, tm):
    def idx(i, *_):
        return ((layer * dm.rp + dm.mod_row(i, tm)) * N_MOD + k, 0, 0)
    return idx


def _adaln_kernel(c_ref, w_ref, b_ref, o_ref):
    cs = _silu(c_ref[...]).astype(BF16)
    o_ref[0] = _dot(cs, w_ref[0].astype(BF16)) + b_ref[0]


def _adaln(cond, ada_w, ada_b, tn):
    depth, d, n = ada_w.shape
    rp = cond.shape[0]
    return pl.pallas_call(
        _adaln_kernel,
        grid=(depth, n // tn),
        in_specs=[pl.BlockSpec((rp, d), lambda l, j: (0, 0)),
                  pl.BlockSpec((1, d, tn), lambda l, j: (l, 0, j)),
                  pl.BlockSpec((1, 1, tn), lambda l, j: (l, 0, j))],
        out_specs=pl.BlockSpec((1, rp, tn), lambda l, j: (l, 0, j)),
        out_shape=jax.ShapeDtypeStruct((depth, rp, n), F32),
        compiler_params=_cp(("parallel", "parallel")),
        name="adaln",
    )(cond, ada_w, ada_b.reshape(depth, 1, n))


def _norm_mod(x, g, sc, sh):
    y = x * lax.rsqrt(jnp.mean(x * x, axis=-1, keepdims=True) + EPS) * g
    return y * (1.0 + sc) + sh


def _nmm_kernel(x_ref, g_ref, sh_ref, sc_ref, w_ref, o_ref, hn_ref):
    @pl.when(pl.program_id(1) == 0)
    def _():
        hn_ref[...] = _norm_mod(x_ref[...], g_ref[0], sc_ref[0], sh_ref[0]).astype(BF16)

    o_ref[...] = _dot(hn_ref[...], w_ref[...]).astype(o_ref.dtype)


def _norm_mod_matmul(dm, x, gain, mods, layer, k_shift, w, out_dtype, tn, name):
    t, d = x.shape
    n = w.shape[1]
    tm = dm.tm
    return pl.pallas_call(
        _nmm_kernel,
        grid=(t // tm, n // tn),
        in_specs=[pl.BlockSpec((tm, d), lambda i, j: (i, 0)),
                  pl.BlockSpec((1, 1, d), lambda i, j: (layer, 0, 0)),
                  pl.BlockSpec((1, 1, d), _mod_idx(dm, layer, k_shift, tm)),
                  pl.BlockSpec((1, 1, d), _mod_idx(dm, layer, k_shift + 1, tm)),
                  pl.BlockSpec((d, tn), lambda i, j: (0, j))],
        out_specs=pl.BlockSpec((tm, tn), lambda i, j: (i, j)),
        out_shape=jax.ShapeDtypeStruct((t, n), out_dtype),
        scratch_shapes=[pltpu.VMEM((tm, d), BF16)],
        compiler_params=_cp(("parallel", "arbitrary")),
        name=name,
    )(x, gain, mods, mods, w)


def _nmm_split_kernel(n_main, x_ref, g_ref, sh_ref, sc_ref, w_ref, o_main_ref, o_tail_ref, hn_ref):
    j = pl.program_id(1)

    @pl.when(j == 0)
    def _():
        hn_ref[...] = _norm_mod(x_ref[...], g_ref[0], sc_ref[0], sh_ref[0]).astype(BF16)

    acc = _dot(hn_ref[...], w_ref[...])

    @pl.when(j < n_main)
    def _():
        o_main_ref[...] = acc.astype(o_main_ref.dtype)

    @pl.when(j >= n_main)
    def _():
        o_tail_ref[...] = acc


def _norm_mod_matmul_split(dm, x, gain, mods, layer, k_shift, w, tn, name):
    t, d = x.shape
    n = w.shape[1]
    tm = dm.tm
    n_main = n // tn - 1
    return pl.pallas_call(
        functools.partial(_nmm_split_kernel, n_main),
        grid=(t // tm, n // tn),
        in_specs=[pl.BlockSpec((tm, d), lambda i, j: (i, 0)),
                  pl.BlockSpec((1, 1, d), lambda i, j: (layer, 0, 0)),
                  pl.BlockSpec((1, 1, d), _mod_idx(dm, layer, k_shift, tm)),
                  pl.BlockSpec((1, 1, d), _mod_idx(dm, layer, k_shift + 1, tm)),
                  pl.BlockSpec((d, tn), lambda i, j: (0, j))],
        out_specs=[pl.BlockSpec((tm, tn), lambda i, j: (i, jnp.minimum(j, n_main - 1))),
                   pl.BlockSpec((tm, tn), lambda i, j: (i, 0))],
        out_shape=[jax.ShapeDtypeStruct((t, n_main * tn), BF16),
                   jax.ShapeDtypeStruct((t, tn), F32)],
        scratch_shapes=[pltpu.VMEM((tm, d), BF16)],
        compiler_params=_cp(("parallel", "arbitrary")),
        name=name,
    )(x, gain, mods, mods, w)


def _mmres_kernel(split, nct, x_ref, gate_ref, *refs):
    n_a = sum(2 if sp else 1 for sp in split)
    a_refs, w_refs, o_ref = refs[:n_a], refs[n_a:n_a + len(split)], refs[n_a + len(split)]

    def run(use_ctx):
        acc, pos = None, 0
        for sp, w_ref in zip(split, w_refs):
            a_ref = a_refs[pos if (use_ctx or not sp) else pos + 1]
            pos += 2 if sp else 1
            part = _dot(a_ref[...], w_ref[...])
            acc = part if acc is None else acc + part
        o_ref[...] = x_ref[...] + gate_ref[0] * acc

    if any(split):
        is_ctx = pl.program_id(0) < nct
        pl.when(is_ctx)(lambda: run(True))
        pl.when(jnp.logical_not(is_ctx))(lambda: run(False))
    else:
        run(True)


def _matmul_residual(dm, x, mods, layer, k_gate, acts, ws, tn, name):
    t, d = x.shape
    tm = dm.tm
    nct = dm.tc // tm
    split = tuple(isinstance(a, tuple) for a in acts)

    def gate_idx(i, j):
        return ((layer * dm.rp + dm.mod_row(i, tm)) * N_MOD + k_gate, 0, j)

    in_specs = [pl.BlockSpec((tm, tn), lambda i, j: (i, j)),
                pl.BlockSpec((1, 1, tn), gate_idx)]
    flat = []
    for a in acts:
        if isinstance(a, tuple):
            in_specs.append(pl.BlockSpec((tm, a[0].shape[1]), lambda i, j: (jnp.minimum(i, nct - 1), 0)))
            in_specs.append(pl.BlockSpec((tm, a[1].shape[1]), lambda i, j: (jnp.maximum(i - nct, 0), 0)))
            flat += list(a)
        else:
            in_specs.append(pl.BlockSpec((tm, a.shape[1]), lambda i, j: (i, 0)))
            flat.append(a)
    in_specs += [pl.BlockSpec((w.shape[0], tn), lambda i, j: (0, j)) for w in ws]
    return pl.pallas_call(
        functools.partial(_mmres_kernel, split, nct),
        grid=(t // tm, d // tn),
        in_specs=in_specs,
        out_specs=pl.BlockSpec((tm, tn), lambda i, j: (i, j)),
        out_shape=jax.ShapeDtypeStruct((t, d), F32),
        compiler_params=_cp(("parallel", "parallel")),
        name=name,
    )(x, mods, *flat, *ws)


def _ret_kernel(has_s0, nch, scale, lg_ref, q_ref, k_ref, v_ref, g_ref, *refs):
    if has_s0:
        s0_ref, o_ref, sfin_ref, oacc_ref, st_ref = refs
    else:
        o_ref, sfin_ref, oacc_ref, st_ref = refs
    c = RET_CHUNK
    h = pl.program_id(1)
    lgf = lg_ref[0, h]
    lgb = lg_ref[1, h]
    dv = v_ref.shape[1]

    ii = lax.broadcasted_iota(jnp.int32, (c, c), 0).astype(F32)
    jj = lax.broadcasted_iota(jnp.int32, (c, c), 1).astype(F32)
    diff = ii - jj
    dtot = (jnp.where(diff >= 0, jnp.exp(lgf * jnp.maximum(diff, 0.0)), 0.0)
            + jnp.where(diff <= 0, jnp.exp(lgb * jnp.maximum(-diff, 0.0)), 0.0)) * scale
    pos = lax.broadcasted_iota(jnp.int32, (c, 1), 0).astype(F32)
    qdec_f = jnp.exp(lgf * (pos + 1.0)) * scale
    kdec_f = jnp.exp(lgf * (c - 1.0 - pos))
    qdec_b = jnp.exp(lgb * (c - pos)) * scale
    kdec_b = jnp.exp(lgb * pos)
    cdec_f = jnp.exp(jnp.full((1, dv), lgf * c, F32))
    cdec_b = jnp.exp(jnp.full((1, dv), lgb * c, F32))

    if has_s0:
        st_ref[...] = s0_ref[0, 0, 0, 0].astype(F32)
    else:
        st_ref[...] = jnp.zeros_like(st_ref)

    def fwd(n, carry):
        r = pl.multiple_of(n * c, c)
        qn = q_ref[pl.ds(r, c), :]
        kn = k_ref[pl.ds(r, c), :]
        vn = v_ref[pl.ds(r, c), :]
        p = (_dot_nt(qn, kn) * dtot).astype(BF16)
        o = _dot(p, vn)
        o += _dot((qn.astype(F32) * qdec_f).astype(BF16), st_ref[...].astype(BF16))
        oacc_ref[pl.ds(r, c), :] = o
        kd = (kn.astype(F32) * kdec_f).astype(BF16)
        st_ref[...] = st_ref[...] * cdec_f + _dot_tn(kd, vn)
        return carry

    lax.fori_loop(0, nch, fwd, 0, unroll=min(nch, RET_UNROLL))
    sfin_ref[0, 0, 0] = st_ref[...]

    if has_s0:
        st_ref[...] = s0_ref[0, 0, 1, 0].astype(F32)
    else:
        st_ref[...] = jnp.zeros_like(st_ref)

    def bwd(m, carry):
        r = pl.multiple_of((nch - 1 - m) * c, c)
        qn = q_ref[pl.ds(r, c), :]
        kn = k_ref[pl.ds(r, c), :]
        vn = v_ref[pl.ds(r, c), :]
        oacc_ref[pl.ds(r, c), :] += _dot((qn.astype(F32) * qdec_b).astype(BF16), st_ref[...].astype(BF16))
        kd = (kn.astype(F32) * kdec_b).astype(BF16)
        st_ref[...] = st_ref[...] * cdec_b + _dot_tn(kd, vn)
        return carry

    lax.fori_loop(0, nch, bwd, 0, unroll=min(nch, RET_UNROLL))
    sfin_ref[0, 1, 0] = st_ref[...]

    o = oacc_ref[...]
    o = o * lax.rsqrt(jnp.mean(o * o, axis=-1, keepdims=True) + EPS)
    o_ref[...] = (_silu(g_ref[...].astype(F32)) * o).astype(o_ref.dtype)


def _retention(z, dk, log_gamma, s0, nb, seq, row_blk0, name):
    hh = RET_HEADS
    dv = dk
    nch = seq // RET_CHUNK
    has_s0 = s0 is not None
    scale = float(dk) ** -0.5

    def col(off):
        return lambda b, h: (row_blk0 + b, off + h)

    in_specs = [pl.BlockSpec(memory_space=pltpu.SMEM),
                pl.BlockSpec((seq, dk), col(0)),
                pl.BlockSpec((seq, dk), col(hh)),
                pl.BlockSpec((seq, dv), col(2 * hh)),
                pl.BlockSpec((seq, dv), col(3 * hh))]
    args = [log_gamma, z, z, z, z]
    if has_s0:
        s0_all, s0_layer = s0
        in_specs.append(pl.BlockSpec((1, 1, 2, 1, dk, dv), lambda b, h: (b, s0_layer, 0, h, 0, 0)))
        args.append(s0_all)
    return pl.pallas_call(
        functools.partial(_ret_kernel, has_s0, nch, scale),
        grid=(nb, hh),
        in_specs=in_specs,
        out_specs=[pl.BlockSpec((seq, dv), lambda b, h: (b, h)),
                   pl.BlockSpec((1, 2, 1, dk, dv), lambda b, h: (b, 0, h, 0, 0))],
        out_shape=[jax.ShapeDtypeStruct((nb * seq, hh * dv), BF16),
                   jax.ShapeDtypeStruct((nb, 2, hh, dk, dv), F32)],
        scratch_shapes=[pltpu.VMEM((seq, dv), F32), pltpu.VMEM((dk, dv), F32)],
        compiler_params=_cp(("parallel", "parallel")),
        name=name,
    )(*args)


def _s5_kernel(rc, ncc, bc, ncs, bs, u_ref, strip_ref, wst_ref, v_ref, lam_ref, d_ref, x0_ref,
               y_ref, fin_ref, mt_ref, sre, sim, are, aim, bre, bim):
    u = u_ref[0]
    q, cg = S5_Q, S5_GROUP
    strip = strip_ref[0]
    for j in range(q):
        off = (q - 1 - j) * cg
        mt_ref[j * cg:(j + 1) * cg, :] = strip[:, off:off + q * cg].astype(BF16)
    z = _dot(u, wst_ref[0])
    y0 = _dot(u, mt_ref[...]) + d_ref[0] * u.astype(F32)
    sre[...] = z[:, :S5_LANES]
    sim[...] = z[:, S5_LANES:2 * S5_LANES]
    lr = lam_ref[0, 0:1, :]
    li = lam_ref[0, 1:2, :]

    def scan(row0, nc, b, xr0, xi0):
        is_f = lax.broadcasted_iota(jnp.int32, (b, S5_LANES), 1) < S5_LANES // 2

        def body(s, carry):
            xr, xi = carry
            rf = pl.multiple_of(row0 + s * b, b)
            rb = pl.multiple_of(row0 + (nc - 1 - s) * b, b)
            are[pl.ds(rf, b), :] = xr
            aim[pl.ds(rf, b), :] = xi
            bre[pl.ds(rb, b), :] = xr
            bim[pl.ds(rb, b), :] = xi
            sr = jnp.where(is_f, sre[pl.ds(rf, b), :], sre[pl.ds(rb, b), :])
            si = jnp.where(is_f, sim[pl.ds(rf, b), :], sim[pl.ds(rb, b), :])
            return xr * lr - xi * li + sr, xi * lr + xr * li + si

        return lax.fori_loop(0, nc, body, (xr0, xi0))

    zero = jnp.zeros((bc, S5_LANES), F32)
    fr, fi = scan(0, ncc, bc, zero, zero)
    fin_ref[0, 0] = fr
    fin_ref[0, 1] = fi
    scan(rc, ncs, bs, x0_ref[0, 0], x0_ref[0, 1])
    y_ref[0] = (y0 + _dot(are[...].astype(BF16), v_ref[0, 0]) + _dot(aim[...].astype(BF16), v_ref[0, 1])
                + _dot(bre[...].astype(BF16), v_ref[0, 2]) + _dot(bim[...].astype(BF16), v_ref[0, 3])
                ).astype(y_ref.dtype)


def _cmul(ar, ai, br, bi):
    return ar * br - ai * bi, ar * bi + ai * br


def _cexp(re, im):
    e = jnp.exp(re)
    return e * jnp.cos(im), e * jnp.sin(im)


def _s5_operators(a_re, a_im, log_dt, b_re, b_im, c_re, c_im):
    q = S5_Q
    hp = lax.Precision.HIGHEST
    a_re, a_im = a_re.astype(F32), a_im.astype(F32)
    dt = jnp.exp(log_dt.astype(F32))[..., None]
    ldr, ldi = a_re * dt, a_im * dt
    lbr, lbi = _cexp(ldr, ldi)
    den = a_re * a_re + a_im * a_im
    fr = ((lbr - 1.0) * a_re + lbi * a_im) / den
    fi = (lbi * a_re - (lbr - 1.0) * a_im) / den
    bbr, bbi = _cmul(fr[..., None], fi[..., None], b_re.astype(F32), b_im.astype(F32))
    c_re, c_im = c_re.astype(F32), c_im.astype(F32)
    g, p = a_re.shape[1], a_re.shape[2]
    cg = bbr.shape[-1]
    steps = jnp.arange(q + 1, dtype=F32)
    pwr, pwi = _cexp(ldr[..., None] * steps, ldi[..., None] * steps)
    c_pr, c_pi = jnp.swapaxes(c_re, 2, 3), jnp.swapaxes(c_im, 2, 3)
    cpr, cpi = _cmul(c_pr[:, :, :, None, :], c_pi[:, :, :, None, :],
                     pwr[..., :q, None], pwi[..., :q, None])
    cpr, cpi = cpr.reshape(2, g, p, q * cg), cpi.reshape(2, g, p, q * cg)
    kern = (jnp.einsum('dgpe,dgpx->dgex', bbr, cpr, precision=hp)
            - jnp.einsum('dgpe,dgpx->dgex', bbi, cpi, precision=hp)).reshape(2, g, cg, q, cg)
    strip = jnp.concatenate([kern[1][:, :, :0:-1], kern[0][:, :, :1] + kern[1][:, :, :1], kern[0][:, :, 1:],
                             jnp.zeros((g, cg, 1, cg), F32)], axis=2).reshape(g, cg, 2 * q * cg)
    ti = jnp.arange(q)

    def state_in(d, t_idx):
        wr, wi = _cmul(pwr[d][..., t_idx][..., None], pwi[d][..., t_idx][..., None],
                       bbr[d][:, :, None, :], bbi[d][:, :, None, :])
        tr = lambda w: jnp.transpose(w, (0, 2, 3, 1)).reshape(g, q * cg, p)
        return tr(wr), tr(wi)

    wfr, wfi = state_in(0, q - 1 - ti)
    wbr, wbi = state_in(1, ti)
    wst = jnp.concatenate([wfr, wbr, wfi, wbi], axis=-1)

    def state_out(d, t_idx):
        vr, vi = _cmul(c_re[d][..., None], c_im[d][..., None],
                       pwr[d][:, None, :, :][..., t_idx], pwi[d][:, None, :, :][..., t_idx])
        tr = lambda w: jnp.transpose(w, (0, 2, 3, 1)).reshape(g, p, q * cg)
        return tr(vr), tr(vi)

    vfr, vfi = state_out(0, ti + 1)
    vbr, vbi = state_out(1, q - ti)
    zf = jnp.zeros_like(vfr)
    v = jnp.stack([jnp.concatenate([vfr, zf], axis=1), jnp.concatenate([-vfi, zf], axis=1),
                   jnp.concatenate([zf, vbr], axis=1), jnp.concatenate([zf, -vbi], axis=1)], axis=1)
    lam_pack = jnp.stack([jnp.concatenate([pwr[0][..., q], pwr[1][..., q]], axis=-1),
                          jnp.concatenate([pwi[0][..., q], pwi[1][..., q]], axis=-1)], axis=1)
    return strip, wst.astype(BF16), v.astype(BF16), lam_pack


def _s5_rows(dm):
    rc = dm.lc // S5_Q * dm.bc
    return rc, rc + dm.ls // S5_Q * dm.bs


def _chunk_row_loops(dm, body):
    rc, _ = _s5_rows(dm)
    for tok0, row0, nc, b, seq in ((0, 0, dm.lc // S5_Q, dm.bc, dm.lc), (dm.tc, rc, dm.ls // S5_Q, dm.bs, dm.ls)):
        def step(n, c, tok0=tok0, row0=row0, b=b, seq=seq):
            body(tok0 + n * S5_Q, pl.multiple_of(row0 + n * b, b), b, seq)
            return c
        lax.fori_loop(0, nc, step, 0)


def _s5_pack_kernel(dm, u_ref, o_ref, a_ref, ab_ref):
    q, cg = S5_Q, S5_GROUP

    def gather(tok, row, b, seq):
        for i in range(q):
            a_ref[i, pl.ds(row, b), :] = u_ref[pl.ds(tok + i, b, stride=seq), :]

    _chunk_row_loops(dm, gather)
    ab_ref[...] = a_ref[...].astype(BF16)
    src = lax.broadcasted_iota(jnp.int32, (2 * LANES, q * cg), 0)
    dst = lax.broadcasted_iota(jnp.int32, (2 * LANES, q * cg), 1)
    for gl in range(LANES // cg):
        acc = None
        for i in range(0, q, 2):
            sel = jnp.where((src % LANES == gl * cg + dst % cg) & (dst // cg == i + src // LANES),
                            1.0, 0.0).astype(BF16)
            part = _dot(jnp.concatenate([ab_ref[i], ab_ref[i + 1]], axis=1), sel)
            acc = part if acc is None else acc + part
        o_ref[gl] = acc.astype(o_ref.dtype)


def _s5_unpack_kernel(dm, y_ref, o_ref, yt_ref):
    q, cg = S5_Q, S5_GROUP
    src = lax.broadcasted_iota(jnp.int32, (q * cg, 2 * LANES), 0)
    dst = lax.broadcasted_iota(jnp.int32, (q * cg, 2 * LANES), 1)
    for i in range(0, q, 2):
        acc = None
        for gl in range(LANES // cg):
            sel = jnp.where((src // cg == i + dst // LANES) & (src % cg == dst % cg)
                            & (dst % LANES // cg == gl), 1.0, 0.0).astype(BF16)
            part = _dot(y_ref[gl], sel)
            acc = part if acc is None else acc + part
        yt_ref[...] = acc

        def spread(tok, row, b, seq, i=i):
            o_ref[pl.ds(tok + i, b, stride=seq), :] = yt_ref[pl.ds(row, b), :LANES]
            o_ref[pl.ds(tok + i + 1, b, stride=seq), :] = yt_ref[pl.ds(row, b), LANES:]

        _chunk_row_loops(dm, spread)


def _s5_pack(dm, u_tok):
    t, w = u_tok.shape
    _, r = _s5_rows(dm)
    gps = LANES // S5_GROUP
    nq = S5_Q * S5_GROUP
    return pl.pallas_call(
        functools.partial(_s5_pack_kernel, dm),
        grid=(w // LANES,),
        in_specs=[pl.BlockSpec((t, LANES), lambda s: (0, s))],
        out_specs=pl.BlockSpec((gps, r, nq), lambda s: (s, 0, 0)),
        out_shape=jax.ShapeDtypeStruct((w // S5_GROUP, r, nq), BF16),
        scratch_shapes=[pltpu.VMEM((S5_Q, r, LANES), F32), pltpu.VMEM((S5_Q, r, LANES), BF16)],
        compiler_params=_cp(("parallel",)),
        name="s5_pack",
    )(u_tok)


def _s5_unpack(dm, y_g):
    g, r, nq = y_g.shape
    gps = LANES // S5_GROUP
    return pl.pallas_call(
        functools.partial(_s5_unpack_kernel, dm),
        grid=(g // gps,),
        in_specs=[pl.BlockSpec((gps, r, nq), lambda s: (s, 0, 0))],
        out_specs=pl.BlockSpec((dm.t, LANES), lambda s: (0, s)),
        out_shape=jax.ShapeDtypeStruct((dm.t, g * S5_GROUP), F32),
        scratch_shapes=[pltpu.VMEM((r, 2 * LANES), F32)],
        compiler_params=_cp(("parallel",)),
        name="s5_unpack",
    )(y_g)


def _s5(dm, u_tok, ops, d_skip, x0_re, x0_im):
    strip, wst, v, lam_pack = ops
    g = wst.shape[0]
    p = lam_pack.shape[2] // 2
    assert 2 * p == S5_LANES
    q, cg = S5_Q, S5_GROUP
    ncc, ncs = dm.lc // q, dm.ls // q
    rc, r = _s5_rows(dm)
    u_g = _s5_pack(dm, u_tok)
    d_g = jnp.tile(d_skip.astype(F32).reshape(g, 1, cg), (1, q, 1)).reshape(g, 1, q * cg)
    x0 = jnp.stack([jnp.concatenate([x0_re[:, 0], x0_re[:, 1]], axis=-1),
                    jnp.concatenate([x0_im[:, 0], x0_im[:, 1]], axis=-1)], axis=0)
    x0 = x0.transpose(2, 0, 1, 3).astype(F32)
    y_g, fin = pl.pallas_call(
        functools.partial(_s5_kernel, rc, ncc, dm.bc, ncs, dm.bs),
        grid=(g,),
        in_specs=[pl.BlockSpec((1, r, q * cg), lambda i: (i, 0, 0)),
                  pl.BlockSpec((1, cg, 2 * q * cg), lambda i: (i, 0, 0)),
                  pl.BlockSpec((1, q * cg, 2 * S5_LANES), lambda i: (i, 0, 0)),
                  pl.BlockSpec((1, 4, S5_LANES, q * cg), lambda i: (i, 0, 0, 0)),
                  pl.BlockSpec((1, 2, S5_LANES), lambda i: (i, 0, 0)),
                  pl.BlockSpec((1, 1, q * cg), lambda i: (i, 0, 0)),
                  pl.BlockSpec((1, 2, dm.bs, S5_LANES), lambda i: (i, 0, 0, 0))],
        out_specs=[pl.BlockSpec((1, r, q * cg), lambda i: (i, 0, 0)),
                   pl.BlockSpec((1, 2, dm.bc, S5_LANES), lambda i: (i, 0, 0, 0))],
        out_shape=[jax.ShapeDtypeStruct((g, r, q * cg), BF16),
                   jax.ShapeDtypeStruct((g, 2, dm.bc, S5_LANES), F32)],
        scratch_shapes=[pltpu.VMEM((q * cg, q * cg), BF16)] + [pltpu.VMEM((r, S5_LANES), F32)] * 6,
        compiler_params=_cp(("parallel",)),
        name="s5",
    )(u_g, strip, wst, v, lam_pack, d_g, x0)

    y_tok = _s5_unpack(dm, y_g)
    fin = fin.transpose(2, 1, 0, 3)
    fin_re = jnp.stack([fin[:, 0, :, :p], fin[:, 0, :, p:]], axis=1)
    fin_im = jnp.stack([fin[:, 1, :, :p], fin[:, 1, :, p:]], axis=1)
    return y_tok, fin_re, fin_im


def _glu_kernel(y_ref, w_ref, o_ref):
    z = _gelu_tanh(y_ref[...].astype(F32))
    o_ref[...] = (z * _sigmoid(_dot(z.astype(BF16), w_ref[...]))).astype(o_ref.dtype)


def _glu(dm, y, w):
    t, n = y.shape
    tm = dm.tm
    return pl.pallas_call(
        _glu_kernel,
        grid=(t // tm,),
        in_specs=[pl.BlockSpec((tm, n), lambda i: (i, 0)),
                  pl.BlockSpec((n, n), lambda i: (0, 0))],
        out_specs=pl.BlockSpec((tm, n), lambda i: (i, 0)),
        out_shape=jax.ShapeDtypeStruct((t, n), BF16),
        compiler_params=_cp(("parallel",)),
        name="s5_glu",
    )(y, w)


def _rot(x, ct, st):
    return x * ct + pltpu.roll(x, 64, 1) * st


def _mla_q_kernel(qscale, cq_ref, ckv_ref, kr_ref, qn_ref, kvn_ref, ct_ref, st_ref, w_ref,
                  q_ref, ckvn_ref, kro_ref):
    ct = ct_ref[...]
    st = st_ref[...]
    cq = cq_ref[...]
    cqn = (cq * lax.rsqrt(jnp.mean(cq * cq, axis=-1, keepdims=True) + EPS) * qn_ref[...]).astype(BF16)
    for h in range(MLA_HEADS):
        qh = _dot(cqn, w_ref[:, h * QK_PAD:(h + 1) * QK_PAD])
        q_ref[:, h * QK_PAD:h * QK_PAD + QK_NOPE] = (qh[:, :QK_NOPE] * qscale).astype(BF16)
        q_ref[:, h * QK_PAD + QK_NOPE:(h + 1) * QK_PAD] = (_rot(qh[:, QK_NOPE:], ct, st) * qscale).astype(BF16)
    ckv = ckv_ref[...]
    ckvn_ref[...] = ckv * lax.rsqrt(jnp.mean(ckv * ckv, axis=-1, keepdims=True) + EPS) * kvn_ref[...]
    kro_ref[...] = _rot(kr_ref[...], ct, st)


def _mla_q(dm, a, q_norm, kv_norm, ct, st, w_uq_ext, q_lora, kv_lora):
    t = a.shape[0]
    nq = w_uq_ext.shape[1]
    kr_blk = (q_lora + kv_lora) // 128
    tm = min(dm.tm, 512)
    qscale = float(QK_NOPE + QK_ROPE) ** -0.5 * math.log2(math.e)
    return pl.pallas_call(
        functools.partial(_mla_q_kernel, qscale),
        grid=(t // tm,),
        in_specs=[pl.BlockSpec((tm, q_lora), lambda i: (i, 0)),
                  pl.BlockSpec((tm, kv_lora), lambda i: (i, q_lora // kv_lora)),
                  pl.BlockSpec((tm, 128), lambda i: (i, kr_blk)),
                  pl.BlockSpec((1, q_lora), lambda i: (0, 0)),
                  pl.BlockSpec((1, kv_lora), lambda i: (0, 0)),
                  pl.BlockSpec((tm, 128), lambda i: (i, 0)),
                  pl.BlockSpec((tm, 128), lambda i: (i, 0)),
                  pl.BlockSpec((q_lora, nq), lambda i: (0, 0))],
        out_specs=[pl.BlockSpec((tm, nq), lambda i: (i, 0)),
                   pl.BlockSpec((tm, kv_lora), lambda i: (i, 0)),
                   pl.BlockSpec((tm, 128), lambda i: (i, 0))],
        out_shape=[jax.ShapeDtypeStruct((t, nq), BF16),
                   jax.ShapeDtypeStruct((t, kv_lora), F32),
                   jax.ShapeDtypeStruct((t, 128), F32)],
        compiler_params=_cp(("parallel",)),
        name="mla_q",
    )(a, a, a, q_norm, kv_norm, ct, st, w_uq_ext)


def _kv_expand_kernel(n_lat_tiles, per, cc_ref, ckr_ref, c_ref, kr_ref, wk_ref, wv_ref, k_ref, v_ref):
    t = pl.program_id(0)
    is_cache = jnp.logical_and(t < n_lat_tiles, t % per == 0)
    ckr = ckr_ref[0, 0].astype(F32)
    ckr = jnp.concatenate([ckr, jnp.zeros((ckr.shape[0], LANES - ckr.shape[1]), F32)], axis=1)
    c = jnp.where(is_cache, cc_ref[0, 0].astype(F32), c_ref[...]).astype(BF16)
    kr = jnp.where(is_cache, ckr, kr_ref[...]).astype(BF16)
    kn = _dot(c, wk_ref[...]).astype(BF16)
    vv = _dot(c, wv_ref[...]).astype(BF16)
    ones = jnp.ones((c.shape[0], V_PAD - V_HEAD), BF16)
    for h in range(MLA_HEADS):
        k_ref[:, h * QK_PAD:h * QK_PAD + QK_NOPE] = kn[:, h * QK_NOPE:(h + 1) * QK_NOPE]
        k_ref[:, h * QK_PAD + QK_NOPE:(h + 1) * QK_PAD] = kr
        v_ref[:, h * V_PAD:h * V_PAD + V_HEAD] = vv[:, h * V_HEAD:(h + 1) * V_HEAD]
        v_ref[:, h * V_PAD + V_HEAD:(h + 1) * V_PAD] = ones


def _kv_expand(dm, cache_ckv, cache_krope, j, ckv_n, kr, w_uk, w_uv):
    past, kv_lora = cache_ckv.shape[2:]
    tm = past
    assert dm.ls % tm == 0 and dm.tc % tm == 0
    per = 1 + dm.ls // tm
    n_lat = dm.bs * per
    nk = (n_lat + dm.tc // tm) * tm

    def own_blk(t):
        lat = dm.tc // tm + (t // per) * (dm.ls // tm) + jnp.maximum(t % per - 1, 0)
        return jnp.where(t < n_lat, lat, t - n_lat)

    def cache_idx(t):
        return (jnp.minimum(t // per, dm.bs - 1), j, 0, 0)

    return pl.pallas_call(
        functools.partial(_kv_expand_kernel, n_lat, per),
        grid=(nk // tm,),
        in_specs=[pl.BlockSpec((1, 1, tm, kv_lora), cache_idx),
                  pl.BlockSpec((1, 1, tm, cache_krope.shape[3]), cache_idx),
                  pl.BlockSpec((tm, kv_lora), lambda t: (own_blk(t), 0)),
                  pl.BlockSpec((tm, LANES), lambda t: (own_blk(t), 0)),
                  pl.BlockSpec(w_uk.shape, lambda t: (0, 0)),
                  pl.BlockSpec(w_uv.shape, lambda t: (0, 0))],
        out_specs=[pl.BlockSpec((tm, MLA_HEADS * QK_PAD), lambda t: (t, 0)),
                   pl.BlockSpec((tm, MLA_HEADS * V_PAD), lambda t: (t, 0))],
        out_shape=[jax.ShapeDtypeStruct((nk, MLA_HEADS * QK_PAD), BF16),
                   jax.ShapeDtypeStruct((nk, MLA_HEADS * V_PAD), BF16)],
        compiler_params=_cp(("parallel",)),
        name="mla_kv_expand",
    )(cache_ckv, cache_krope, ckv_n, kr, w_uk, w_uv)


def _attn_kernel(nh, nkb, q_ref, k_ref, v_ref, o_ref):
    kb = k_ref.shape[0] // nkb
    for h in range(nh):
        q = q_ref[:, h * QK_PAD:(h + 1) * QK_PAD]
        m, acc = None, None
        for j in range(nkb):
            s = _dot_nt(q, k_ref[j * kb:(j + 1) * kb, h * QK_PAD:(h + 1) * QK_PAD])
            mj = jnp.max(s, axis=-1, keepdims=True)
            m_new = mj if j == 0 else jnp.maximum(m, mj)
            pv = _dot(jnp.exp2(s - m_new).astype(BF16), v_ref[j * kb:(j + 1) * kb, h * V_PAD:(h + 1) * V_PAD])
            acc = pv if j == 0 else acc * jnp.exp2(m - m_new) + pv
            m = m_new
        o_ref[:, h * V_HEAD:(h + 1) * V_HEAD] = (acc[:, :V_HEAD] / acc[:, V_HEAD:V_HEAD + 1]).astype(o_ref.dtype)


def _attention(q, k, v, nb, lq, lk, tq, nh, q_row0, k_row0, name):
    nqt = lq // tq
    qb0, kb0 = q_row0 // tq, k_row0 // lk
    nkb = -(-lk // KB_MAX)
    assert lk % nkb == 0
    return pl.pallas_call(
        functools.partial(_attn_kernel, nh, nkb),
        grid=(nb, MLA_HEADS // nh, nqt),
        in_specs=[pl.BlockSpec((tq, nh * QK_PAD), lambda b, h, i: (qb0 + b * nqt + i, h)),
                  pl.BlockSpec((lk, nh * QK_PAD), lambda b, h, i: (kb0 + b, h)),
                  pl.BlockSpec((lk, nh * V_PAD), lambda b, h, i: (kb0 + b, h))],
        out_specs=pl.BlockSpec((tq, nh * V_HEAD), lambda b, h, i: (b * nqt + i, h)),
        out_shape=jax.ShapeDtypeStruct((nb * lq, MLA_HEADS * V_HEAD), BF16),
        compiler_params=_cp(("parallel", "parallel", "arbitrary")),
        name=name,
    )(q, k, v)


def _route_kernel(n_exp, x_ref, g_ref, sh_ref, sc_ref, rw_ref, rb_ref, tri_ref,
                  hp_ref, idx_ref, rank_ref, wt_ref, cnt_ref, run_ref):
    @pl.when(pl.program_id(0) == 0)
    def _():
        run_ref[...] = jnp.zeros_like(run_ref)

    h = _norm_mod(x_ref[...], g_ref[0], sc_ref[0], sh_ref[0])
    _store_row_tiles(hp_ref, _pack_halves(h))
    scores =_sigmoid(_dot_nt(rw_ref[...], h.astype(BF16)))
    biased = scores + rb_ref[...]
    per = n_exp // N_EXPERT_GROUPS
    assert per == 4
    rows_b = [biased[e:e + 1, :] for e in range(n_exp)]
    rows_s = [scores[e:e + 1, :] for e in range(n_exp)]
    best_sum, best_g = None, None
    for gi in range(N_EXPERT_GROUPS):
        a, b, c, d = rows_b[per * gi:per * gi + per]
        hi1, lo1 = jnp.maximum(a, b), jnp.minimum(a, b)
        hi2, lo2 = jnp.maximum(c, d), jnp.minimum(c, d)
        top2 = jnp.maximum(hi1, hi2) + jnp.maximum(jnp.minimum(hi1, hi2), jnp.maximum(lo1, lo2))
        if gi == 0:
            best_sum, best_g = top2, jnp.zeros_like(top2, dtype=jnp.int32)
        else:
            upd = top2 > best_sum
            best_sum = jnp.where(upd, top2, best_sum)
            best_g = jnp.where(upd, gi, best_g)
    vb, vs = [], []
    for k in range(per):
        accb, accs = rows_b[k], rows_s[k]
        for gi in range(1, N_EXPERT_GROUPS):
            sel = best_g == gi
            accb = jnp.where(sel, rows_b[per * gi + k], accb)
            accs = jnp.where(sel, rows_s[per * gi + k], accs)
        vb.append(accb)
        vs.append(accs)

    def first_argmax(vals, exclude):
        bv, bi, bs = None, None, None
        for k in range(per):
            v = vals[k] if exclude is None else jnp.where(exclude == k, -jnp.inf, vals[k])
            if k == 0:
                bv, bi, bs = v, jnp.zeros_like(best_g), vs[0]
            else:
                upd = v > bv
                bv = jnp.where(upd, v, bv)
                bi = jnp.where(upd, k, bi)
                bs = jnp.where(upd, vs[k], bs)
        return bi, bs

    i1, s1 = first_argmax(vb, None)
    i2, s2 = first_argmax(vb, i1)
    tot = s1 + s2
    e1 = best_g * per + i1
    e2 = best_g * per + i2
    idx_ref[0:1, :] = e1
    idx_ref[1:2, :] = e2
    wt_ref[0:1, :] = s1 / tot
    wt_ref[1:2, :] = s2 / tot
    eids = lax.broadcasted_iota(jnp.int32, (n_exp, e1.shape[1]), 0)
    hit1 = eids == e1
    hit2 = eids == e2
    oh1 = jnp.where(hit1, 1.0, 0.0)
    oh2 = jnp.where(hit2, 1.0, 0.0)
    p1 = _dot(oh1.astype(BF16), tri_ref[...])
    p2 = _dot(oh2.astype(BF16), tri_ref[...])
    c1 = jnp.sum(oh1, axis=1, keepdims=True)
    c2 = jnp.sum(oh2, axis=1, keepdims=True)
    run = run_ref[...]
    rank_ref[0:1, :] = jnp.sum(jnp.where(hit1, run + p1, 0.0), axis=0, keepdims=True).astype(jnp.int32)
    rank_ref[1:2, :] = jnp.sum(jnp.where(hit2, run + c1 + p2, 0.0), axis=0, keepdims=True).astype(jnp.int32)
    run_ref[...] = run + c1 + c2
    cnt_ref[...] = run + c1 + c2


def _route(dm, x, gain, mods, layer, rw_t, rb):
    t, d = x.shape
    n_exp = rw_t.shape[0]
    tm = dm.tm
    rt = d // 2 // LANES
    tri = (jnp.arange(tm)[:, None] < jnp.arange(tm)[None, :]).astype(BF16)
    return pl.pallas_call(
        functools.partial(_route_kernel, n_exp),
        grid=(t // tm,),
        in_specs=[pl.BlockSpec((tm, d), lambda i: (i, 0)),
                  pl.BlockSpec((1, 1, d), lambda i: (layer, 0, 0)),
                  pl.BlockSpec((1, 1, d), _mod_idx(dm, layer, 3, tm)),
                  pl.BlockSpec((1, 1, d), _mod_idx(dm, layer, 4, tm)),
                  pl.BlockSpec((n_exp, d), lambda i: (0, 0)),
                  pl.BlockSpec((n_exp, 1), lambda i: (0, 0)),
                  pl.BlockSpec((tm, tm), lambda i: (0, 0))],
        out_specs=[pl.BlockSpec((tm * rt, LANES), lambda i: (i, 0)),
                   pl.BlockSpec((2, tm), lambda i: (0, i)),
                   pl.BlockSpec((2, tm), lambda i: (0, i)),
                   pl.BlockSpec((2, tm), lambda i: (0, i)),
                   pl.BlockSpec((n_exp, 1), lambda i: (0, 0))],
        out_shape=[jax.ShapeDtypeStruct((t * rt, LANES), U32),
                   jax.ShapeDtypeStruct((2, t), jnp.int32),
                   jax.ShapeDtypeStruct((2, t), jnp.int32),
                   jax.ShapeDtypeStruct((2, t), F32),
                   jax.ShapeDtypeStruct((n_exp, 1), F32)],
        scratch_shapes=[pltpu.VMEM((n_exp, 1), F32)],
        compiler_params=_cp(("arbitrary",)),
        name="moe_route",
    )(x, gain, mods, mods, rw_t, rb, tri)


def _slot_plan(idx, rank, counts, n_exp, n_slots):
    counts = counts.reshape(n_exp).astype(jnp.int32)
    padded = ((counts + TM_MOE - 1) // TM_MOE) * TM_MOE
    ends = jnp.cumsum(padded)
    starts = ends - padded
    start_of = jnp.sum(jnp.where(idx[..., None] == jnp.arange(n_exp, dtype=jnp.int32), starts, 0), axis=-1)
    dest = (start_of + rank).reshape(-1).astype(jnp.int32)
    tile_start = jnp.arange(n_slots // TM_MOE, dtype=jnp.int32) * TM_MOE
    tile_exp = jnp.minimum(jnp.sum((ends[None, :] <= tile_start[:, None]).astype(jnp.int32), axis=1), n_exp - 1)
    tile_ok = (tile_start < ends[-1]).astype(jnp.int32)
    pad_lo = jnp.concatenate([starts + counts, ends[-1:]]).astype(jnp.int32)
    pad_n = (padded - counts).astype(jnp.int32)
    return dest, tile_exp.astype(jnp.int32), tile_ok, pad_lo, pad_n


def _scatter_kernel(n_exp, t_total, rows, rt, dest_ref, padlo_ref, padn_ref, hp_ref, xs_ref, zrow, sem):
    i = pl.program_id(0)
    base = i * rows
    tile_rows = TM_MOE * rt

    def slot(s):
        return xs_ref.at[pl.ds(pl.multiple_of(s * rt, rt), rt)]

    @pl.when(i == 0)
    def _():
        zrow[...] = jnp.zeros_like(zrow)
        for e in range(n_exp):
            lo = padlo_ref[e]

            def pad_copy(r, lo=lo):
                return pltpu.make_async_copy(zrow.at[pl.ds(0, rt)], slot(lo + r), sem)

            def start(r, c, pad_copy=pad_copy):
                pad_copy(r).start()
                return c

            def wait(r, c, pad_copy=pad_copy):
                pad_copy(r).wait()
                return c

            lax.fori_loop(0, padn_ref[e], start, 0)
            lax.fori_loop(0, padn_ref[e], wait, 0)

        tail0 = padlo_ref[n_exp]

        def tail_copy(k):
            row = pl.multiple_of((tail0 + k * TM_MOE) * rt, tile_rows)
            return pltpu.make_async_copy(zrow, xs_ref.at[pl.ds(row, tile_rows)], sem)

        def tail_start(k, c):
            tail_copy(k).start()
            return c

        def tail_wait(k, c):
            tail_copy(k).wait()
            return c

        n_tail = (xs_ref.shape[0] // rt - tail0) // TM_MOE
        lax.fori_loop(0, n_tail, tail_start, 0)
        lax.fori_loop(0, n_tail, tail_wait, 0)

    def copies(r):
        src = hp_ref.at[pl.ds(pl.multiple_of(r * rt, rt), rt)]
        return (pltpu.make_async_copy(src, slot(dest_ref[base + r]), sem),
                pltpu.make_async_copy(src, slot(dest_ref[t_total + base + r]), sem))

    def start(gi, c):
        for u in range(DMA_UNROLL):
            for prio, cp in enumerate(copies(gi * DMA_UNROLL + u)):
                cp.start(priority=prio)
        return c

    def wait(gi, c):
        for u in range(DMA_UNROLL):
            for cp in copies(gi * DMA_UNROLL + u):
                cp.wait()
        return c

    lax.fori_loop(0, rows // DMA_UNROLL, start, 0)
    lax.fori_loop(0, rows // DMA_UNROLL, wait, 0)


def _scatter_rows(hp, dest, pad_lo, pad_n, n_slots, rows, rt):
    t = hp.shape[0] // rt
    n_exp = pad_n.shape[0]
    return pl.pallas_call(
        functools.partial(_scatter_kernel, n_exp, t, rows, rt),
        grid_spec=pltpu.PrefetchScalarGridSpec(
            num_scalar_prefetch=3,
            grid=(t // rows,),
            in_specs=[pl.BlockSpec((rows * rt, LANES), lambda i, *_: (i, 0))],
            out_specs=pl.BlockSpec(memory_space=pl.ANY),
            scratch_shapes=[pltpu.VMEM((TM_MOE * rt, LANES), U32), pltpu.SemaphoreType.DMA(())]),
        out_shape=jax.ShapeDtypeStruct((n_slots * rt, LANES), U32),
        compiler_params=_cp(("arbitrary",)),
        name="moe_scatter",
    )(dest, pad_lo, pad_n, hp)


def _expert_kernel(rt, te_ref, ok_ref, x_ref, wg_ref, wu_ref, wd_ref, o_ref, wg_b, wu_b, wd_b):
    i = pl.program_id(0)
    new_expert = jnp.logical_or(i == 0, te_ref[i] != te_ref[jnp.maximum(i - 1, 0)])

    @pl.when(new_expert)
    def _():
        wg_b[...] = wg_ref[0, 0].astype(BF16)
        wu_b[...] = wu_ref[0, 0].astype(BF16)
        wd_b[...] = wd_ref[0, 0].astype(BF16)

    @pl.when(ok_ref[i] == 1)
    def _():
        x_lo, x_hi = _unpack_halves(_load_row_tiles(x_ref, rt))
        x_lo, x_hi = x_lo.astype(BF16), x_hi.astype(BF16)
        half = x_lo.shape[1]
        h1 = _dot(x_lo, wg_b[:half, :]) + _dot(x_hi, wg_b[half:, :])
        h2 = _dot(x_lo, wu_b[:half, :]) + _dot(x_hi, wu_b[half:, :])
        act = (_silu(h1) * h2).astype(BF16)
        _store_row_tiles(o_ref, _pack_halves(_dot(act, wd_b[...])))

    @pl.when(ok_ref[i] == 0)
    def _():
        o_ref[...] = jnp.zeros_like(o_ref)


def _experts(x_sorted, tile_exp, tile_ok, w_gate, w_up, w_down, layer, rt):
    n_slots = x_sorted.shape[0] // rt
    d, f = w_gate.shape[-2:]
    return pl.pallas_call(
        functools.partial(_expert_kernel, rt),
        grid_spec=pltpu.PrefetchScalarGridSpec(
            num_scalar_prefetch=2,
            grid=(n_slots // TM_MOE,),
            in_specs=[pl.BlockSpec((TM_MOE * rt, LANES), lambda i, te, ok: (i, 0)),
                      pl.BlockSpec((1, 1, d, f), lambda i, te, ok: (layer, te[i], 0, 0)),
                      pl.BlockSpec((1, 1, d, f), lambda i, te, ok: (layer, te[i], 0, 0)),
                      pl.BlockSpec((1, 1, f, d), lambda i, te, ok: (layer, te[i], 0, 0))],
            out_specs=pl.BlockSpec((TM_MOE * rt, LANES), lambda i, te, ok: (i, 0)),
            scratch_shapes=[pltpu.VMEM((d, f), BF16), pltpu.VMEM((d, f), BF16), pltpu.VMEM((f, d), BF16)]),
        out_shape=jax.ShapeDtypeStruct((n_slots * rt, LANES), U32),
        compiler_params=_cp(("arbitrary",)),
        name="moe_experts",
    )(tile_exp, tile_ok, x_sorted, w_gate, w_up, w_down)


def _combine_kernel(t_total, rows, rt, dest_ref, x_ref, gate_ref, wt_ref, y_ref, o_ref, y0, y1, sem):
    base = pl.program_id(0) * rows

    def slot(s):
        return y_ref.at[pl.ds(pl.multiple_of(s * rt, rt), rt)]

    def copies(r):
        dst = pl.ds(pl.multiple_of(r * rt, rt), rt)
        return (pltpu.make_async_copy(slot(dest_ref[base + r]), y0.at[dst], sem),
                pltpu.make_async_copy(slot(dest_ref[t_total + base + r]), y1.at[dst], sem))

    def start(gi, c):
        for u in range(DMA_UNROLL):
            for prio, cp in enumerate(copies(gi * DMA_UNROLL + u)):
                cp.start(priority=prio)
        return c

    def wait(gi, c):
        for u in range(DMA_UNROLL):
            for cp in copies(gi * DMA_UNROLL + u):
                cp.wait()
        return c

    lax.fori_loop(0, rows // DMA_UNROLL, start, 0)
    lax.fori_loop(0, rows // DMA_UNROLL, wait, 0)
    w0 = wt_ref[:, 0:1]
    w1 = wt_ref[:, 1:2]
    a_lo, a_hi = _unpack_halves(_load_row_tiles(y0, rt))
    b_lo, b_hi = _unpack_halves(_load_row_tiles(y1, rt))
    half = a_lo.shape[1]
    gate = gate_ref[0]
    o_ref[:, :half] = x_ref[:, :half] + gate[:, :half] * (w0 * a_lo + w1 * b_lo)
    o_ref[:, half:] = x_ref[:, half:] + gate[:, half:] * (w0 * a_hi + w1 * b_hi)


def _combine(dm, x, mods, layer, y_sorted, dest, wts_t):
    t, d = x.shape
    rows = min(dm.tm, 512)
    rt = d // 2 // LANES
    return pl.pallas_call(
        functools.partial(_combine_kernel, t, rows, rt),
        grid_spec=pltpu.PrefetchScalarGridSpec(
            num_scalar_prefetch=1,
            grid=(t // rows,),
            in_specs=[pl.BlockSpec((rows, d), lambda i, dst: (i, 0)),
                      pl.BlockSpec((1, 1, d), lambda i, dst: _mod_idx(dm, layer, 5, rows)(i)),
                      pl.BlockSpec((rows, 2), lambda i, dst: (i, 0)),
                      pl.BlockSpec(memory_space=pl.ANY)],
            out_specs=pl.BlockSpec((rows, d), lambda i, dst: (i, 0)),
            scratch_shapes=[pltpu.VMEM((rows * rt, LANES), U32), pltpu.VMEM((rows * rt, LANES), U32),
                            pltpu.SemaphoreType.DMA(())]),
        out_shape=jax.ShapeDtypeStruct((t, d), F32),
        compiler_params=_cp(("arbitrary",)),
        name="moe_combine",
    )(dest, x, mods, wts_t, y_sorted)


def _moe(dm, x, gain, mods, layer, rw_t, rb, w_gate, w_up, w_down):
    n_exp = rw_t.shape[0]
    hp, idx, rank, wts, counts = _route(dm, x, gain, mods, layer, rw_t, rb)
    n_slots = 2 * dm.t + n_exp * TM_MOE
    dest, tile_exp, tile_ok, pad_lo, pad_n = _slot_plan(idx, rank, counts, n_exp, n_slots)
    rt = dm.d // 2 // LANES
    x_sorted = _scatter_rows(hp, dest, pad_lo, pad_n, n_slots, min(dm.tm, 512), rt)
    y_sorted = _experts(x_sorted, tile_exp, tile_ok, w_gate, w_up, w_down, layer, rt)
    return _combine(dm, x, mods, layer, y_sorted, dest, wts.T)


def _final_norm_kernel(x_ref, g_ref, o_ref):
    x = x_ref[...]
    o_ref[...] = x * lax.rsqrt(jnp.mean(x * x, axis=-1, keepdims=True) + EPS) * g_ref[...]


def _final_norm(dm, x, gain, row0, rows):
    d = x.shape[1]
    tm = dm.tm
    blk0 = row0 // tm
    return pl.pallas_call(
        _final_norm_kernel,
        grid=(rows // tm,),
        in_specs=[pl.BlockSpec((tm, d), lambda i: (blk0 + i, 0)),
                  pl.BlockSpec((1, d), lambda i: (0, 0))],
        out_specs=pl.BlockSpec((tm, d), lambda i: (i, 0)),
        out_shape=jax.ShapeDtypeStruct((rows, d), F32),
        compiler_params=_cp(("parallel",)),
        name="final_norm",
    )(x, gain.reshape(1, d))


def _rope_tables(dm):
    rows = dm.ls // GRID_W
    row = jnp.repeat(jnp.arange(rows, dtype=F32), GRID_W)
    col = jnp.tile(jnp.arange(GRID_W, dtype=F32), rows)
    half = QK_ROPE // 2
    freqs = jnp.power(ROPE_BASE, -jnp.arange(0, half, 2, dtype=F32) / half)
    ar, ac = row[:, None] * freqs, col[:, None] * freqs
    zeros = jnp.zeros((dm.ls, 128 - QK_ROPE), F32)
    ct = jnp.concatenate([jnp.cos(ar), jnp.cos(ar), jnp.cos(ac), jnp.cos(ac), zeros], axis=-1)
    st = jnp.concatenate([-jnp.sin(ar), jnp.sin(ar), -jnp.sin(ac), jnp.sin(ac), zeros], axis=-1)
    ct_c = jnp.concatenate([jnp.ones((dm.tc, QK_ROPE), F32), jnp.zeros((dm.tc, 128 - QK_ROPE), F32)], axis=-1)
    ct = jnp.concatenate([ct_c, jnp.tile(ct, (dm.bs, 1))], axis=0)
    st = jnp.concatenate([jnp.zeros((dm.tc, 128), F32), jnp.tile(st, (dm.bs, 1))], axis=0)
    return ct, st


def _swap_halves_cols(w):
    qt = QK_ROPE // 4
    return jnp.concatenate([w[..., qt:2 * qt], w[..., :qt], w[..., 3 * qt:], w[..., 2 * qt:3 * qt]], axis=-1)


def kernel(x_prompt, x_sample, c, state_ret, state_s5_re, state_s5_im, cache_ckv, cache_krope, c_ctx, ada_w, ada_b, norm_mix, norm_ffn, norm_final, even_w_in, even_w_out, ret_decay, s5_a_re, s5_a_im, s5_log_dt, s5_b_re, s5_b_im, s5_c_re, s5_c_im, s5_d, s5_w_glu, mla_w_in, mla_q_norm, mla_w_uq, mla_kv_norm, mla_w_ukv, mla_w_out, router_w, router_bias, moe_w_gate, moe_w_up, moe_w_down):
    dm = _Dims(x_prompt, x_sample)
    d = dm.d
    depth = ada_w.shape[0]
    n_exp = router_w.shape[1]
    past = cache_ckv.shape[2]
    q_lora = mla_q_norm.shape[1]
    kv_lora = mla_kv_norm.shape[1]

    x = jnp.concatenate([x_prompt.reshape(dm.tc, d), x_sample.reshape(dm.ts, d)], axis=0)
    cond = jnp.zeros((dm.rp, d), F32).at[0].set(c_ctx).at[1:1 + dm.bs].set(c)
    mods = _adaln(cond, ada_w, ada_b, tn=d * N_MOD // 8).reshape(depth * dm.rp * N_MOD, 1, d)
    g_mix = norm_mix.reshape(depth, 1, d)
    g_ffn = norm_ffn.reshape(depth, 1, d)
    rw_t = router_w.T.astype(BF16)
    rb = router_bias.astype(F32).reshape(n_exp, 1)
    ct, st = _rope_tables(dm)
    s5_ops = jax.vmap(_s5_operators)(s5_a_re, s5_a_im, s5_log_dt, s5_b_re, s5_b_im, s5_c_re, s5_c_im)

    rets, s5r, s5i, ckvs, krs = [], [], [], [], []
    for layer in range(depth):
        if layer % 2 == 0:
            i = layer // 2
            ret_w = even_w_out.shape[1] - s5_d.shape[1]
            s5_w = s5_d.shape[1]
            z, u = _norm_mod_matmul_split(dm, x, g_mix, mods, layer, 0, even_w_in[i].astype(BF16), tn=s5_w,
                                          name="even_in_proj")
            log_gamma = -jnp.exp(ret_decay[i].astype(F32))
            ro_c, sfin = _retention(z, ret_w // RET_HEADS, log_gamma, None, dm.bc, dm.lc, 0, "retention_ctx")
            ro_s, _ = _retention(z, ret_w // RET_HEADS, log_gamma, (state_ret, i), dm.bs, dm.ls,
                                 dm.tc // dm.ls, "retention_lat")
            ops = tuple(op[i] for op in s5_ops)
            y, f_re, f_im = _s5(dm, u, ops, s5_d[i], state_s5_re[:, i], state_s5_im[:, i])
            s5_out = _glu(dm, y, s5_w_glu[i].astype(BF16))
            w_out = even_w_out[i].astype(BF16)
            x = _matmul_residual(dm, x, mods, layer, 2, [(ro_c, ro_s), s5_out], [w_out[:ret_w], w_out[ret_w:]],
                                 tn=d // 2, name="even_out_proj")
            rets.append(sfin)
            s5r.append(f_re)
            s5i.append(f_im)
        else:
            j = layer // 2
            w_in = mla_w_in[j]
            w_in_ext = jnp.concatenate([w_in, _swap_halves_cols(w_in[:, q_lora + kv_lora:])], axis=1).astype(BF16)
            a = _norm_mod_matmul(dm, x, g_mix, mods, layer, 0, w_in_ext, F32, tn=w_in_ext.shape[1],
                                 name="mla_in_proj")
            w_uq = mla_w_uq[j].reshape(q_lora, MLA_HEADS, QK_NOPE + QK_ROPE)
            w_uq_ext = jnp.concatenate([w_uq, _swap_halves_cols(w_uq[..., QK_NOPE:])], axis=-1)
            w_uq_ext = w_uq_ext.reshape(q_lora, MLA_HEADS * QK_PAD).astype(BF16)
            q, ckv_n, kr = _mla_q(dm, a, mla_q_norm[j].reshape(1, q_lora), mla_kv_norm[j].reshape(1, kv_lora),
                                  ct, st, w_uq_ext, q_lora, kv_lora)
            lk = past + dm.ls
            w_ukv = mla_w_ukv[j].reshape(kv_lora, MLA_HEADS, QK_NOPE + V_HEAD)
            w_uk = w_ukv[..., :QK_NOPE].reshape(kv_lora, MLA_HEADS * QK_NOPE).astype(BF16)
            w_uv = w_ukv[..., QK_NOPE:].reshape(kv_lora, MLA_HEADS * V_HEAD).astype(BF16)
            k_all, v_all = _kv_expand(dm, cache_ckv, cache_krope, j, ckv_n, kr, w_uk, w_uv)
            o_c = _attention(q, k_all, v_all, dm.bc, dm.lc, dm.lc, dm.lc, MLA_HEADS, 0, dm.bs * lk, "attn_ctx")
            o_s = _attention(q, k_all, v_all, dm.bs, dm.ls, lk, min(TQ_MAX, dm.ls), LAT_HEADS_PER_STEP, dm.tc, 0,
                             "attn_lat")
            x = _matmul_residual(dm, x, mods, layer, 2, [(o_c, o_s)], [mla_w_out[j].astype(BF16)], tn=d // 2,
                                 name="mla_out_proj")
            ckvs.append(ckv_n[:dm.tc].reshape(dm.bc, dm.lc, kv_lora))
            krs.append(kr[:dm.tc, :QK_ROPE].reshape(dm.bc, dm.lc, QK_ROPE))
        x = _moe(dm, x, g_ffn, mods, layer, rw_t, rb, moe_w_gate, moe_w_up, moe_w_down)

    y_prompt = _final_norm(dm, x, norm_final, 0, dm.tc).reshape(dm.bc, dm.lc, d)
    y_sample = _final_norm(dm, x, norm_final, dm.tc, dm.ts).reshape(dm.bs, dm.ls, d)
    return (y_prompt, y_sample, jnp.stack(rets, axis=1), jnp.stack(s5r, axis=1), jnp.stack(s5i, axis=1),
            jnp.stack(ckvs, axis=1), jnp.stack(krs, axis=1))
```

```python
import functools
import math

import jax
import jax.numpy as jnp
import numpy as np
from jax import lax
from jax.experimental import pallas as pl
from jax.experimental.pallas import tpu as pltpu

F32 = jnp.float32
BF16 = jnp.bfloat16
U32 = jnp.uint32
EPS = 1e-6

RET_HEADS = 4
RET_CHUNK = 128
RET_UNROLL = 4
S5_GROUP = 16
S5_Q = 16
S5_LANES = 128
LANES = 128
MLA_HEADS = 16
QK_NOPE = 128
QK_ROPE = 64
V_HEAD = 128
QK_PAD = 256
V_PAD = 256
GRID_W = 64
ROPE_BASE = 10000.0
N_EXPERT_GROUPS = 4
N_MOD = 6

VMEM_LIMIT = 56 * 1024 * 1024
TM_MAX = 1024
TQ_MAX = 512
KB_MAX = 768
LAT_HEADS_PER_STEP = 8
TM_MOE = 256
DMA_UNROLL = 8
HI_MASK = np.uint32(0xFFFF0000)


def _cp(sem, vmem=VMEM_LIMIT):
    return pltpu.CompilerParams(dimension_semantics=sem, vmem_limit_bytes=vmem)


def _sigmoid(x):
    return 1.0 / (1.0 + jnp.exp(-x))


def _silu(x):
    return x * _sigmoid(x)


def _gelu_tanh(x):
    return 0.5 * x * (1.0 + jnp.tanh(math.sqrt(2.0 / math.pi) * (x + 0.044715 * (x * x * x))))


def _dot(a, b):
    return jnp.dot(a, b, preferred_element_type=F32)


def _dot_nt(a, b):
    return lax.dot_general(a, b, (((1,), (1,)), ((), ())), preferred_element_type=F32)


def _dot_tn(a, b):
    return lax.dot_general(a, b, (((0,), (0,)), ((), ())), preferred_element_type=F32)


def _pack_halves(x):
    half = x.shape[1] // 2
    xb = x.astype(BF16).astype(F32)
    lo = lax.bitcast_convert_type(xb[:, :half], U32) >> 16
    hi = lax.bitcast_convert_type(xb[:, half:], U32) & HI_MASK
    return hi | lo


def _unpack_halves(w):
    return (lax.bitcast_convert_type(w << 16, F32), lax.bitcast_convert_type(w & HI_MASK, F32))


def _store_row_tiles(ref, w):
    r, rt = w.shape[0], w.shape[1] // LANES
    for s in range(rt):
        ref[pl.ds(s, r, stride=rt), :] = w[:, s * LANES:(s + 1) * LANES]


def _load_row_tiles(ref, rt):
    r = ref.shape[0] // rt
    return jnp.concatenate([ref[pl.ds(s, r, stride=rt), :] for s in range(rt)], axis=1)


class _Dims:
    def __init__(self, x_prompt, x_sample):
        self.bc, self.lc, self.d = x_prompt.shape
        self.bs, self.ls, _ = x_sample.shape
        self.tc = self.bc * self.lc
        self.ts = self.bs * self.ls
        self.t = self.tc + self.ts
        self.rp = -(-(1 + self.bs) // 8) * 8
        self.tm = min(TM_MAX, self.ls)
        assert self.tc % self.tm == 0 and self.ls % self.tm == 0

    def mod_row(self, i, tm):
        nct = self.tc // tm
        return jnp.where(i < nct, 0, 1 + (i - nct) // (self.ls // tm))


def _mod_idx(dm, layer, k, tm):
    def idx(i, *_):
        return ((layer * dm.rp + dm.mod_row(i, tm)) * N_MOD + k, 0, 0)
    return idx


def _adaln_kernel(c_ref, w_ref, b_ref, o_ref):
    cs = _silu(c_ref[...]).astype(BF16)
    o_ref[0] = _dot(cs, w_ref[0].astype(BF16)) + b_ref[0]


def _adaln(cond, ada_w, ada_b, tn):
    depth, d, n = ada_w.shape
    rp = cond.shape[0]
    return pl.pallas_call(
        _adaln_kernel,
        grid=(depth, n // tn),
        in_specs=[pl.BlockSpec((rp, d), lambda l, j: (0, 0)),
                  pl.BlockSpec((1, d, tn), lambda l, j: (l, 0, j)),
                  pl.BlockSpec((1, 1, tn), lambda l, j: (l, 0, j))],
        out_specs=pl.BlockSpec((1, rp, tn), lambda l, j: (l, 0, j)),
        out_shape=jax.ShapeDtypeStruct((depth, rp, n), F32),
        compiler_params=_cp(("parallel", "parallel")),
        name="adaln",
    )(cond, ada_w, ada_b.reshape(depth, 1, n))


def _norm_mod(x, g, sc, sh):
    y = x * lax.rsqrt(jnp.mean(x * x, axis=-1, keepdims=True) + EPS) * g
    return y * (1.0 + sc) + sh


def _nmm_kernel(x_ref, g_ref, sh_ref, sc_ref, w_ref, o_ref, hn_ref):
    @pl.when(pl.program_id(1) == 0)
    def _():
        hn_ref[...] = _norm_mod(x_ref[...], g_ref[0], sc_ref[0], sh_ref[0]).astype(BF16)

    o_ref[...] = _dot(hn_ref[...], w_ref[...]).astype(o_ref.dtype)


def _norm_mod_matmul(dm, x, gain, mods, layer, k_shift, w, out_dtype, tn, name):
    t, d = x.shape
    n = w.shape[1]
    tm = dm.tm
    return pl.pallas_call(
        _nmm_kernel,
        grid=(t // tm, n // tn),
        in_specs=[pl.BlockSpec((tm, d), lambda i, j: (i, 0)),
                  pl.BlockSpec((1, 1, d), lambda i, j: (layer, 0, 0)),
                  pl.BlockSpec((1, 1, d), _mod_idx(dm, layer, k_shift, tm)),
                  pl.BlockSpec((1, 1, d), _mod_idx(dm, layer, k_shift + 1, tm)),
                  pl.BlockSpec((d, tn), lambda i, j: (0, j))],
        out_specs=pl.BlockSpec((tm, tn), lambda i, j: (i, j)),
        out_shape=jax.ShapeDtypeStruct((t, n), out_dtype),
        scratch_shapes=[pltpu.VMEM((tm, d), BF16)],
        compiler_params=_cp(("parallel", "arbitrary")),
        name=name,
    )(x, gain, mods, mods, w)


def _nmm_split_kernel(n_main, x_ref, g_ref, sh_ref, sc_ref, w_ref, o_main_ref, o_tail_ref, hn_ref):
    j = pl.program_id(1)

    @pl.when(j == 0)
    def _():
        hn_ref[...] = _norm_mod(x_ref[...], g_ref[0], sc_ref[0], sh_ref[0]).astype(BF16)

    acc = _dot(hn_ref[...], w_ref[...])

    @pl.when(j < n_main)
    def _():
        o_main_ref[...] = acc.astype(o_main_ref.dtype)

    @pl.when(j >= n_main)
    def _():
        o_tail_ref[...] = acc


def _norm_mod_matmul_split(dm, x, gain, mods, layer, k_shift, w, tn, name):
    t, d = x.shape
    n = w.shape[1]
    tm = dm.tm
    n_main = n // tn - 1
    return pl.pallas_call(
        functools.partial(_nmm_split_kernel, n_main),
        grid=(t // tm, n // tn),
        in_specs=[pl.BlockSpec((tm, d), lambda i, j: (i, 0)),
                  pl.BlockSpec((1, 1, d), lambda i, j: (layer, 0, 0)),
                  pl.BlockSpec((1, 1, d), _mod_idx(dm, layer, k_shift, tm)),
                  pl.BlockSpec((1, 1, d), _mod_idx(dm, layer, k_shift + 1, tm)),
                  pl.BlockSpec((d, tn), lambda i, j: (0, j))],
        out_specs=[pl.BlockSpec((tm, tn), lambda i, j: (i, jnp.minimum(j, n_main - 1))),
                   pl.BlockSpec((tm, tn), lambda i, j: (i, 0))],
        out_shape=[jax.ShapeDtypeStruct((t, n_main * tn), BF16),
                   jax.ShapeDtypeStruct((t, tn), F32)],
        scratch_shapes=[pltpu.VMEM((tm, d), BF16)],
        compiler_params=_cp(("parallel", "arbitrary")),
        name=name,
    )(x, gain, mods, mods, w)


def _mmres_kernel(split, nct, x_ref, gate_ref, *refs):
    n_a = sum(2 if sp else 1 for sp in split)
    a_refs, w_refs, o_ref = refs[:n_a], refs[n_a:n_a + len(split)], refs[n_a + len(split)]

    def run(use_ctx):
        acc, pos = None, 0
        for sp, w_ref in zip(split, w_refs):
            a_ref = a_refs[pos if (use_ctx or not sp) else pos + 1]
            pos += 2 if sp else 1
            part = _dot(a_ref[...], w_ref[...])
            acc = part if acc is None else acc + part
        o_ref[...] = x_ref[...] + gate_ref[0] * acc

    if any(split):
        is_ctx = pl.program_id(0) < nct
        pl.when(is_ctx)(lambda: run(True))
        pl.when(jnp.logical_not(is_ctx))(lambda: run(False))
    else:
        run(True)


def _matmul_residual(dm, x, mods, layer, k_gate, acts, ws, tn, name):
    t, d = x.shape
    tm = dm.tm
    nct = dm.tc // tm
    split = tuple(isinstance(a, tuple) for a in acts)

    def gate_idx(i, j):
        return ((layer * dm.rp + dm.mod_row(i, tm)) * N_MOD + k_gate, 0, j)

    in_specs = [pl.BlockSpec((tm, tn), lambda i, j: (i, j)),
                pl.BlockSpec((1, 1, tn), gate_idx)]
    flat = []
    for a in acts:
        if isinstance(a, tuple):
            in_specs.append(pl.BlockSpec((tm, a[0].shape[1]), lambda i, j: (jnp.minimum(i, nct - 1), 0)))
            in_specs.append(pl.BlockSpec((tm, a[1].shape[1]), lambda i, j: (jnp.maximum(i - nct, 0), 0)))
            flat += list(a)
        else:
            in_specs.append(pl.BlockSpec((tm, a.shape[1]), lambda i, j: (i, 0)))
            flat.append(a)
    in_specs += [pl.BlockSpec((w.shape[0], tn), lambda i, j: (0, j)) for w in ws]
    return pl.pallas_call(
        functools.partial(_mmres_kernel, split, nct),
        grid=(t // tm, d // tn),
        in_specs=in_specs,
        out_specs=pl.BlockSpec((tm, tn), lambda i, j: (i, j)),
        out_shape=jax.ShapeDtypeStruct((t, d), F32),
        compiler_params=_cp(("parallel", "parallel")),
        name=name,
    )(x, mods, *flat, *ws)


def _ret_kernel(has_s0, nch, scale, lg_ref, q_ref, k_ref, v_ref, g_ref, *refs):
    if has_s0:
        s0_ref, o_ref, sfin_ref, oacc_ref, st_ref = refs
    else:
        o_ref, sfin_ref, oacc_ref, st_ref = refs
    c = RET_CHUNK
    h = pl.program_id(1)
    lgf = lg_ref[0, h]
    lgb = lg_ref[1, h]
    dv = v_ref.shape[1]

    ii = lax.broadcasted_iota(jnp.int32, (c, c), 0).astype(F32)
    jj = lax.broadcasted_iota(jnp.int32, (c, c), 1).astype(F32)
    diff = ii - jj
    dtot = (jnp.where(diff >= 0, jnp.exp(lgf * jnp.maximum(diff, 0.0)), 0.0)
            + jnp.where(diff <= 0, jnp.exp(lgb * jnp.maximum(-diff, 0.0)), 0.0)) * scale
    pos = lax.broadcasted_iota(jnp.int32, (c, 1), 0).astype(F32)
    qdec_f = jnp.exp(lgf * (pos + 1.0)) * scale
    kdec_f = jnp.exp(lgf * (c - 1.0 - pos))
    qdec_b = jnp.exp(lgb * (c - pos)) * scale
    kdec_b = jnp.exp(lgb * pos)
    cdec_f = jnp.exp(jnp.full((1, dv), lgf * c, F32))
    cdec_b = jnp.exp(jnp.full((1, dv), lgb * c, F32))

    if has_s0:
        st_ref[...] = s0_ref[0, 0, 0, 0].astype(F32)
    else:
        st_ref[...] = jnp.zeros_like(st_ref)

    def fwd(n, carry):
        r = pl.multiple_of(n * c, c)
        qn = q_ref[pl.ds(r, c), :]
        kn = k_ref[pl.ds(r, c), :]
        vn = v_ref[pl.ds(r, c), :]
        p = (_dot_nt(qn, kn) * dtot).astype(BF16)
        o = _dot(p, vn)
        o += _dot((qn.astype(F32) * qdec_f).astype(BF16), st_ref[...].astype(BF16))
        oacc_ref[pl.ds(r, c), :] = o
        kd = (kn.astype(F32) * kdec_f).astype(BF16)
        st_ref[...] = st_ref[...] * cdec_f + _dot_tn(kd, vn)
        return carry

    lax.fori_loop(0, nch, fwd, 0, unroll=min(nch, RET_UNROLL))
    sfin_ref[0, 0, 0] = st_ref[...]

    if has_s0:
        st_ref[...] = s0_ref[0, 0, 1, 0].astype(F32)
    else:
        st_ref[...] = jnp.zeros_like(st_ref)

    def bwd(m, carry):
        r = pl.multiple_of((nch - 1 - m) * c, c)
        qn = q_ref[pl.ds(r, c), :]
        kn = k_ref[pl.ds(r, c), :]
        vn = v_ref[pl.ds(r, c), :]
        oacc_ref[pl.ds(r, c), :] += _dot((qn.astype(F32) * qdec_b).astype(BF16), st_ref[...].astype(BF16))
        kd = (kn.astype(F32) * kdec_b).astype(BF16)
        st_ref[...] = st_ref[...] * cdec_b + _dot_tn(kd, vn)
        return carry

    lax.fori_loop(0, nch, bwd, 0, unroll=min(nch, RET_UNROLL))
    sfin_ref[0, 1, 0] = st_ref[...]

    o = oacc_ref[...]
    o = o * lax.rsqrt(jnp.mean(o * o, axis=-1, keepdims=True) + EPS)
    o_ref[...] = (_silu(g_ref[...].astype(F32)) * o).astype(o_ref.dtype)


def _retention(z, dk, log_gamma, s0, nb, seq, row_blk0, name):
    hh = RET_HEADS
    dv = dk
    nch = seq // RET_CHUNK
    has_s0 = s0 is not None
    scale = float(dk) ** -0.5

    def col(off):
        return lambda b, h: (row_blk0 + b, off + h)

    in_specs = [pl.BlockSpec(memory_space=pltpu.SMEM),
                pl.BlockSpec((seq, dk), col(0)),
                pl.BlockSpec((seq, dk), col(hh)),
                pl.BlockSpec((seq, dv), col(2 * hh)),
                pl.BlockSpec((seq, dv), col(3 * hh))]
    args = [log_gamma, z, z, z, z]
    if has_s0:
        s0_all, s0_layer = s0
        in_specs.append(pl.BlockSpec((1, 1, 2, 1, dk, dv), lambda b, h: (b, s0_layer, 0, h, 0, 0)))
        args.append(s0_all)
    return pl.pallas_call(
        functools.partial(_ret_kernel, has_s0, nch, scale),
        grid=(nb, hh),
        in_specs=in_specs,
        out_specs=[pl.BlockSpec((seq, dv), lambda b, h: (b, h)),
                   pl.BlockSpec((1, 2, 1, dk, dv), lambda b, h: (b, 0, h, 0, 0))],
        out_shape=[jax.ShapeDtypeStruct((nb * seq, hh * dv), BF16),
                   jax.ShapeDtypeStruct((nb, 2, hh, dk, dv), F32)],
        scratch_shapes=[pltpu.VMEM((seq, dv), F32), pltpu.VMEM((dk, dv), F32)],
        compiler_params=_cp(("parallel", "parallel")),
        name=name,
    )(*args)


def _s5_kernel(rc, ncc, bc, ncs, bs, u_ref, strip_ref, wst_ref, v_ref, lam_ref, d_ref, x0_ref,
               y_ref, fin_ref, mt_ref, sre, sim, are, aim, bre, bim):
    u = u_ref[0]
    q, cg = S5_Q, S5_GROUP
    strip = strip_ref[0]
    for j in range(q):
        off = (q - 1 - j) * cg
        mt_ref[j * cg:(j + 1) * cg, :] = strip[:, off:off + q * cg].astype(BF16)
    z = _dot(u, wst_ref[0])
    y0 = _dot(u, mt_ref[...]) + d_ref[0] * u.astype(F32)
    sre[...] = z[:, :S5_LANES]
    sim[...] = z[:, S5_LANES:2 * S5_LANES]
    lr = lam_ref[0, 0:1, :]
    li = lam_ref[0, 1:2, :]

    def scan(row0, nc, b, xr0, xi0):
        is_f = lax.broadcasted_iota(jnp.int32, (b, S5_LANES), 1) < S5_LANES // 2

        def body(s, carry):
            xr, xi = carry
            rf = pl.multiple_of(row0 + s * b, b)
            rb = pl.multiple_of(row0 + (nc - 1 - s) * b, b)
            are[pl.ds(rf, b), :] = xr
            aim[pl.ds(rf, b), :] = xi
            bre[pl.ds(rb, b), :] = xr
            bim[pl.ds(rb, b), :] = xi
            sr = jnp.where(is_f, sre[pl.ds(rf, b), :], sre[pl.ds(rb, b), :])
            si = jnp.where(is_f, sim[pl.ds(rf, b), :], sim[pl.ds(rb, b), :])
            return xr * lr - xi * li + sr, xi * lr + xr * li + si

        return lax.fori_loop(0, nc, body, (xr0, xi0))

    zero = jnp.zeros((bc, S5_LANES), F32)
    fr, fi = scan(0, ncc, bc, zero, zero)
    fin_ref[0, 0] = fr
    fin_ref[0, 1] = fi
    scan(rc, ncs, bs, x0_ref[0, 0], x0_ref[0, 1])
    y_ref[0] = (y0 + _dot(are[...].astype(BF16), v_ref[0, 0]) + _dot(aim[...].astype(BF16), v_ref[0, 1])
                + _dot(bre[...].astype(BF16), v_ref[0, 2]) + _dot(bim[...].astype(BF16), v_ref[0, 3])
                ).astype(y_ref.dtype)


def _cmul(ar, ai, br, bi):
    return ar * br - ai * bi, ar * bi + ai * br


def _cexp(re, im):
    e = jnp.exp(re)
    return e * jnp.cos(im), e * jnp.sin(im)


def _s5_operators(a_re, a_im, log_dt, b_re, b_im, c_re, c_im):
    q = S5_Q
    hp = lax.Precision.HIGHEST
    a_re, a_im = a_re.astype(F32), a_im.astype(F32)
    dt = jnp.exp(log_dt.astype(F32))[..., None]
    ldr, ldi = a_re * dt, a_im * dt
    lbr, lbi = _cexp(ldr, ldi)
    den = a_re * a_re + a_im * a_im
    fr = ((lbr - 1.0) * a_re + lbi * a_im) / den
    fi = (lbi * a_re - (lbr - 1.0) * a_im) / den
    bbr, bbi = _cmul(fr[..., None], fi[..., None], b_re.astype(F32), b_im.astype(F32))
    c_re, c_im = c_re.astype(F32), c_im.astype(F32)
    g, p = a_re.shape[1], a_re.shape[2]
    cg = bbr.shape[-1]
    steps = jnp.arange(q + 1, dtype=F32)
    pwr, pwi = _cexp(ldr[..., None] * steps, ldi[..., None] * steps)
    c_pr, c_pi = jnp.swapaxes(c_re, 2, 3), jnp.swapaxes(c_im, 2, 3)
    cpr, cpi = _cmul(c_pr[:, :, :, None, :], c_pi[:, :, :, None, :],
                     pwr[..., :q, None], pwi[..., :q, None])
    cpr, cpi = cpr.reshape(2, g, p, q * cg), cpi.reshape(2, g, p, q * cg)
    kern = (jnp.einsum('dgpe,dgpx->dgex', bbr, cpr, precision=hp)
            - jnp.einsum('dgpe,dgpx->dgex', bbi, cpi, precision=hp)).reshape(2, g, cg, q, cg)
    strip = jnp.concatenate([kern[1][:, :, :0:-1], kern[0][:, :, :1] + kern[1][:, :, :1], kern[0][:, :, 1:],
                             jnp.zeros((g, cg, 1, cg), F32)], axis=2).reshape(g, cg, 2 * q * cg)
    ti = jnp.arange(q)

    def state_in(d, t_idx):
        wr, wi = _cmul(pwr[d][..., t_idx][..., None], pwi[d][..., t_idx][..., None],
                       bbr[d][:, :, None, :], bbi[d][:, :, None, :])
        tr = lambda w: jnp.transpose(w, (0, 2, 3, 1)).reshape(g, q * cg, p)
        return tr(wr), tr(wi)

    wfr, wfi = state_in(0, q - 1 - ti)
    wbr, wbi = state_in(1, ti)
    wst = jnp.concatenate([wfr, wbr, wfi, wbi], axis=-1)

    def state_out(d, t_idx):
        vr, vi = _cmul(c_re[d][..., None], c_im[d][..., None],
                       pwr[d][:, None, :, :][..., t_idx], pwi[d][:, None, :, :][..., t_idx])
        tr = lambda w: jnp.transpose(w, (0, 2, 3, 1)).reshape(g, p, q * cg)
        return tr(vr), tr(vi)

    vfr, vfi = state_out(0, ti + 1)
    vbr, vbi = state_out(1, q - ti)
    zf = jnp.zeros_like(vfr)
    v = jnp.stack([jnp.concatenate([vfr, zf], axis=1), jnp.concatenate([-vfi, zf], axis=1),
                   jnp.concatenate([zf, vbr], axis=1), jnp.concatenate([zf, -vbi], axis=1)], axis=1)
    lam_pack = jnp.stack([jnp.concatenate([pwr[0][..., q], pwr[1][..., q]], axis=-1),
                          jnp.concatenate([pwi[0][..., q], pwi[1][..., q]], axis=-1)], axis=1)
    return strip, wst.astype(BF16), v.astype(BF16), lam_pack


def _s5_rows(dm):
    rc = dm.lc // S5_Q * dm.bc
    return rc, rc + dm.ls // S5_Q * dm.bs


def _chunk_row_loops(dm, body):
    rc, _ = _s5_rows(dm)
    for tok0, row0, nc, b, seq in ((0, 0, dm.lc // S5_Q, dm.bc, dm.lc), (dm.tc, rc, dm.ls // S5_Q, dm.bs, dm.ls)):
        def step(n, c, tok0=tok0, row0=row0, b=b, seq=seq):
            body(tok0 + n * S5_Q, pl.multiple_of(row0 + n * b, b), b, seq)
            return c
        lax.fori_loop(0, nc, step, 0)


def _s5_pack_kernel(dm, u_ref, o_ref, a_ref, ab_ref):
    q, cg = S5_Q, S5_GROUP

    def gather(tok, row, b, seq):
        for i in range(q):
            a_ref[i, pl.ds(row, b), :] = u_ref[pl.ds(tok + i, b, stride=seq), :]

    _chunk_row_loops(dm, gather)
    ab_ref[...] = a_ref[...].astype(BF16)
    src = lax.broadcasted_iota(jnp.int32, (2 * LANES, q * cg), 0)
    dst = lax.broadcasted_iota(jnp.int32, (2 * LANES, q * cg), 1)
    for gl in range(LANES // cg):
        acc = None
        for i in range(0, q, 2):
            sel = jnp.where((src % LANES == gl * cg + dst % cg) & (dst // cg == i + src // LANES),
                            1.0, 0.0).astype(BF16)
            part = _dot(jnp.concatenate([ab_ref[i], ab_ref[i + 1]], axis=1), sel)
            acc = part if acc is None else acc + part
        o_ref[gl] = acc.astype(o_ref.dtype)


def _s5_unpack_kernel(dm, y_ref, o_ref, yt_ref):
    q, cg = S5_Q, S5_GROUP
    src = lax.broadcasted_iota(jnp.int32, (q * cg, 2 * LANES), 0)
    dst = lax.broadcasted_iota(jnp.int32, (q * cg, 2 * LANES), 1)
    for i in range(0, q, 2):
        acc = None
        for gl in range(LANES // cg):
            sel = jnp.where((src // cg == i + dst // LANES) & (src % cg == dst % cg)
                            & (dst % LANES // cg == gl), 1.0, 0.0).astype(BF16)
            part = _dot(y_ref[gl], sel)
            acc = part if acc is None else acc + part
        yt_ref[...] = acc

        def spread(tok, row, b, seq, i=i):
            o_ref[pl.ds(tok + i, b, stride=seq), :] = yt_ref[pl.ds(row, b), :LANES]
            o_ref[pl.ds(tok + i + 1, b, stride=seq), :] = yt_ref[pl.ds(row, b), LANES:]

        _chunk_row_loops(dm, spread)


def _s5_pack(dm, u_tok):
    t, w = u_tok.shape
    _, r = _s5_rows(dm)
    gps = LANES // S5_GROUP
    nq = S5_Q * S5_GROUP
    return pl.pallas_call(
        functools.partial(_s5_pack_kernel, dm),
        grid=(w // LANES,),
        in_specs=[pl.BlockSpec((t, LANES), lambda s: (0, s))],
        out_specs=pl.BlockSpec((gps, r, nq), lambda s: (s, 0, 0)),
        out_shape=jax.ShapeDtypeStruct((w // S5_GROUP, r, nq), BF16),
        scratch_shapes=[pltpu.VMEM((S5_Q, r, LANES), F32), pltpu.VMEM((S5_Q, r, LANES), BF16)],
        compiler_params=_cp(("parallel",)),
        name="s5_pack",
    )(u_tok)


def _s5_unpack(dm, y_g):
    g, r, nq = y_g.shape
    gps = LANES // S5_GROUP
    return pl.pallas_call(
        functools.partial(_s5_unpack_kernel, dm),
        grid=(g // gps,),
        in_specs=[pl.BlockSpec((gps, r, nq), lambda s: (s, 0, 0))],
        out_specs=pl.BlockSpec((dm.t, LANES), lambda s: (0, s)),
        out_shape=jax.ShapeDtypeStruct((dm.t, g * S5_GROUP), F32),
        scratch_shapes=[pltpu.VMEM((r, 2 * LANES), F32)],
        compiler_params=_cp(("parallel",)),
        name="s5_unpack",
    )(y_g)


def _s5(dm, u_tok, ops, d_skip, x0_re, x0_im):
    strip, wst, v, lam_pack = ops
    g = wst.shape[0]
    p = lam_pack.shape[2] // 2
    assert 2 * p == S5_LANES
    q, cg = S5_Q, S5_GROUP
    ncc, ncs = dm.lc // q, dm.ls // q
    rc, r = _s5_rows(dm)
    u_g = _s5_pack(dm, u_tok)
    d_g = jnp.tile(d_skip.astype(F32).reshape(g, 1, cg), (1, q, 1)).reshape(g, 1, q * cg)
    x0 = jnp.stack([jnp.concatenate([x0_re[:, 0], x0_re[:, 1]], axis=-1),
                    jnp.concatenate([x0_im[:, 0], x0_im[:, 1]], axis=-1)], axis=0)
    x0 = x0.transpose(2, 0, 1, 3).astype(F32)
    y_g, fin = pl.pallas_call(
        functools.partial(_s5_kernel, rc, ncc, dm.bc, ncs, dm.bs),
        grid=(g,),
        in_specs=[pl.BlockSpec((1, r, q * cg), lambda i: (i, 0, 0)),
                  pl.BlockSpec((1, cg, 2 * q * cg), lambda i: (i, 0, 0)),
                  pl.BlockSpec((1, q * cg, 2 * S5_LANES), lambda i: (i, 0, 0)),
                  pl.BlockSpec((1, 4, S5_LANES, q * cg), lambda i: (i, 0, 0, 0)),
                  pl.BlockSpec((1, 2, S5_LANES), lambda i: (i, 0, 0)),
                  pl.BlockSpec((1, 1, q * cg), lambda i: (i, 0, 0)),
                  pl.BlockSpec((1, 2, dm.bs, S5_LANES), lambda i: (i, 0, 0, 0))],
        out_specs=[pl.BlockSpec((1, r, q * cg), lambda i: (i, 0, 0)),
                   pl.BlockSpec((1, 2, dm.bc, S5_LANES), lambda i: (i, 0, 0, 0))],
        out_shape=[jax.ShapeDtypeStruct((g, r, q * cg), BF16),
                   jax.ShapeDtypeStruct((g, 2, dm.bc, S5_LANES), F32)],
        scratch_shapes=[pltpu.VMEM((q * cg, q * cg), BF16)] + [pltpu.VMEM((r, S5_LANES), F32)] * 6,
        compiler_params=_cp(("parallel",)),
        name="s5",
    )(u_g, strip, wst, v, lam_pack, d_g, x0)

    y_tok = _s5_unpack(dm, y_g)
    fin = fin.transpose(2, 1, 0, 3)
    fin_re = jnp.stack([fin[:, 0, :, :p], fin[:, 0, :, p:]], axis=1)
    fin_im = jnp.stack([fin[:, 1, :, :p], fin[:, 1, :, p:]], axis=1)
    return y_tok, fin_re, fin_im


def _glu_kernel(y_ref, w_ref, o_ref):
    z = _gelu_tanh(y_ref[...].astype(F32))
    o_ref[...] = (z * _sigmoid(_dot(z.astype(BF16), w_ref[...]))).astype(o_ref.dtype)


def _glu(dm, y, w):
    t, n = y.shape
    tm = dm.tm
    return pl.pallas_call(
        _glu_kernel,
        grid=(t // tm,),
        in_specs=[pl.BlockSpec((tm, n), lambda i: (i, 0)),
                  pl.BlockSpec((n, n), lambda i: (0, 0))],
        out_specs=pl.BlockSpec((tm, n), lambda i: (i, 0)),
        out_shape=jax.ShapeDtypeStruct((t, n), BF16),
        compiler_params=_cp(("parallel",)),
        name="s5_glu",
    )(y, w)


def _rot(x, ct, st):
    return x * ct + pltpu.roll(x, 64, 1) * st


def _mla_q_kernel(qscale, cq_ref, ckv_ref, kr_ref, qn_ref, kvn_ref, ct_ref, st_ref, w_ref,
                  q_ref, ckvn_ref, kro_ref):
    ct = ct_ref[...]
    st = st_ref[...]
    cq = cq_ref[...]
    cqn = (cq * lax.rsqrt(jnp.mean(cq * cq, axis=-1, keepdims=True) + EPS) * qn_ref[...]).astype(BF16)
    for h in range(MLA_HEADS):
        qh = _dot(cqn, w_ref[:, h * QK_PAD:(h + 1) * QK_PAD])
        q_ref[:, h * QK_PAD:h * QK_PAD + QK_NOPE] = (qh[:, :QK_NOPE] * qscale).astype(BF16)
        q_ref[:, h * QK_PAD + QK_NOPE:(h + 1) * QK_PAD] = (_rot(qh[:, QK_NOPE:], ct, st) * qscale).astype(BF16)
    ckv = ckv_ref[...]
    ckvn_ref[...] = ckv * lax.rsqrt(jnp.mean(ckv * ckv, axis=-1, keepdims=True) + EPS) * kvn_ref[...]
    kro_ref[...] = _rot(kr_ref[...], ct, st)


def _mla_q(dm, a, q_norm, kv_norm, ct, st, w_uq_ext, q_lora, kv_lora):
    t = a.shape[0]
    nq = w_uq_ext.shape[1]
    kr_blk = (q_lora + kv_lora) // 128
    tm = min(dm.tm, 512)
    qscale = float(QK_NOPE + QK_ROPE) ** -0.5 * math.log2(math.e)
    return pl.pallas_call(
        functools.partial(_mla_q_kernel, qscale),
        grid=(t // tm,),
        in_specs=[pl.BlockSpec((tm, q_lora), lambda i: (i, 0)),
                  pl.BlockSpec((tm, kv_lora), lambda i: (i, q_lora // kv_lora)),
                  pl.BlockSpec((tm, 128), lambda i: (i, kr_blk)),
                  pl.BlockSpec((1, q_lora), lambda i: (0, 0)),
                  pl.BlockSpec((1, kv_lora), lambda i: (0, 0)),
                  pl.BlockSpec((tm, 128), lambda i: (i, 0)),
                  pl.BlockSpec((tm, 128), lambda i: (i, 0)),
                  pl.BlockSpec((q_lora, nq), lambda i: (0, 0))],
        out_specs=[pl.BlockSpec((tm, nq), lambda i: (i, 0)),
                   pl.BlockSpec((tm, kv_lora), lambda i: (i, 0)),
                   pl.BlockSpec((tm, 128), lambda i: (i, 0))],
        out_shape=[jax.ShapeDtypeStruct((t, nq), BF16),
                   jax.ShapeDtypeStruct((t, kv_lora), F32),
                   jax.ShapeDtypeStruct((t, 128), F32)],
        compiler_params=_cp(("parallel",)),
        name="mla_q",
    )(a, a, a, q_norm, kv_norm, ct, st, w_uq_ext)


def _kv_expand_kernel(n_lat_tiles, per, cc_ref, ckr_ref, c_ref, kr_ref, wk_ref, wv_ref, k_ref, v_ref):
    t = pl.program_id(0)
    is_cache = jnp.logical_and(t < n_lat_tiles, t % per == 0)
    ckr = ckr_ref[0, 0].astype(F32)
    ckr = jnp.concatenate([ckr, jnp.zeros((ckr.shape[0], LANES - ckr.shape[1]), F32)], axis=1)
    c = jnp.where(is_cache, cc_ref[0, 0].astype(F32), c_ref[...]).astype(BF16)
    kr = jnp.where(is_cache, ckr, kr_ref[...]).astype(BF16)
    kn = _dot(c, wk_ref[...]).astype(BF16)
    vv = _dot(c, wv_ref[...]).astype(BF16)
    ones = jnp.ones((c.shape[0], V_PAD - V_HEAD), BF16)
    for h in range(MLA_HEADS):
        k_ref[:, h * QK_PAD:h * QK_PAD + QK_NOPE] = kn[:, h * QK_NOPE:(h + 1) * QK_NOPE]
        k_ref[:, h * QK_PAD + QK_NOPE:(h + 1) * QK_PAD] = kr
        v_ref[:, h * V_PAD:h * V_PAD + V_HEAD] = vv[:, h * V_HEAD:(h + 1) * V_HEAD]
        v_ref[:, h * V_PAD + V_HEAD:(h + 1) * V_PAD] = ones


def _kv_expand(dm, cache_ckv, cache_krope, j, ckv_n, kr, w_uk, w_uv):
    past, kv_lora = cache_ckv.shape[2:]
    tm = past
    assert dm.ls % tm == 0 and dm.tc % tm == 0
    per = 1 + dm.ls // tm
    n_lat = dm.bs * per
    nk = (n_lat + dm.tc // tm) * tm

    def own_blk(t):
        lat = dm.tc // tm + (t // per) * (dm.ls // tm) + jnp.maximum(t % per - 1, 0)
        return jnp.where(t < n_lat, lat, t - n_lat)

    def cache_idx(t):
        return (jnp.minimum(t // per, dm.bs - 1), j, 0, 0)

    return pl.pallas_call(
        functools.partial(_kv_expand_kernel, n_lat, per),
        grid=(nk // tm,),
        in_specs=[pl.BlockSpec((1, 1, tm, kv_lora), cache_idx),
                  pl.BlockSpec((1, 1, tm, cache_krope.shape[3]), cache_idx),
                  pl.BlockSpec((tm, kv_lora), lambda t: (own_blk(t), 0)),
                  pl.BlockSpec((tm, LANES), lambda t: (own_blk(t), 0)),
                  pl.BlockSpec(w_uk.shape, lambda t: (0, 0)),
                  pl.BlockSpec(w_uv.shape, lambda t: (0, 0))],
        out_specs=[pl.BlockSpec((tm, MLA_HEADS * QK_PAD), lambda t: (t, 0)),
                   pl.BlockSpec((tm, MLA_HEADS * V_PAD), lambda t: (t, 0))],
        out_shape=[jax.ShapeDtypeStruct((nk, MLA_HEADS * QK_PAD), BF16),
                   jax.ShapeDtypeStruct((nk, MLA_HEADS * V_PAD), BF16)],
        compiler_params=_cp(("parallel",)),
        name="mla_kv_expand",
    )(cache_ckv, cache_krope, ckv_n, kr, w_uk, w_uv)


def _attn_kernel(nh, nkb, q_ref, k_ref, v_ref, o_ref):
    kb = k_ref.shape[0] // nkb
    for h in range(nh):
        q = q_ref[:, h * QK_PAD:(h + 1) * QK_PAD]
        m, acc = None, None
        for j in range(nkb):
            s = _dot_nt(q, k_ref[j * kb:(j + 1) * kb, h * QK_PAD:(h + 1) * QK_PAD])
            mj = jnp.max(s, axis=-1, keepdims=True)
            m_new = mj if j == 0 else jnp.maximum(m, mj)
            pv = _dot(jnp.exp2(s - m_new).astype(BF16), v_ref[j * kb:(j + 1) * kb, h * V_PAD:(h + 1) * V_PAD])
            acc = pv if j == 0 else acc * jnp.exp2(m - m_new) + pv
            m = m_new
        o_ref[:, h * V_HEAD:(h + 1) * V_HEAD] = (acc[:, :V_HEAD] / acc[:, V_HEAD:V_HEAD + 1]).astype(o_ref.dtype)


def _attention(q, k, v, nb, lq, lk, tq, nh, q_row0, k_row0, name):
    nqt = lq // tq
    qb0, kb0 = q_row0 // tq, k_row0 // lk
    nkb = -(-lk // KB_MAX)
    assert lk % nkb == 0
    return pl.pallas_call(
        functools.partial(_attn_kernel, nh, nkb),
        grid=(nb, MLA_HEADS // nh, nqt),
        in_specs=[pl.BlockSpec((tq, nh * QK_PAD), lambda b, h, i: (qb0 + b * nqt + i, h)),
                  pl.BlockSpec((lk, nh * QK_PAD), lambda b, h, i: (kb0 + b, h)),
                  pl.BlockSpec((lk, nh * V_PAD), lambda b, h, i: (kb0 + b, h))],
        out_specs=pl.BlockSpec((tq, nh * V_HEAD), lambda b, h, i: (b * nqt + i, h)),
        out_shape=jax.ShapeDtypeStruct((nb * lq, MLA_HEADS * V_HEAD), BF16),
        compiler_params=_cp(("parallel", "parallel", "arbitrary")),
        name=name,
    )(q, k, v)


def _route_kernel(n_exp, x_ref, g_ref, sh_ref, sc_ref, rw_ref, rb_ref, tri_ref,
                  hp_ref, idx_ref, rank_ref, wt_ref, cnt_ref, run_ref):
    @pl.when(pl.program_id(0) == 0)
    def _():
        run_ref[...] = jnp.zeros_like(run_ref)

    h = _norm_mod(x_ref[...], g_ref[0], sc_ref[0], sh_ref[0])
    _store_row_tiles(hp_ref, _pack_halves(h))
    scores =_sigmoid(_dot_nt(rw_ref[...], h.astype(BF16)))
    biased = scores + rb_ref[...]
    per = n_exp // N_EXPERT_GROUPS
    assert per == 4
    rows_b = [biased[e:e + 1, :] for e in range(n_exp)]
    rows_s = [scores[e:e + 1, :] for e in range(n_exp)]
    best_sum, best_g = None, None
    for gi in range(N_EXPERT_GROUPS):
        a, b, c, d = rows_b[per * gi:per * gi + per]
        hi1, lo1 = jnp.maximum(a, b), jnp.minimum(a, b)
        hi2, lo2 = jnp.maximum(c, d), jnp.minimum(c, d)
        top2 = jnp.maximum(hi1, hi2) + jnp.maximum(jnp.minimum(hi1, hi2), jnp.maximum(lo1, lo2))
        if gi == 0:
            best_sum, best_g = top2, jnp.zeros_like(top2, dtype=jnp.int32)
        else:
            upd = top2 > best_sum
            best_sum = jnp.where(upd, top2, best_sum)
            best_g = jnp.where(upd, gi, best_g)
    vb, vs = [], []
    for k in range(per):
        accb, accs = rows_b[k], rows_s[k]
        for gi in range(1, N_EXPERT_GROUPS):
            sel = best_g == gi
            accb = jnp.where(sel, rows_b[per * gi + k], accb)
            accs = jnp.where(sel, rows_s[per * gi + k], accs)
        vb.append(accb)
        vs.append(accs)

    def first_argmax(vals, exclude):
        bv, bi, bs = None, None, None
        for k in range(per):
            v = vals[k] if exclude is None else jnp.where(exclude == k, -jnp.inf, vals[k])
            if k == 0:
                bv, bi, bs = v, jnp.zeros_like(best_g), vs[0]
            else:
                upd = v > bv
                bv = jnp.where(upd, v, bv)
                bi = jnp.where(upd, k, bi)
                bs = jnp.where(upd, vs[k], bs)
        return bi, bs

    i1, s1 = first_argmax(vb, None)
    i2, s2 = first_argmax(vb, i1)
    tot = s1 + s2
    e1 = best_g * per + i1
    e2 = best_g * per + i2
    idx_ref[0:1, :] = e1
    idx_ref[1:2, :] = e2
    wt_ref[0:1, :] = s1 / tot
    wt_ref[1:2, :] = s2 / tot
    eids = lax.broadcasted_iota(jnp.int32, (n_exp, e1.shape[1]), 0)
    hit1 = eids == e1
    hit2 = eids == e2
    oh1 = jnp.where(hit1, 1.0, 0.0)
    oh2 = jnp.where(hit2, 1.0, 0.0)
    p1 = _dot(oh1.astype(BF16), tri_ref[...])
    p2 = _dot(oh2.astype(BF16), tri_ref[...])
    c1 = jnp.sum(oh1, axis=1, keepdims=True)
    c2 = jnp.sum(oh2, axis=1, keepdims=True)
    run = run_ref[...]
    rank_ref[0:1, :] = jnp.sum(jnp.where(hit1, run + p1, 0.0), axis=0, keepdims=True).astype(jnp.int32)
    rank_ref[1:2, :] = jnp.sum(jnp.where(hit2, run + c1 + p2, 0.0), axis=0, keepdims=True).astype(jnp.int32)
    run_ref[...] = run + c1 + c2
    cnt_ref[...] = run + c1 + c2


def _route(dm, x, gain, mods, layer, rw_t, rb):
    t, d = x.shape
    n_exp = rw_t.shape[0]
    tm = dm.tm
    rt = d // 2 // LANES
    tri = (jnp.arange(tm)[:, None] < jnp.arange(tm)[None, :]).astype(BF16)
    return pl.pallas_call(
        functools.partial(_route_kernel, n_exp),
        grid=(t // tm,),
        in_specs=[pl.BlockSpec((tm, d), lambda i: (i, 0)),
                  pl.BlockSpec((1, 1, d), lambda i: (layer, 0, 0)),
                  pl.BlockSpec((1, 1, d), _mod_idx(dm, layer, 3, tm)),
                  pl.BlockSpec((1, 1, d), _mod_idx(dm, layer, 4, tm)),
                  pl.BlockSpec((n_exp, d), lambda i: (0, 0)),
                  pl.BlockSpec((n_exp, 1), lambda i: (0, 0)),
                  pl.BlockSpec((tm, tm), lambda i: (0, 0))],
        out_specs=[pl.BlockSpec((tm * rt, LANES), lambda i: (i, 0)),
                   pl.BlockSpec((2, tm), lambda i: (0, i)),
                   pl.BlockSpec((2, tm), lambda i: (0, i)),
                   pl.BlockSpec((2, tm), lambda i: (0, i)),
                   pl.BlockSpec((n_exp, 1), lambda i: (0, 0))],
        out_shape=[jax.ShapeDtypeStruct((t * rt, LANES), U32),
                   jax.ShapeDtypeStruct((2, t), jnp.int32),
                   jax.ShapeDtypeStruct((2, t), jnp.int32),
                   jax.ShapeDtypeStruct((2, t), F32),
                   jax.ShapeDtypeStruct((n_exp, 1), F32)],
        scratch_shapes=[pltpu.VMEM((n_exp, 1), F32)],
        compiler_params=_cp(("arbitrary",)),
        name="moe_route",
    )(x, gain, mods, mods, rw_t, rb, tri)


def _slot_plan(idx, rank, counts, n_exp, n_slots):
    counts = counts.reshape(n_exp).astype(jnp.int32)
    padded = ((counts + TM_MOE - 1) // TM_MOE) * TM_MOE
    ends = jnp.cumsum(padded)
    starts = ends - padded
    start_of = jnp.sum(jnp.where(idx[..., None] == jnp.arange(n_exp, dtype=jnp.int32), starts, 0), axis=-1)
    dest = (start_of + rank).reshape(-1).astype(jnp.int32)
    tile_start = jnp.arange(n_slots // TM_MOE, dtype=jnp.int32) * TM_MOE
    tile_exp = jnp.minimum(jnp.sum((ends[None, :] <= tile_start[:, None]).astype(jnp.int32), axis=1), n_exp - 1)
    tile_ok = (tile_start < ends[-1]).astype(jnp.int32)
    pad_lo = jnp.concatenate([starts + counts, ends[-1:]]).astype(jnp.int32)
    pad_n = (padded - counts).astype(jnp.int32)
    return dest, tile_exp.astype(jnp.int32), tile_ok, pad_lo, pad_n


def _scatter_kernel(n_exp, t_total, rows, rt, dest_ref, padlo_ref, padn_ref, hp_ref, xs_ref, zrow, sem):
    i = pl.program_id(0)
    base = i * rows
    tile_rows = TM_MOE * rt

    def slot(s):
        return xs_ref.at[pl.ds(pl.multiple_of(s * rt, rt), rt)]

    @pl.when(i == 0)
    def _():
        zrow[...] = jnp.zeros_like(zrow)
        for e in range(n_exp):
            lo = padlo_ref[e]

            def pad_copy(r, lo=lo):
                return pltpu.make_async_copy(zrow.at[pl.ds(0, rt)], slot(lo + r), sem)

            def start(r, c, pad_copy=pad_copy):
                pad_copy(r).start()
                return c

            def wait(r, c, pad_copy=pad_copy):
                pad_copy(r).wait()
                return c

            lax.fori_loop(0, padn_ref[e], start, 0)
            lax.fori_loop(0, padn_ref[e], wait, 0)

        tail0 = padlo_ref[n_exp]

        def tail_copy(k):
            row = pl.multiple_of((tail0 + k * TM_MOE) * rt, tile_rows)
            return pltpu.make_async_copy(zrow, xs_ref.at[pl.ds(row, tile_rows)], sem)

        def tail_start(k, c):
            tail_copy(k).start()
            return c

        def tail_wait(k, c):
            tail_copy(k).wait()
            return c

        n_tail = (xs_ref.shape[0] // rt - tail0) // TM_MOE
        lax.fori_loop(0, n_tail, tail_start, 0)
        lax.fori_loop(0, n_tail, tail_wait, 0)

    def copies(r):
        src = hp_ref.at[pl.ds(pl.multiple_of(r * rt, rt), rt)]
        return (pltpu.make_async_copy(src, slot(dest_ref[base + r]), sem),
                pltpu.make_async_copy(src, slot(dest_ref[t_total + base + r]), sem))

    def start(gi, c):
        for u in range(DMA_UNROLL):
            for prio, cp in enumerate(copies(gi * DMA_UNROLL + u)):
                cp.start(priority=prio)
        return c

    def wait(gi, c):
        for u in range(DMA_UNROLL):
            for cp in copies(gi * DMA_UNROLL + u):
                cp.wait()
        return c

    lax.fori_loop(0, rows // DMA_UNROLL, start, 0)
    lax.fori_loop(0, rows // DMA_UNROLL, wait, 0)


def _scatter_rows(hp, dest, pad_lo, pad_n, n_slots, rows, rt):
    t = hp.shape[0] // rt
    n_exp = pad_n.shape[0]
    return pl.pallas_call(
        functools.partial(_scatter_kernel, n_exp, t, rows, rt),
        grid_spec=pltpu.PrefetchScalarGridSpec(
            num_scalar_prefetch=3,
            grid=(t // rows,),
            in_specs=[pl.BlockSpec((rows * rt, LANES), lambda i, *_: (i, 0))],
            out_specs=pl.BlockSpec(memory_space=pl.ANY),
            scratch_shapes=[pltpu.VMEM((TM_MOE * rt, LANES), U32), pltpu.SemaphoreType.DMA(())]),
        out_shape=jax.ShapeDtypeStruct((n_slots * rt, LANES), U32),
        compiler_params=_cp(("arbitrary",)),
        name="moe_scatter",
    )(dest, pad_lo, pad_n, hp)


def _expert_kernel(rt, te_ref, ok_ref, x_ref, wg_ref, wu_ref, wd_ref, o_ref, wg_b, wu_b, wd_b):
    i = pl.program_id(0)
    new_expert = jnp.logical_or(i == 0, te_ref[i] != te_ref[jnp.maximum(i - 1, 0)])

    @pl.when(new_expert)
    def _():
        wg_b[...] = wg_ref[0, 0].astype(BF16)
        wu_b[...] = wu_ref[0, 0].astype(BF16)
        wd_b[...] = wd_ref[0, 0].astype(BF16)

    @pl.when(ok_ref[i] == 1)
    def _():
        x_lo, x_hi = _unpack_halves(_load_row_tiles(x_ref, rt))
        x_lo, x_hi = x_lo.astype(BF16), x_hi.astype(BF16)
        half = x_lo.shape[1]
        h1 = _dot(x_lo, wg_b[:half, :]) + _dot(x_hi, wg_b[half:, :])
        h2 = _dot(x_lo, wu_b[:half, :]) + _dot(x_hi, wu_b[half:, :])
        act = (_silu(h1) * h2).astype(BF16)
        _store_row_tiles(o_ref, _pack_halves(_dot(act, wd_b[...])))

    @pl.when(ok_ref[i] == 0)
    def _():
        o_ref[...] = jnp.zeros_like(o_ref)


def _experts(x_sorted, tile_exp, tile_ok, w_gate, w_up, w_down, layer, rt):
    n_slots = x_sorted.shape[0] // rt
    d, f = w_gate.shape[-2:]
    return pl.pallas_call(
        functools.partial(_expert_kernel, rt),
        grid_spec=pltpu.PrefetchScalarGridSpec(
            num_scalar_prefetch=2,
            grid=(n_slots // TM_MOE,),
            in_specs=[pl.BlockSpec((TM_MOE * rt, LANES), lambda i, te, ok: (i, 0)),
                      pl.BlockSpec((1, 1, d, f), lambda i, te, ok: (layer, te[i], 0, 0)),
                      pl.BlockSpec((1, 1, d, f), lambda i, te, ok: (layer, te[i], 0, 0)),
                      pl.BlockSpec((1, 1, f, d), lambda i, te, ok: (layer, te[i], 0, 0))],
            out_specs=pl.BlockSpec((TM_MOE * rt, LANES), lambda i, te, ok: (i, 0)),
            scratch_shapes=[pltpu.VMEM((d, f), BF16), pltpu.VMEM((d, f), BF16), pltpu.VMEM((f, d), BF16)]),
        out_shape=jax.ShapeDtypeStruct((n_slots * rt, LANES), U32),
        compiler_params=_cp(("arbitrary",)),
        name="moe_experts",
    )(tile_exp, tile_ok, x_sorted, w_gate, w_up, w_down)


def _combine_kernel(t_total, rows, rt, dest_ref, x_ref, gate_ref, wt_ref, y_ref, o_ref, y0, y1, sem):
    base = pl.program_id(0) * rows

    def slot(s):
        return y_ref.at[pl.ds(pl.multiple_of(s * rt, rt), rt)]

    def copies(r):
        dst = pl.ds(pl.multiple_of(r * rt, rt), rt)
        return (pltpu.make_async_copy(slot(dest_ref[base + r]), y0.at[dst], sem),
                pltpu.make_async_copy(slot(dest_ref[t_total + base + r]), y1.at[dst], sem))

    def start(gi, c):
        for u in range(DMA_UNROLL):
            for prio, cp in enumerate(copies(gi * DMA_UNROLL + u)):
                cp.start(priority=prio)
        return c

    def wait(gi, c):
        for u in range(DMA_UNROLL):
            for cp in copies(gi * DMA_UNROLL + u):
                cp.wait()
        return c

    lax.fori_loop(0, rows // DMA_UNROLL, start, 0)
    lax.fori_loop(0, rows // DMA_UNROLL, wait, 0)
    w0 = wt_ref[:, 0:1]
    w1 = wt_ref[:, 1:2]
    a_lo, a_hi = _unpack_halves(_load_row_tiles(y0, rt))
    b_lo, b_hi = _unpack_halves(_load_row_tiles(y1, rt))
    half = a_lo.shape[1]
    gate = gate_ref[0]
    o_ref[:, :half] = x_ref[:, :half] + gate[:, :half] * (w0 * a_lo + w1 * b_lo)
    o_ref[:, half:] = x_ref[:, half:] + gate[:, half:] * (w0 * a_hi + w1 * b_hi)


def _combine(dm, x, mods, layer, y_sorted, dest, wts_t):
    t, d = x.shape
    rows = min(dm.tm, 512)
    rt = d // 2 // LANES
    return pl.pallas_call(
        functools.partial(_combine_kernel, t, rows, rt),
        grid_spec=pltpu.PrefetchScalarGridSpec(
            num_scalar_prefetch=1,
            grid=(t // rows,),
            in_specs=[pl.BlockSpec((rows, d), lambda i, dst: (i, 0)),
                      pl.BlockSpec((1, 1, d), lambda i, dst: _mod_idx(dm, layer, 5, rows)(i)),
                      pl.BlockSpec((rows, 2), lambda i, dst: (i, 0)),
                      pl.BlockSpec(memory_space=pl.ANY)],
            out_specs=pl.BlockSpec((rows, d), lambda i, dst: (i, 0)),
            scratch_shapes=[pltpu.VMEM((rows * rt, LANES), U32), pltpu.VMEM((rows * rt, LANES), U32),
                            pltpu.SemaphoreType.DMA(())]),
        out_shape=jax.ShapeDtypeStruct((t, d), F32),
        compiler_params=_cp(("arbitrary",)),
        name="moe_combine",
    )(dest, x, mods, wts_t, y_sorted)


def _moe(dm, x, gain, mods, layer, rw_t, rb, w_gate, w_up, w_down):
    n_exp = rw_t.shape[0]
    hp, idx, rank, wts, counts = _route(dm, x, gain, mods, layer, rw_t, rb)
    n_slots = 2 * dm.t + n_exp * TM_MOE
    dest, tile_exp, tile_ok, pad_lo, pad_n = _slot_plan(idx, rank, counts, n_exp, n_slots)
    rt = dm.d // 2 // LANES
    x_sorted = _scatter_rows(hp, dest, pad_lo, pad_n, n_slots, min(dm.tm, 512), rt)
    y_sorted = _experts(x_sorted, tile_exp, tile_ok, w_gate, w_up, w_down, layer, rt)
    return _combine(dm, x, mods, layer, y_sorted, dest, wts.T)


def _final_norm_kernel(x_ref, g_ref, o_ref):
    x = x_ref[...]
    o_ref[...] = x * lax.rsqrt(jnp.mean(x * x, axis=-1, keepdims=True) + EPS) * g_ref[...]


def _final_norm(dm, x, gain, row0, rows):
    d = x.shape[1]
    tm = dm.tm
    blk0 = row0 // tm
    return pl.pallas_call(
        _final_norm_kernel,
        grid=(rows // tm,),
        in_specs=[pl.BlockSpec((tm, d), lambda i: (blk0 + i, 0)),
                  pl.BlockSpec((1, d), lambda i: (0, 0))],
        out_specs=pl.BlockSpec((tm, d), lambda i: (i, 0)),
        out_shape=jax.ShapeDtypeStruct((rows, d), F32),
        compiler_params=_cp(("parallel",)),
        name="final_norm",
    )(x, gain.reshape(1, d))


def _rope_tables(dm):
    rows = dm.ls // GRID_W
    row = jnp.repeat(jnp.arange(rows, dtype=F32), GRID_W)
    col = jnp.tile(jnp.arange(GRID_W, dtype=F32), rows)
    half = QK_ROPE // 2
    freqs = jnp.power(ROPE_BASE, -jnp.arange(0, half, 2, dtype=F32) / half)
    ar, ac = row[:, None] * freqs, col[:, None] * freqs
    zeros = jnp.zeros((dm.ls, 128 - QK_ROPE), F32)
    ct = jnp.concatenate([jnp.cos(ar), jnp.cos(ar), jnp.cos(ac), jnp.cos(ac), zeros], axis=-1)
    st = jnp.concatenate([-jnp.sin(ar), jnp.sin(ar), -jnp.sin(ac), jnp.sin(ac), zeros], axis=-1)
    ct_c = jnp.concatenate([jnp.ones((dm.tc, QK_ROPE), F32), jnp.zeros((dm.tc, 128 - QK_ROPE), F32)], axis=-1)
    ct = jnp.concatenate([ct_c, jnp.tile(ct, (dm.bs, 1))], axis=0)
    st = jnp.concatenate([jnp.zeros((dm.tc, 128), F32), jnp.tile(st, (dm.bs, 1))], axis=0)
    return ct, st


def _swap_halves_cols(w):
    qt = QK_ROPE // 4
    return jnp.concatenate([w[..., qt:2 * qt], w[..., :qt], w[..., 3 * qt:], w[..., 2 * qt:3 * qt]], axis=-1)


def kernel(x_prompt, x_sample, c, state_ret, state_s5_re, state_s5_im, cache_ckv, cache_krope, c_ctx, ada_w, ada_b, norm_mix, norm_ffn, norm_final, even_w_in, even_w_out, ret_decay, s5_a_re, s5_a_im, s5_log_dt, s5_b_re, s5_b_im, s5_c_re, s5_c_im, s5_d, s5_w_glu, mla_w_in, mla_q_norm, mla_w_uq, mla_kv_norm, mla_w_ukv, mla_w_out, router_w, router_bias, moe_w_gate, moe_w_up, moe_w_down):
    dm = _Dims(x_prompt, x_sample)
    d = dm.d
    depth = ada_w.shape[0]
    n_exp = router_w.shape[1]
    past = cache_ckv.shape[2]
    q_lora = mla_q_norm.shape[1]
    kv_lora = mla_kv_norm.shape[1]

    x = jnp.concatenate([x_prompt.reshape(dm.tc, d), x_sample.reshape(dm.ts, d)], axis=0)
    cond = jnp.zeros((dm.rp, d), F32).at[0].set(c_ctx).at[1:1 + dm.bs].set(c)
    mods = _adaln(cond, ada_w, ada_b, tn=d * N_MOD // 8).reshape(depth * dm.rp * N_MOD, 1, d)
    g_mix = norm_mix.reshape(depth, 1, d)
    g_ffn = norm_ffn.reshape(depth, 1, d)
    rw_t = router_w.T.astype(BF16)
    rb = router_bias.astype(F32).reshape(n_exp, 1)
    ct, st = _rope_tables(dm)
    s5_ops = jax.vmap(_s5_operators)(s5_a_re, s5_a_im, s5_log_dt, s5_b_re, s5_b_im, s5_c_re, s5_c_im)

    rets, s5r, s5i, ckvs, krs = [], [], [], [], []
    for layer in range(depth):
        if layer % 2 == 0:
            i = layer // 2
            ret_w = even_w_out.shape[1] - s5_d.shape[1]
            s5_w = s5_d.shape[1]
            z, u = _norm_mod_matmul_split(dm, x, g_mix, mods, layer, 0, even_w_in[i].astype(BF16), tn=s5_w,
                                          name="even_in_proj")
            log_gamma = -jnp.exp(ret_decay[i].astype(F32))
            ro_c, sfin = _retention(z, ret_w // RET_HEADS, log_gamma, None, dm.bc, dm.lc, 0, "retention_ctx")
            ro_s, _ = _retention(z, ret_w // RET_HEADS, log_gamma, (state_ret, i), dm.bs, dm.ls,
                                 dm.tc // dm.ls, "retention_lat")
            ops = tuple(op[i] for op in s5_ops)
            y, f_re, f_im = _s5(dm, u, ops, s5_d[i], state_s5_re[:, i], state_s5_im[:, i])
            s5_out = _glu(dm, y, s5_w_glu[i].astype(BF16))
            w_out = even_w_out[i].astype(BF16)
            x = _matmul_residual(dm, x, mods, layer, 2, [(ro_c, ro_s), s5_out], [w_out[:ret_w], w_out[ret_w:]],
                                 tn=d // 2, name="even_out_proj")
            rets.append(sfin)
            s5r.append(f_re)
            s5i.append(f_im)
        else:
            j = layer // 2
            w_in = mla_w_in[j]
            w_in_ext = jnp.concatenate([w_in, _swap_halves_cols(w_in[:, q_lora + kv_lora:])], axis=1).astype(BF16)
            a = _norm_mod_matmul(dm, x, g_mix, mods, layer, 0, w_in_ext, F32, tn=w_in_ext.shape[1],
                                 name="mla_in_proj")
            w_uq = mla_w_uq[j].reshape(q_lora, MLA_HEADS, QK_NOPE + QK_ROPE)
            w_uq_ext = jnp.concatenate([w_uq, _swap_halves_cols(w_uq[..., QK_NOPE:])], axis=-1)
            w_uq_ext = w_uq_ext.reshape(q_lora, MLA_HEADS * QK_PAD).astype(BF16)
            q, ckv_n, kr = _mla_q(dm, a, mla_q_norm[j].reshape(1, q_lora), mla_kv_norm[j].reshape(1, kv_lora),
                                  ct, st, w_uq_ext, q_lora, kv_lora)
            lk = past + dm.ls
            w_ukv = mla_w_ukv[j].reshape(kv_lora, MLA_HEADS, QK_NOPE + V_HEAD)
            w_uk = w_ukv[..., :QK_NOPE].reshape(kv_lora, MLA_HEADS * QK_NOPE).astype(BF16)
            w_uv = w_ukv[..., QK_NOPE:].reshape(kv_lora, MLA_HEADS * V_HEAD).astype(BF16)
            k_all, v_all = _kv_expand(dm, cache_ckv, cache_krope, j, ckv_n, kr, w_uk, w_uv)
            o_c = _attention(q, k_all, v_all, dm.bc, dm.lc, dm.lc, dm.lc, MLA_HEADS, 0, dm.bs * lk, "attn_ctx")
            o_s = _attention(q, k_all, v_all, dm.bs, dm.ls, lk, min(TQ_MAX, dm.ls), LAT_HEADS_PER_STEP, dm.tc, 0,
                             "attn_lat")
            x = _matmul_residual(dm, x, mods, layer, 2, [(o_c, o_s)], [mla_w_out[j].astype(BF16)], tn=d // 2,
                                 name="mla_out_proj")
            ckvs.append(ckv_n[:dm.tc].reshape(dm.bc, dm.lc, kv_lora))
            krs.append(kr[:dm.tc, :QK_ROPE].reshape(dm.bc, dm.lc, QK_ROPE))
        x = _moe(dm, x, g_ffn, mods, layer, rw_t, rb, moe_w_gate, moe_w_up, moe_w_down)

    y_prompt = _final_norm(dm, x, norm_final, 0, dm.tc).reshape(dm.bc, dm.lc, d)
    y_sample = _final_norm(dm, x, norm_final, dm.tc, dm.ts).reshape(dm.bs, dm.ls, d)
    return (y_prompt, y_sample, jnp.stack(rets, axis=1), jnp.stack(s5r, axis=1), jnp.stack(s5i, axis=1),
            jnp.stack(ckvs, axis=1), jnp.stack(krs, axis=1))
```
